```python
import math
import jax, jax.numpy as jnp
from jax import lax
import numpy as np

D_MODEL = 1024
BATCH = 4
SEQ = 4096
DEPTH = 2
DEC_BATCH = 128
DEC_SEQ = 8
PAST_LEN = 2048
PAGE_SIZE = 128

N_MIXERS = 2
N_A_LAYERS = (DEPTH + 1) // 2
N_B_LAYERS = DEPTH // 2
GLA_HEADS = 4
GLA_DK = D_MODEL // 2 // GLA_HEADS
GLA_DV = D_MODEL // GLA_HEADS
GLA_LOWRANK = 16
GLA_TAU = 16.0
GLA_CHUNK = 32
GLA_IN = 2 * GLA_HEADS * GLA_DK + 2 * GLA_HEADS * GLA_DV + GLA_LOWRANK
DIFF_HEADS = 8
DIFF_DH = D_MODEL // DIFF_HEADS // 2
DIFF_VD = 2 * DIFF_DH
DIFF_WIDTH = DIFF_HEADS * DIFF_VD
Q_BLOCK = 128
N_BUCKETS = 32
MAX_DISTANCE = 128
PLE_DIM = 256
ALPHA = (2 * DEPTH) ** 0.25
BETA = (8 * DEPTH) ** -0.25
N_PAGES = PAST_LEN // PAGE_SIZE
N_POOL = (DEC_BATCH * N_PAGES * 5) // 4
EPS = 1e-5

kernel_name = 'hybrid_gla_diffattn_decoder'

F32 = jnp.float32


def layer_norm(x, g, b):
    xf = x.astype(F32)
    mu = jnp.mean(xf, axis=-1, keepdims=True)
    var = jnp.mean(jnp.square(xf - mu), axis=-1, keepdims=True)
    return ((xf - mu) * lax.rsqrt(var + EPS) * g.astype(F32) + b.astype(F32)).astype(x.dtype)


def head_rmsnorm(o, g):
    of = o.astype(F32)
    of = of * lax.rsqrt(jnp.mean(jnp.square(of), axis=-1, keepdims=True) + EPS)
    return of * g.astype(F32).reshape(o.shape[-2:])


def gla_recurrence(q, k, v, log_a, s0):
    bsz, t, h, _ = q.shape
    dv = v.shape[-1]
    c = math.gcd(t, GLA_CHUNK)
    n = t // c

    def chunks(a):
        return a.astype(F32).reshape(bsz, n, c, h, a.shape[-1])

    q, k, v, log_a = chunks(q), chunks(k), chunks(v), chunks(log_a)
    cum = jnp.cumsum(log_a, axis=2)
    tot = cum[:, :, -1]
    q_dec = q * jnp.exp(cum)
    k_inv = k * jnp.exp(-cum)
    k_end = k * jnp.exp(tot[:, :, None] - cum)
    causal = jnp.tril(jnp.ones((c, c), dtype=bool))
    att = jnp.einsum('bnihd,bnjhd->bnhij', q_dec, k_inv)
    att = jnp.where(causal, att, 0.0)
    o_intra = jnp.einsum('bnhij,bnjhv->bnihv', att, v)

    def step(s, xs):
        qn, kn, vn, tn = xs
        o = jnp.einsum('bihd,bhdv->bihv', qn, s)
        s = jnp.exp(tn)[..., None] * s + jnp.einsum('bjhd,bjhv->bhdv', kn, vn)
        return s, o

    xs = (jnp.moveaxis(q_dec, 1, 0), jnp.moveaxis(k_end, 1, 0), jnp.moveaxis(v, 1, 0), jnp.moveaxis(tot, 1, 0))
    s_final, o_inter = lax.scan(step, s0.astype(F32), xs)
    o = o_intra + jnp.moveaxis(o_inter, 0, 1)
    return o.reshape(bsz, t, h, dv), s_final


def gla_mixer(x, s0, w_in, w_a2, b_a, norm_g, w_out):
    bsz, t, _ = x.shape
    hk = GLA_HEADS * GLA_DK
    hv = GLA_HEADS * GLA_DV
    proj = x @ w_in
    q, k, v, gate, a_lr = jnp.split(proj, [hk, 2 * hk, 2 * hk + hv, 2 * hk + 2 * hv], axis=-1)
    q = q.reshape(bsz, t, GLA_HEADS, GLA_DK) * (GLA_DK ** -0.5)
    k = k.reshape(bsz, t, GLA_HEADS, GLA_DK)
    v = v.reshape(bsz, t, GLA_HEADS, GLA_DV)
    log_a = jax.nn.log_sigmoid((a_lr @ w_a2 + b_a).astype(F32)) / GLA_TAU
    log_a = log_a.reshape(bsz, t, GLA_HEADS, GLA_DK)
    o, s = gla_recurrence(q, k, v, log_a, s0)
    o = head_rmsnorm(o, norm_g).reshape(bsz, t, hv) * jax.nn.silu(gate.astype(F32))
    return o.astype(x.dtype) @ w_out, s


def t5_bucket(rel):
    n = jnp.maximum(rel, 0)
    max_exact = N_BUCKETS // 2
    nf = jnp.maximum(n, 1).astype(F32)
    large = max_exact + (jnp.log(nf / max_exact) / math.log(MAX_DISTANCE / max_exact)
                         * (N_BUCKETS - max_exact)).astype(jnp.int32)
    large = jnp.minimum(large, N_BUCKETS - 1)
    return jnp.where(n < max_exact, n, large)


def diff_lambda(lq1, lk1, lq2, lk2, lam_init):
    return (jnp.exp(jnp.sum(lq1.astype(F32) * lk1.astype(F32)))
            - jnp.exp(jnp.sum(lq2.astype(F32) * lk2.astype(F32))) + lam_init)


def diff_attend(q, k, v, q_pos, k_pos, rel_bias, lam):
    bsz, tq, h, _ = q.shape
    tk = k.shape[1]
    qs = q.reshape(bsz, tq, h, 2, DIFF_DH)
    ks = k.reshape(bsz, tk, h, 2, DIFF_DH)
    s = jnp.einsum('bqhmd,bkhmd->bhmqk', qs, ks).astype(F32) * (DIFF_DH ** -0.5)
    rel = q_pos[:, None] - k_pos[None, :]
    bias = jnp.moveaxis(rel_bias.astype(F32)[t5_bucket(rel)], -1, 0)
    s = s + bias[None, :, None]
    s = jnp.where(rel >= 0, s, jnp.finfo(F32).min)
    p = jax.nn.softmax(s, axis=-1)
    a = p[:, :, 0] - lam * p[:, :, 1]
    return jnp.einsum('bhqk,bkhd->bqhd', a, v.astype(F32))


def diff_project(x, w_in):
    bsz, t, _ = x.shape
    q, k, v, gate = jnp.split(x @ w_in, 4, axis=-1)
    shp = (bsz, t, DIFF_HEADS, DIFF_VD)
    return q.reshape(shp), k.reshape(shp), v.reshape(shp), gate


def diff_output(o, gate, norm_g, lam_init, w_out, dtype):
    bsz, t = o.shape[:2]
    o = head_rmsnorm(o, norm_g) * (1.0 - lam_init)
    o = o.reshape(bsz, t, DIFF_WIDTH) * jax.nn.silu(gate.astype(F32))
    return o.astype(dtype) @ w_out


def diff_prompt(x, w_in, norm_g, w_out, rel_bias, lam, lam_init):
    bsz, t, _ = x.shape
    q, k, v, gate = diff_project(x, w_in)
    qb = math.gcd(t, Q_BLOCK)
    nb = t // qb
    pos = jnp.arange(t, dtype=jnp.int32)
    q_blocks = jnp.moveaxis(q.reshape(bsz, nb, qb, DIFF_HEADS, DIFF_VD), 1, 0)
    pos_blocks = pos.reshape(nb, qb)
    o = lax.map(lambda xs: diff_attend(xs[0], k, v, xs[1], pos, rel_bias, lam), (q_blocks, pos_blocks))
    o = jnp.moveaxis(o, 0, 1).reshape(bsz, t, DIFF_HEADS, DIFF_VD)
    return diff_output(o, gate, norm_g, lam_init, w_out, x.dtype), k, v


def diff_sample(x, cache_k, cache_v, j, page_table, w_in, norm_g, w_out, rel_bias, lam, lam_init):
    bsz, t, _ = x.shape
    q, k, v, gate = diff_project(x, w_in)
    past = page_table.shape[1] * PAGE_SIZE
    k_past = cache_k[j, page_table].reshape(bsz, past, DIFF_HEADS, DIFF_VD)
    v_past = cache_v[j, page_table].reshape(bsz, past, DIFF_HEADS, DIFF_VD)
    k_all = jnp.concatenate([k_past, k.astype(k_past.dtype)], axis=1)
    v_all = jnp.concatenate([v_past, v.astype(v_past.dtype)], axis=1)
    q_pos = past + jnp.arange(t, dtype=jnp.int32)
    k_pos = jnp.arange(past + t, dtype=jnp.int32)
    o = diff_attend(q, k_all, v_all, q_pos, k_pos, rel_bias, lam)
    return diff_output(o, gate, norm_g, lam_init, w_out, x.dtype), k, v


def residual_update(x, f, g, b, p, w_proj, w_gate):
    h = layer_norm(ALPHA * x + f, g, b)
    gate = jax.nn.sigmoid((h @ w_gate).astype(F32))
    return (h.astype(F32) + gate * (p @ w_proj).astype(F32)).astype(x.dtype)


def setup_inputs(seed: int = 0) -> dict:
    key = jax.random.key(seed)
    ks = jax.random.split(key, 26)
    nrm = jax.random.normal
    hk = GLA_HEADS * GLA_DK
    hv = GLA_HEADS * GLA_DV
    perm = jax.random.permutation(ks[5], N_POOL)[:DEC_BATCH * N_PAGES]
    return {
        'x_prompt': nrm(ks[0], (BATCH, SEQ, D_MODEL), F32),
        'x_sample': nrm(ks[1], (DEC_BATCH, DEC_SEQ, D_MODEL), F32),
        'state_gla': 0.5 * nrm(ks[2], (N_A_LAYERS, DEC_BATCH, GLA_HEADS, GLA_DK, GLA_DV), F32),
        'cache_k': nrm(ks[3], (N_B_LAYERS, N_POOL, PAGE_SIZE, DIFF_HEADS, DIFF_VD), F32),
        'cache_v': nrm(ks[4], (N_B_LAYERS, N_POOL, PAGE_SIZE, DIFF_HEADS, DIFF_VD), F32),
        'page_table': perm.reshape(DEC_BATCH, N_PAGES).astype(jnp.int32),
        'p_prompt': nrm(ks[6], (DEPTH, BATCH, SEQ, PLE_DIM), F32),
        'p_sample': nrm(ks[7], (DEPTH, DEC_BATCH, DEC_SEQ, PLE_DIM), F32),
        'rel_bias': 0.5 * nrm(ks[8], (N_BUCKETS, DIFF_HEADS), F32),
        'gla_w_in': nrm(ks[9], (N_A_LAYERS, D_MODEL, GLA_IN), F32) * D_MODEL ** -0.5,
        'gla_w_a2': nrm(ks[10], (N_A_LAYERS, GLA_LOWRANK, hk), F32) * GLA_LOWRANK ** -0.5,
        'gla_b_a': 0.1 * nrm(ks[11], (N_A_LAYERS, hk), F32),
        'gla_norm_g': 1.0 + 0.02 * nrm(ks[12], (N_A_LAYERS, hv), F32),
        'gla_w_out': nrm(ks[13], (N_A_LAYERS, hv, D_MODEL), F32) * (hv ** -0.5 * BETA),
        'diff_w_in': nrm(ks[14], (N_B_LAYERS, D_MODEL, 4 * DIFF_WIDTH), F32) * D_MODEL ** -0.5,
        'diff_lam_q1': 0.1 * nrm(ks[15], (N_B_LAYERS, DIFF_DH), F32),
        'diff_lam_k1': 0.1 * nrm(ks[16], (N_B_LAYERS, DIFF_DH), F32),
        'diff_lam_q2': 0.1 * nrm(ks[17], (N_B_LAYERS, DIFF_DH), F32),
        'diff_lam_k2': 0.1 * nrm(ks[18], (N_B_LAYERS, DIFF_DH), F32),
        'diff_norm_g': 1.0 + 0.02 * nrm(ks[19], (N_B_LAYERS, DIFF_WIDTH), F32),
        'diff_w_out': nrm(ks[20], (N_B_LAYERS, DIFF_WIDTH, D_MODEL), F32) * (DIFF_WIDTH ** -0.5 * BETA),
        'ln_g': 1.0 + 0.02 * nrm(ks[21], (DEPTH, D_MODEL), F32),
        'ln_b': 0.02 * nrm(ks[22], (DEPTH, D_MODEL), F32),
        'ple_w_proj': nrm(ks[23], (DEPTH, PLE_DIM, D_MODEL), F32) * PLE_DIM ** -0.5,
        'ple_w_gate': nrm(ks[24], (DEPTH, D_MODEL, D_MODEL), F32) * D_MODEL ** -0.5,
    }


def reference(x_prompt, x_sample, state_gla, cache_k, cache_v, page_table, p_prompt, p_sample,
              rel_bias, gla_w_in, gla_w_a2, gla_b_a, gla_norm_g, gla_w_out,
              diff_w_in, diff_lam_q1, diff_lam_k1, diff_lam_q2, diff_lam_k2, diff_norm_g, diff_w_out,
              ln_g, ln_b, ple_w_proj, ple_w_gate):
    xp, xs = x_prompt, x_sample
    gla_sp, gla_ss, k_p, v_p, k_s, v_s = [], [], [], [], [], []
    for i in range(DEPTH):
        j = i // N_MIXERS
        if i % N_MIXERS == 0:
            s_zero = jnp.zeros((xp.shape[0], GLA_HEADS, GLA_DK, GLA_DV), F32)
            fp, sp = gla_mixer(xp, s_zero, gla_w_in[j], gla_w_a2[j], gla_b_a[j], gla_norm_g[j], gla_w_out[j])
            fs, ss = gla_mixer(xs, state_gla[j], gla_w_in[j], gla_w_a2[j], gla_b_a[j], gla_norm_g[j], gla_w_out[j])
            gla_sp.append(sp.astype(state_gla.dtype))
            gla_ss.append(ss.astype(state_gla.dtype))
        else:
            lam_init = 0.8 - 0.6 * math.exp(-0.3 * i)
            lam = diff_lambda(diff_lam_q1[j], diff_lam_k1[j], diff_lam_q2[j], diff_lam_k2[j], lam_init)
            fp, kn, vn = diff_prompt(xp, diff_w_in[j], diff_norm_g[j], diff_w_out[j], rel_bias, lam, lam_init)
            fs, kn2, vn2 = diff_sample(xs, cache_k, cache_v, j, page_table, diff_w_in[j], diff_norm_g[j],
                                       diff_w_out[j], rel_bias, lam, lam_init)
            k_p.append(kn.astype(cache_k.dtype))
            v_p.append(vn.astype(cache_v.dtype))
            k_s.append(kn2.astype(cache_k.dtype))
            v_s.append(vn2.astype(cache_v.dtype))
        xp = residual_update(xp, fp, ln_g[i], ln_b[i], p_prompt[i], ple_w_proj[i], ple_w_gate[i])
        xs = residual_update(xs, fs, ln_g[i], ln_b[i], p_sample[i], ple_w_proj[i], ple_w_gate[i])
    return (xp, xs, jnp.stack(gla_sp), jnp.stack(gla_ss), jnp.stack(k_p), jnp.stack(v_p), jnp.stack(k_s), jnp.stack(v_s))
```

```python
import functools
import math

import jax
import jax.numpy as jnp
from jax import lax
from jax.experimental import pallas as pl
from jax.experimental.pallas import tpu as pltpu

F32 = jnp.float32
BF16 = jnp.bfloat16

D_MODEL = 1024
DEPTH = 2
GLA_HEADS = 4
GLA_DK = 128
GLA_DV = 256
GLA_HK = GLA_HEADS * GLA_DK
GLA_HV = GLA_HEADS * GLA_DV
GLA_LOWRANK = 16
GLA_TAU = 16.0
DIFF_HEADS = 8
DIFF_DH = 64
DIFF_VD = 128
DIFF_WIDTH = DIFF_HEADS * DIFF_VD
N_BUCKETS = 32
MAX_DISTANCE = 128
PLE_DIM = 256
PAGE_SIZE = 128
ALPHA = (2 * DEPTH) ** 0.25
EPS = 1e-5

LANES = 128
GLA_CHUNK = 64
VMEM_LIMIT = 56 * 1024 * 1024
MASK_VALUE = -1e30


def _dot(a, b):
    return jnp.dot(a, b, preferred_element_type=F32)


def _dot_tb(a, b):
    return lax.dot_general(a, b, (((1,), (1,)), ((), ())), preferred_element_type=F32)


def _dot_ta(a, b):
    return lax.dot_general(a, b, (((0,), (0,)), ((), ())), preferred_element_type=F32)


def _split_bf16(x):
    hi = x.astype(BF16)
    mid = (x - hi.astype(F32)).astype(BF16)
    return hi, mid


def _log_sigmoid(z):
    return jnp.minimum(z, 0.0) - jnp.log(1.0 + jnp.exp(-jnp.abs(z)))


def _sigmoid(z):
    return 1.0 / (1.0 + jnp.exp(-z))


def _head_rmsnorm_gate(o, gate, norm_g, n_heads, head_dim, scale):
    parts = []
    for h in range(n_heads):
        oh = o[:, h * head_dim:(h + 1) * head_dim]
        ms = jnp.mean(oh * oh, axis=-1, keepdims=True)
        parts.append(oh * lax.rsqrt(ms + EPS))
    on = jnp.concatenate(parts, axis=-1) * norm_g
    if scale != 1.0:
        on = on * scale
    return on * (gate * _sigmoid(gate))


def _residual_update(x, f, p, ln_g, ln_b, w_gate, w_proj):
    hp = ALPHA * x + f
    mu = jnp.mean(hp, axis=-1, keepdims=True)
    hc = hp - mu
    var = jnp.mean(hc * hc, axis=-1, keepdims=True)
    h = hc * lax.rsqrt(var + EPS) * ln_g + ln_b
    gate = _sigmoid(_dot(h.astype(BF16), w_gate))
    return h + gate * _dot(p.astype(BF16), w_proj)


def _gla_project(xb, wq, wk, wv, wg, wa, wa2, ba):
    q = _dot(xb, wq) * (GLA_DK ** -0.5)
    k = _dot(xb, wk)
    v = _dot(xb, wv)
    g = _dot(xb, wg)
    a_lr = _dot(xb, wa)
    z = _dot(a_lr.astype(BF16), wa2) + ba
    log_a = _log_sigmoid(z) * (1.0 / GLA_TAU)
    return q, k, v, g, log_a


def _chunk_decay_matrices(rows, chunk):
    half = chunk // 2
    i = lax.broadcasted_iota(jnp.int32, (rows, rows), 0)
    j = lax.broadcasted_iota(jnp.int32, (rows, rows), 1)
    same = (i // chunk) == (j // chunk)
    jl = j % chunk
    il = i % chunk
    pos = same & (jl >= half) & (jl <= il)
    neg = same & (jl < half) & (jl > il)
    rel = jnp.where(pos, 1.0, jnp.where(neg, -1.0, 0.0)).astype(BF16)
    n_sel = max(8, 2 * rows // chunk)
    s = lax.broadcasted_iota(jnp.int32, (n_sel, rows), 0)
    t = lax.broadcasted_iota(jnp.int32, (n_sel, rows), 1)
    halves = jnp.where((t // half) == s, 1.0, 0.0).astype(BF16)
    return rel, halves


def _gla_prompt_kernel(x_ref, p_ref, wq_ref, wk_ref, wv_ref, wg_ref, wa_ref, wa2_ref, ba_ref, ng_ref,
                       wout_ref, lng_ref, lnb_ref, wgate_ref, wproj_ref,
                       y_ref, st_ref, s_scr, o_scr, *, chunk):
    blk = pl.program_id(1)
    rows = x_ref.shape[1]

    @pl.when(blk == 0)
    def _():
        s_scr[...] = jnp.zeros_like(s_scr)

    x = x_ref[0]
    xb = x.astype(BF16)
    q, k, v, g, log_a = _gla_project(xb, wq_ref[...], wk_ref[...], wv_ref[...], wg_ref[...],
                                     wa_ref[...], wa2_ref[...], ba_ref[...])
    rel, halves = _chunk_decay_matrices(rows, chunk)
    la_hi, la_mid = _split_bf16(log_a)
    d = _dot(rel, la_hi) + _dot(rel, la_mid)
    hs = _dot(halves, la_hi) + _dot(halves, la_mid)
    ehs = jnp.exp(hs)
    q_dec = (q * jnp.exp(d)).astype(BF16)
    k_inv = (k * jnp.exp(-d)).astype(BF16)
    vb = v.astype(BF16)

    ci = lax.broadcasted_iota(jnp.int32, (chunk, chunk), 0)
    cj = lax.broadcasted_iota(jnp.int32, (chunk, chunk), 1)
    causal = ci >= cj
    for c in range(rows // chunk):
        r0 = c * chunk
        for h in range(GLA_HEADS):
            ks = slice(h * GLA_DK, (h + 1) * GLA_DK)
            vs = slice(h * GLA_DV, (h + 1) * GLA_DV)
            qd = q_dec[r0:r0 + chunk, ks]
            ki = k_inv[r0:r0 + chunk, ks]
            vh = vb[r0:r0 + chunk, vs]
            e_first = ehs[2 * c:2 * c + 1, ks]
            e_second = ehs[2 * c + 1:2 * c + 2, ks]
            s_mid = s_scr[h] * e_first
            att = jnp.where(causal, _dot_tb(qd, ki), 0.0).astype(BF16)
            o_scr[r0:r0 + chunk, vs] = _dot(att, vh) + _dot_tb(qd, s_mid.astype(BF16))
            s_scr[h] = (s_mid + _dot_ta(vh, ki)) * e_second

    on = _head_rmsnorm_gate(o_scr[...], g, ng_ref[...], GLA_HEADS, GLA_DV, 1.0)
    f = _dot(on.astype(BF16), wout_ref[...])
    y_ref[0] = _residual_update(x, f, p_ref[0], lng_ref[...], lnb_ref[...], wgate_ref[...], wproj_ref[...])

    @pl.when(blk == pl.num_programs(1) - 1)
    def _():
        for h in range(GLA_HEADS):
            st_ref[0, h] = s_scr[h].T


def _const_spec(shape):
    nd = len(shape)
    return pl.BlockSpec(shape, lambda *_: (0,) * nd)


def _gla_weights(w_in, w_a2, b_a, norm_g, w_out, ln_g, ln_b, w_gate, w_proj):
    hk, hv = GLA_HK, GLA_HV
    wq = w_in[:, :hk].astype(BF16)
    wk = w_in[:, hk:2 * hk].astype(BF16)
    wv = w_in[:, 2 * hk:2 * hk + hv].astype(BF16)
    wg = w_in[:, 2 * hk + hv:2 * hk + 2 * hv].astype(BF16)
    wa = jnp.pad(w_in[:, 2 * hk + 2 * hv:], ((0, 0), (0, LANES - GLA_LOWRANK))).astype(BF16)
    wa2 = jnp.pad(w_a2, ((0, LANES - GLA_LOWRANK), (0, 0))).astype(BF16)
    return (wq, wk, wv, wg, wa, wa2, b_a.reshape(1, hk), norm_g.reshape(1, hv),
            w_out.astype(BF16), ln_g.reshape(1, D_MODEL), ln_b.reshape(1, D_MODEL),
            w_gate.astype(BF16), w_proj.astype(BF16))


def gla_layer_prompt(x, p, weights, *, block=256, chunk=GLA_CHUNK):
    bsz, t, _ = x.shape
    assert t % block == 0 and block % chunk == 0
    in_specs = [pl.BlockSpec((1, block, D_MODEL), lambda b, i: (b, i, 0)),
                pl.BlockSpec((1, block, PLE_DIM), lambda b, i: (b, i, 0))]
    in_specs += [_const_spec(w.shape) for w in weights]
    return pl.pallas_call(
        functools.partial(_gla_prompt_kernel, chunk=chunk),
        grid=(bsz, t // block),
        in_specs=in_specs,
        out_specs=[pl.BlockSpec((1, block, D_MODEL), lambda b, i: (b, i, 0)),
                   pl.BlockSpec((1, GLA_HEADS, GLA_DK, GLA_DV), lambda b, i: (b, 0, 0, 0))],
        out_shape=[jax.ShapeDtypeStruct((bsz, t, D_MODEL), F32),
                   jax.ShapeDtypeStruct((bsz, GLA_HEADS, GLA_DK, GLA_DV), F32)],
        scratch_shapes=[pltpu.VMEM((GLA_HEADS, GLA_DV, GLA_DK), F32),
                        pltpu.VMEM((block, GLA_HV), F32)],
        compiler_params=pltpu.CompilerParams(dimension_semantics=("arbitrary", "arbitrary"),
                                             vmem_limit_bytes=VMEM_LIMIT),
        name="gla_prompt",
    )(x, p, *weights)


def _gla_sample_kernel(x_ref, p_ref, s0_ref, wq_ref, wk_ref, wv_ref, wg_ref, wa_ref, wa2_ref, ba_ref, ng_ref,
                       wout_ref, lng_ref, lnb_ref, wgate_ref, wproj_ref,
                       y_ref, st_ref, o_scr, *, seq):
    rows = x_ref.shape[0]
    n_seq = rows // seq
    half = seq // 2
    x = x_ref[...]
    xb = x.astype(BF16)
    q, k, v, g, log_a = _gla_project(xb, wq_ref[...], wk_ref[...], wv_ref[...], wg_ref[...],
                                     wa_ref[...], wa2_ref[...], ba_ref[...])
    rel, _ = _chunk_decay_matrices(rows, seq)
    la_hi, la_mid = _split_bf16(log_a)
    d = _dot(rel, la_hi) + _dot(rel, la_mid)
    s_i = lax.broadcasted_iota(jnp.int32, (n_seq, rows), 0)
    t_i = lax.broadcasted_iota(jnp.int32, (n_seq, rows), 1)
    in_seq = (t_i // seq) == s_i
    sel_first = jnp.where(in_seq & ((t_i % seq) < half), 1.0, 0.0).astype(BF16)
    sel_second = jnp.where(in_seq & ((t_i % seq) >= half), 1.0, 0.0).astype(BF16)
    t_c = lax.broadcasted_iota(jnp.int32, (rows, n_seq), 0)
    s_c = lax.broadcasted_iota(jnp.int32, (rows, n_seq), 1)
    sel_tot_t = jnp.where((t_c // seq) == s_c, 1.0, 0.0).astype(BF16)
    e_first = jnp.exp(_dot(sel_first, la_hi) + _dot(sel_first, la_mid))
    e_second = jnp.exp(_dot(sel_second, la_hi) + _dot(sel_second, la_mid))
    e_tot_col = jnp.exp(_dot_ta(la_hi, sel_tot_t) + _dot_ta(la_mid, sel_tot_t))
    q_dec = q * jnp.exp(d)
    k_inv = k * jnp.exp(-d)

    ci = lax.broadcasted_iota(jnp.int32, (seq, seq), 0)
    cj = lax.broadcasted_iota(jnp.int32, (seq, seq), 1)
    causal = ci >= cj
    for c in range(n_seq):
        r0 = c * seq
        for h in range(GLA_HEADS):
            ks = slice(h * GLA_DK, (h + 1) * GLA_DK)
            vs = slice(h * GLA_DV, (h + 1) * GLA_DV)
            qd = q_dec[r0:r0 + seq, ks]
            ki = k_inv[r0:r0 + seq, ks]
            vh = v[r0:r0 + seq, vs].astype(BF16)
            s_old = s0_ref[c, h]
            att = jnp.where(causal, _dot_tb(qd.astype(BF16), ki.astype(BF16)), 0.0).astype(BF16)
            q_mid = (qd * e_first[c:c + 1, ks]).astype(BF16)
            o_scr[r0:r0 + seq, vs] = _dot(att, vh) + _dot(q_mid, s_old.astype(BF16))
            k_end = (ki * e_second[c:c + 1, ks]).astype(BF16)
            st_ref[c, h] = s_old * e_tot_col[h * GLA_DK:(h + 1) * GLA_DK, c:c + 1] + _dot_ta(k_end, vh)

    on = _head_rmsnorm_gate(o_scr[...], g, ng_ref[...], GLA_HEADS, GLA_DV, 1.0)
    f = _dot(on.astype(BF16), wout_ref[...])
    y_ref[...] = _residual_update(x, f, p_ref[...], lng_ref[...], lnb_ref[...], wgate_ref[...], wproj_ref[...])


def gla_layer_sample(x, p, s0, weights, *, group=8):
    n, seq, _ = x.shape
    assert n % group == 0 and seq % 2 == 0
    rows = group * seq
    in_specs = [pl.BlockSpec((rows, D_MODEL), lambda i: (i, 0)),
                pl.BlockSpec((rows, PLE_DIM), lambda i: (i, 0)),
                pl.BlockSpec((group, GLA_HEADS, GLA_DK, GLA_DV), lambda i: (i, 0, 0, 0))]
    in_specs += [_const_spec(w.shape) for w in weights]
    y, st = pl.pallas_call(
        functools.partial(_gla_sample_kernel, seq=seq),
        grid=(n // group,),
        in_specs=in_specs,
        out_specs=[pl.BlockSpec((rows, D_MODEL), lambda i: (i, 0)),
                   pl.BlockSpec((group, GLA_HEADS, GLA_DK, GLA_DV), lambda i: (i, 0, 0, 0))],
        out_shape=[jax.ShapeDtypeStruct((n * seq, D_MODEL), F32),
                   jax.ShapeDtypeStruct((n, GLA_HEADS, GLA_DK, GLA_DV), F32)],
        scratch_shapes=[pltpu.VMEM((rows, GLA_HV), F32)],
        compiler_params=pltpu.CompilerParams(dimension_semantics=("arbitrary",),
                                             vmem_limit_bytes=VMEM_LIMIT),
        name="gla_sample",
    )(x.reshape(n * seq, D_MODEL), p.reshape(n * seq, PLE_DIM), s0, *weights)
    return y.reshape(n, seq, D_MODEL), st


def _resident_spec(shape):
    nd = len(shape)
    return pl.BlockSpec(shape, lambda *_: (0,) * nd, pipeline_mode=pl.Buffered(1))


def _t5_bias(rel_bias, dist):
    max_exact = N_BUCKETS // 2
    n = jnp.maximum(dist, 0)
    nf = jnp.maximum(n, 1).astype(F32)
    large = max_exact + (jnp.log(nf / max_exact) / math.log(MAX_DISTANCE / max_exact)
                         * (N_BUCKETS - max_exact)).astype(jnp.int32)
    large = jnp.minimum(large, N_BUCKETS - 1)
    bucket = jnp.where(n < max_exact, n, large)[None]
    table = rel_bias.astype(F32)
    per_head = (DIFF_HEADS,) + (1,) * dist.ndim
    out = jnp.zeros((DIFF_HEADS,) + dist.shape, F32)
    for b in range(N_BUCKETS):
        out = jnp.where(bucket == b, table[b].reshape(per_head), out)
    return out


def _store_token_head_rows(ref, lead, x):
    tokens = x.shape[0]
    for h in range(DIFF_HEADS):
        ref[(*lead, pl.ds(h, tokens, stride=DIFF_HEADS), slice(None))] = x[:, h * DIFF_VD:(h + 1) * DIFF_VD]


def _load_token_head_rows(ref, lead, tokens):
    return jnp.concatenate(
        [ref[(*lead, pl.ds(h, tokens, stride=DIFF_HEADS), slice(None))] for h in range(DIFF_HEADS)], axis=1)


def _diff_lambda(lam_ref, lam_init):
    lv = lam_ref[...]
    a = jnp.sum(lv[0:1] * lv[1:2], axis=-1, keepdims=True)
    b = jnp.sum(lv[2:3] * lv[3:4], axis=-1, keepdims=True)
    return jnp.exp(a) - jnp.exp(b) + lam_init


def _diff_prompt_kernel(bfar_ref, x_ref, p_ref, wq_ref, wk_ref, wv_ref, wg_ref, lam_ref, bias_ref, ng_ref,
                        wout_ref, lng_ref, lnb_ref, wgate_ref, wproj_ref,
                        y_ref, kout_ref, vout_ref,
                        k_scr, v_scr, q_scr, o_scr, acc_scr, m_scr, l_scr, *, lam_init):
    i = pl.program_id(1)
    tq = x_ref.shape[1]
    x = x_ref[0]
    xb = x.astype(BF16)
    q = _dot(xb, wq_ref[...]) * (DIFF_DH ** -0.5)
    k = _dot(xb, wk_ref[...])
    v = _dot(xb, wv_ref[...])
    _store_token_head_rows(kout_ref, (0,), k)
    _store_token_head_rows(vout_ref, (0,), v)
    first_half = lax.broadcasted_iota(jnp.int32, (tq, DIFF_VD), 1) < DIFF_DH
    row0 = pl.multiple_of(i * tq, tq)
    for h in range(DIFF_HEADS):
        hs = slice(h * DIFF_VD, (h + 1) * DIFF_VD)
        qh = q[:, hs]
        q_scr[h, 0:tq, :] = jnp.where(first_half, qh, 0.0).astype(BF16)
        q_scr[h, tq:2 * tq, :] = jnp.where(first_half, 0.0, qh).astype(BF16)
        k_scr[h, pl.ds(row0, tq), :] = k[:, hs].astype(BF16)
        v_scr[h, pl.ds(row0, tq), :] = v[:, hs].astype(BF16)
    lam = _diff_lambda(lam_ref, lam_init)

    def head_body(h, carry):
        qp = q_scr[h]
        m_scr[...] = jnp.full(m_scr.shape, MASK_VALUE, F32)
        l_scr[...] = jnp.zeros(l_scr.shape, F32)
        acc_scr[...] = jnp.zeros(acc_scr.shape, F32)

        def block(j, bias):
            k0 = pl.multiple_of(j * tq, tq)
            kb = k_scr[h, pl.ds(k0, tq), :]
            vb = v_scr[h, pl.ds(k0, tq), :]
            s = _dot_tb(kb, qp) + bias
            m_old = m_scr[...]
            m_new = jnp.maximum(m_old, jnp.max(s, axis=0, keepdims=True))
            alpha = jnp.exp(m_old - m_new)
            pr = jnp.exp(s - m_new)
            l_scr[...] = alpha * l_scr[...] + jnp.sum(pr, axis=0, keepdims=True)
            acc_scr[...] = acc_scr[...] * alpha + _dot_ta(vb, pr.astype(BF16))
            m_scr[...] = m_new

        def far_block(j, c):
            block(j, bfar_ref[h])
            return c

        lax.fori_loop(0, jnp.maximum(i - 1, 0), far_block, 0)

        @pl.when(i > 0)
        def _():
            t = bias_ref[h, 1]
            block(i - 1, jnp.concatenate([t, t], axis=1))

        t = bias_ref[h, 0]
        block(i, jnp.concatenate([t, t], axis=1))

        inv_l = 1.0 / l_scr[...]
        acc = acc_scr[...]
        o_t = acc[:, :tq] * inv_l[:, :tq] - lam * (acc[:, tq:] * inv_l[:, tq:])
        o_scr[h] = o_t.T
        return carry

    lax.fori_loop(0, DIFF_HEADS, head_body, 0)
    o = jnp.concatenate([o_scr[h] for h in range(DIFF_HEADS)], axis=-1)
    gate = _dot(xb, wg_ref[...])
    on = _head_rmsnorm_gate(o, gate, ng_ref[...], DIFF_HEADS, DIFF_VD, 1.0 - lam_init)
    f = _dot(on.astype(BF16), wout_ref[...])
    y_ref[0] = _residual_update(x, f, p_ref[0], lng_ref[...], lnb_ref[...], wgate_ref[...], wproj_ref[...])


def _diff_weights(w_in, lam_q1, lam_k1, lam_q2, lam_k2, norm_g, w_out, ln_g, ln_b, w_gate, w_proj):
    w = DIFF_WIDTH
    return dict(
        wq=w_in[:, :w].astype(BF16), wk=w_in[:, w:2 * w].astype(BF16),
        wv=w_in[:, 2 * w:3 * w].astype(BF16), wg=w_in[:, 3 * w:].astype(BF16),
        lam=jnp.stack([lam_q1, lam_k1, lam_q2, lam_k2]).astype(F32),
        ng=norm_g.reshape(1, w), wout=w_out.astype(BF16),
        lng=ln_g.reshape(1, D_MODEL), lnb=ln_b.reshape(1, D_MODEL),
        wgate=w_gate.astype(BF16), wproj=w_proj.astype(BF16))


def diff_layer_prompt(x, p, dw, rel_bias, lam_init, *, block=256):
    bsz, t, _ = x.shape
    assert t % block == 0 and block >= MAX_DISTANCE
    kk = jnp.arange(block, dtype=jnp.int32)[:, None]
    qq = jnp.arange(block, dtype=jnp.int32)[None, :]
    dist = jnp.stack([qq - kk, block + qq - kk])
    bias = jnp.where(dist >= 0, _t5_bias(rel_bias, dist), MASK_VALUE)
    bfar = rel_bias.astype(F32)[N_BUCKETS - 1]
    tok = lambda width: pl.BlockSpec((1, block, width), lambda b, i, *_: (b, i, 0))
    kv_spec = pl.BlockSpec((1, block * DIFF_HEADS, DIFF_VD), lambda b, i, *_: (b, i, 0))
    consts = [dw["wq"], dw["wk"], dw["wv"], dw["wg"], dw["lam"], bias, dw["ng"], dw["wout"],
              dw["lng"], dw["lnb"], dw["wgate"], dw["wproj"]]
    grid_spec = pltpu.PrefetchScalarGridSpec(
        num_scalar_prefetch=1,
        grid=(bsz, t // block),
        in_specs=[tok(D_MODEL), tok(PLE_DIM)] + [_resident_spec(c.shape) for c in consts],
        out_specs=[tok(D_MODEL), kv_spec, kv_spec],
        scratch_shapes=[pltpu.VMEM((DIFF_HEADS, t, DIFF_VD), BF16),
                        pltpu.VMEM((DIFF_HEADS, t, DIFF_VD), BF16),
                        pltpu.VMEM((DIFF_HEADS, 2 * block, DIFF_VD), BF16),
                        pltpu.VMEM((DIFF_HEADS, block, DIFF_VD), F32),
                        pltpu.VMEM((DIFF_VD, 2 * block), F32),
                        pltpu.VMEM((1, 2 * block), F32),
                        pltpu.VMEM((1, 2 * block), F32)])
    return pl.pallas_call(
        functools.partial(_diff_prompt_kernel, lam_init=lam_init),
        grid_spec=grid_spec,
        out_shape=[jax.ShapeDtypeStruct((bsz, t, D_MODEL), F32),
                   jax.ShapeDtypeStruct((bsz, t * DIFF_HEADS, DIFF_VD), F32),
                   jax.ShapeDtypeStruct((bsz, t * DIFF_HEADS, DIFF_VD), F32)],
        compiler_params=pltpu.CompilerParams(dimension_semantics=("arbitrary", "arbitrary"),
                                             vmem_limit_bytes=VMEM_LIMIT),
        name="diff_prompt",
    )(bfar, x, p, *consts)


def _diff_sample_project_kernel(x_ref, wqt_ref, wk_ref, wv_ref, qt_ref, k_ref, v_ref):
    xb = x_ref[...].astype(BF16)
    qt_ref[...] = (_dot_tb(wqt_ref[...], xb) * (DIFF_DH ** -0.5)).astype(BF16)
    _store_token_head_rows(k_ref, (), _dot(xb, wk_ref[...]))
    _store_token_head_rows(v_ref, (), _dot(xb, wv_ref[...]))


def _row_to_col(row, n):
    eye = lax.broadcasted_iota(jnp.int32, (n, n), 0) == lax.broadcasted_iota(jnp.int32, (n, n), 1)
    return jnp.sum(jnp.where(eye, row, 0.0), axis=1, keepdims=True)


def _paged_attn_kernel(pt_ref, qt_ref, kn_ref, vn_ref, lam_ref, bias_ref, bnew_ref, *rest, pages, seq, lam_init):
    k_pages = rest[:pages]
    v_pages = rest[pages:2 * pages]
    o_ref = rest[2 * pages]
    wq_scr, acc_scr, m_scr, l_scr = rest[2 * pages + 1:]
    b = pl.program_id(0)
    g = pl.program_id(1)
    ncol = 2 * DIFF_HEADS * seq

    @pl.when(g == 0)
    def _():
        local = (b % (LANES // seq)) * seq
        src = lax.broadcasted_iota(jnp.int32, (LANES, ncol), 0)
        col = lax.broadcasted_iota(jnp.int32, (LANES, ncol), 1)
        pick = jnp.where(src == local + col % seq, 1.0, 0.0).astype(BF16)
        rep = _dot(qt_ref[...], pick)
        r = lax.broadcasted_iota(jnp.int32, (DIFF_WIDTH, ncol), 0)
        c = lax.broadcasted_iota(jnp.int32, (DIFF_WIDTH, ncol), 1)
        own = ((r // DIFF_VD) == ((c % (DIFF_HEADS * seq)) // seq)) & \
              (((r % DIFF_VD) // DIFF_DH) == (c // (DIFF_HEADS * seq)))
        wq_scr[...] = jnp.where(own, rep, 0.0).astype(BF16)
        m_scr[...] = jnp.full(m_scr.shape, MASK_VALUE, F32)
        l_scr[...] = jnp.zeros(l_scr.shape, F32)
        acc_scr[...] = jnp.zeros(acc_scr.shape, F32)

    def flash_step(scores, values):
        m_old = m_scr[...]
        m_new = m_old
        for s in scores:
            m_new = jnp.maximum(m_new, jnp.max(s, axis=0, keepdims=True))
        alpha = jnp.exp(m_old - m_new)
        l_new = alpha * l_scr[...]
        pv = None
        for s, val in zip(scores, values):
            pr = jnp.exp(s - m_new)
            l_new = l_new + jnp.sum(pr, axis=0, keepdims=True)
            t = _dot_ta(pr.astype(BF16), val)
            pv = t if pv is None else pv + t
        acc_scr[...] = acc_scr[...] * _row_to_col(alpha, ncol) + pv
        l_scr[...] = l_new
        m_scr[...] = m_new

    def load_page(ref):
        return _load_token_head_rows(ref, (0,), PAGE_SIZE).astype(BF16)

    wq = wq_scr[...]
    scores, values = [], []
    for i in range(pages):
        scores.append(_dot(load_page(k_pages[i]), wq) + bias_ref[0, i * PAGE_SIZE:(i + 1) * PAGE_SIZE, :])
        values.append(load_page(v_pages[i]))
    flash_step(scores, values)

    @pl.when(g == pl.num_programs(1) - 1)
    def _():
        pad = jnp.zeros((16 - seq, DIFF_WIDTH), F32)
        kn = jnp.concatenate([_load_token_head_rows(kn_ref, (0,), seq), pad], axis=0).astype(BF16)
        vn = jnp.concatenate([_load_token_head_rows(vn_ref, (0,), seq), pad], axis=0).astype(BF16)
        flash_step([_dot(kn, wq) + bnew_ref[...]], [vn])
        lam = _diff_lambda(lam_ref, lam_init)
        inv_l = _row_to_col(1.0 / l_scr[...], ncol)
        half_rows = DIFF_HEADS * seq
        for h in range(DIFF_HEADS):
            cs = slice(h * DIFF_VD, (h + 1) * DIFF_VD)
            r1 = slice(h * seq, (h + 1) * seq)
            r2 = slice(half_rows + h * seq, half_rows + (h + 1) * seq)
            o_ref[0, :, cs] = acc_scr[r1, cs] * inv_l[r1] - lam * (acc_scr[r2, cs] * inv_l[r2])


def _diff_sample_out_kernel(x_ref, o_ref, p_ref, wg_ref, ng_ref, wout_ref, lng_ref, lnb_ref, wgate_ref, wproj_ref,
                            y_ref, *, lam_init):
    x = x_ref[...]
    gate = _dot(x.astype(BF16), wg_ref[...])
    on = _head_rmsnorm_gate(o_ref[...], gate, ng_ref[...], DIFF_HEADS, DIFF_VD, 1.0 - lam_init)
    f = _dot(on.astype(BF16), wout_ref[...])
    y_ref[...] = _residual_update(x, f, p_ref[...], lng_ref[...], lnb_ref[...], wgate_ref[...], wproj_ref[...])


def diff_layer_sample(x, p, cache_k, cache_v, page_table, dw, rel_bias, lam_init, *, pages=8, block=256):
    n, seq, _ = x.shape
    n_pages = page_table.shape[1]
    rows = n * seq
    assert n_pages % pages == 0 and rows % block == 0 and LANES % seq == 0 and seq <= 16
    assert PAGE_SIZE >= MAX_DISTANCE
    x2 = x.reshape(rows, D_MODEL)
    row_spec = lambda width: pl.BlockSpec((block, width), lambda i: (i, 0))
    qt, kn, vn = pl.pallas_call(
        _diff_sample_project_kernel,
        grid=(rows // block,),
        in_specs=[row_spec(D_MODEL), _const_spec((DIFF_WIDTH, D_MODEL)),
                  _const_spec((D_MODEL, DIFF_WIDTH)), _const_spec((D_MODEL, DIFF_WIDTH))],
        out_specs=[pl.BlockSpec((DIFF_WIDTH, block), lambda i: (0, i)),
                   pl.BlockSpec((block * DIFF_HEADS, DIFF_VD), lambda i: (i, 0)),
                   pl.BlockSpec((block * DIFF_HEADS, DIFF_VD), lambda i: (i, 0))],
        out_shape=[jax.ShapeDtypeStruct((DIFF_WIDTH, rows), BF16),
                   jax.ShapeDtypeStruct((rows * DIFF_HEADS, DIFF_VD), F32),
                   jax.ShapeDtypeStruct((rows * DIFF_HEADS, DIFF_VD), F32)],
        compiler_params=pltpu.CompilerParams(dimension_semantics=("arbitrary",), vmem_limit_bytes=VMEM_LIMIT),
        name="diff_sample_project",
    )(x2, dw["wq"].T, dw["wk"], dw["wv"])

    ncol = 2 * DIFF_HEADS * seq
    group_keys = pages * PAGE_SIZE
    col = jnp.arange(ncol, dtype=jnp.int32)[None, :]
    col_h = (col % (DIFF_HEADS * seq)) // seq
    col_t = col % seq
    kk = jnp.arange(group_keys, dtype=jnp.int32)[:, None]
    tk = jnp.arange(16, dtype=jnp.int32)[:, None]
    dist_far = jnp.full((group_keys, ncol), MAX_DISTANCE, jnp.int32)
    dist_last = group_keys + col_t - kk
    dist_new = col_t - tk

    def own_head(per_head):
        return sum(jnp.where(col_h == h, per_head[h], 0.0) for h in range(DIFF_HEADS))

    bias_pages = jnp.stack([own_head(_t5_bias(rel_bias, dist_far)),
                            own_head(_t5_bias(rel_bias, dist_last))])
    bias_new = jnp.where((tk < seq) & (dist_new >= 0), own_head(_t5_bias(rel_bias, dist_new)), MASK_VALUE)

    n_groups = n_pages // pages
    seq_per_blk = LANES // seq

    def page_spec(i):
        return pl.BlockSpec((1, PAGE_SIZE * DIFF_HEADS, DIFF_VD),
                            lambda b, g, pt: (pt[b * n_pages + g * pages + i], 0, 0))

    tok_spec = pl.BlockSpec((1, seq, DIFF_WIDTH), lambda b, g, pt: (b, 0, 0))
    new_spec = pl.BlockSpec((1, seq * DIFF_HEADS, DIFF_VD), lambda b, g, pt: (b, 0, 0))
    grid_spec = pltpu.PrefetchScalarGridSpec(
        num_scalar_prefetch=1,
        grid=(n, n_groups),
        in_specs=[pl.BlockSpec((DIFF_WIDTH, LANES), lambda b, g, pt: (0, b // seq_per_blk)),
                  new_spec, new_spec,
                  pl.BlockSpec((4, DIFF_DH), lambda b, g, pt: (0, 0)),
                  pl.BlockSpec((1, pages * PAGE_SIZE, ncol), lambda b, g, pt: ((g + 1) // n_groups, 0, 0)),
                  pl.BlockSpec((16, ncol), lambda b, g, pt: (0, 0))]
                 + [page_spec(i) for i in range(pages)] + [page_spec(i) for i in range(pages)],
        out_specs=tok_spec,
        scratch_shapes=[pltpu.VMEM((DIFF_WIDTH, ncol), BF16),
                        pltpu.VMEM((ncol, DIFF_WIDTH), F32),
                        pltpu.VMEM((1, ncol), F32),
                        pltpu.VMEM((1, ncol), F32)])
    o = pl.pallas_call(
        functools.partial(_paged_attn_kernel, pages=pages, seq=seq, lam_init=lam_init),
        grid_spec=grid_spec,
        out_shape=jax.ShapeDtypeStruct((n, seq, DIFF_WIDTH), F32),
        compiler_params=pltpu.CompilerParams(dimension_semantics=("arbitrary", "arbitrary"),
                                             vmem_limit_bytes=VMEM_LIMIT),
        name="diff_sample_attn",
    )(page_table.reshape(-1), qt, kn.reshape(n, seq * DIFF_HEADS, DIFF_VD), vn.reshape(n, seq * DIFF_HEADS, DIFF_VD),
      dw["lam"], bias_pages, bias_new, *([cache_k] * pages), *([cache_v] * pages))

    consts = [dw["wg"], dw["ng"], dw["wout"], dw["lng"], dw["lnb"], dw["wgate"], dw["wproj"]]
    y = pl.pallas_call(
        functools.partial(_diff_sample_out_kernel, lam_init=lam_init),
        grid=(rows // block,),
        in_specs=[row_spec(D_MODEL), row_spec(DIFF_WIDTH), row_spec(PLE_DIM)] + [_const_spec(c.shape) for c in consts],
        out_specs=row_spec(D_MODEL),
        out_shape=jax.ShapeDtypeStruct((rows, D_MODEL), F32),
        compiler_params=pltpu.CompilerParams(dimension_semantics=("arbitrary",), vmem_limit_bytes=VMEM_LIMIT),
        name="diff_sample_out",
    )(x2, o.reshape(rows, DIFF_WIDTH), p.reshape(rows, PLE_DIM), *consts)
    return y.reshape(n, seq, D_MODEL), kn, vn


def kernel(x_prompt, x_sample, state_gla, cache_k, cache_v, page_table, p_prompt, p_sample, rel_bias,
           gla_w_in, gla_w_a2, gla_b_a, gla_norm_g, gla_w_out,
           diff_w_in, diff_lam_q1, diff_lam_k1, diff_lam_q2, diff_lam_k2, diff_norm_g, diff_w_out,
           ln_g, ln_b, ple_w_proj, ple_w_gate):
    w0 = _gla_weights(gla_w_in[0], gla_w_a2[0], gla_b_a[0], gla_norm_g[0], gla_w_out[0],
                      ln_g[0], ln_b[0], ple_w_gate[0], ple_w_proj[0])
    xp1, sp = gla_layer_prompt(x_prompt, p_prompt[0], w0)
    xs1, ss = gla_layer_sample(x_sample, p_sample[0], state_gla[0], w0)
    lam_init = 0.8 - 0.6 * math.exp(-0.3 * 1)
    dw = _diff_weights(diff_w_in[0], diff_lam_q1[0], diff_lam_k1[0], diff_lam_q2[0], diff_lam_k2[0],
                       diff_norm_g[0], diff_w_out[0], ln_g[1], ln_b[1], ple_w_gate[1], ple_w_proj[1])
    yp, kp, vp = diff_layer_prompt(xp1, p_prompt[1], dw, rel_bias, lam_init)
    pool = cache_k.shape[1]
    ys, ks, vs = diff_layer_sample(xs1, p_sample[1],
                                   cache_k[0].reshape(pool, PAGE_SIZE * DIFF_HEADS, DIFF_VD),
                                   cache_v[0].reshape(pool, PAGE_SIZE * DIFF_HEADS, DIFF_VD),
                                   page_table, dw, rel_bias, lam_init)
    bsz, t, _ = x_prompt.shape
    n, seq, _ = x_sample.shape
    heads = (DIFF_HEADS, DIFF_VD)
    return (yp, ys, sp[None], ss[None],
            kp.reshape(1, bsz, t, *heads), vp.reshape(1, bsz, t, *heads),
            ks.reshape(1, n, seq, *heads), vs.reshape(1, n, seq, *heads))
```

```python
import functools
import math

import jax
import jax.numpy as jnp
from jax import lax
from jax.experimental import pallas as pl
from jax.experimental.pallas import tpu as pltpu

F32 = jnp.float32
BF16 = jnp.bfloat16

D_MODEL = 1024
DEPTH = 2
GLA_HEADS = 4
GLA_DK = 128
GLA_DV = 256
GLA_HK = GLA_HEADS * GLA_DK
GLA_HV = GLA_HEADS * GLA_DV
GLA_LOWRANK = 16
GLA_TAU = 16.0
DIFF_HEADS = 8
DIFF_DH = 64
DIFF_VD = 128
DIFF_WIDTH = DIFF_HEADS * DIFF_VD
N_BUCKETS = 32
MAX_DISTANCE = 128
PLE_DIM = 256
PAGE_SIZE = 128
ALPHA = (2 * DEPTH) ** 0.25
EPS = 1e-5

LANES = 128
GLA_CHUNK = 64
VMEM_LIMIT = 56 * 1024 * 1024
MASK_VALUE = -1e30
LOG2_E = math.log2(math.e)


def _dot(a, b):
    return jnp.dot(a, b, preferred_element_type=F32)


def _dot_tb(a, b):
    return lax.dot_general(a, b, (((1,), (1,)), ((), ())), preferred_element_type=F32)


def _dot_ta(a, b):
    return lax.dot_general(a, b, (((0,), (0,)), ((), ())), preferred_element_type=F32)


def _split_bf16(x):
    hi = x.astype(BF16)
    mid = (x - hi.astype(F32)).astype(BF16)
    return hi, mid


def _log_sigmoid(z):
    return jnp.minimum(z, 0.0) - jnp.log(1.0 + jnp.exp(-jnp.abs(z)))


def _sigmoid(z):
    return 1.0 / (1.0 + jnp.exp(-z))


def _head_rmsnorm_gate(o, gate, norm_g, n_heads, head_dim, scale):
    parts = []
    for h in range(n_heads):
        oh = o[:, h * head_dim:(h + 1) * head_dim]
        ms = jnp.mean(oh * oh, axis=-1, keepdims=True)
        parts.append(oh * lax.rsqrt(ms + EPS))
    on = jnp.concatenate(parts, axis=-1) * norm_g
    if scale != 1.0:
        on = on * scale
    return on * (gate * _sigmoid(gate))


def _residual_update(x, f, p, ln_g, ln_b, w_gate, w_proj):
    hp = ALPHA * x + f
    mu = jnp.mean(hp, axis=-1, keepdims=True)
    hc = hp - mu
    var = jnp.mean(hc * hc, axis=-1, keepdims=True)
    h = hc * lax.rsqrt(var + EPS) * ln_g + ln_b
    gate = _sigmoid(_dot(h.astype(BF16), w_gate))
    return h + gate * _dot(p.astype(BF16), w_proj)


def _gla_project(xb, wq, wk, wv, wg, wa, wa2, ba):
    q = _dot(xb, wq) * (GLA_DK ** -0.5)
    k = _dot(xb, wk)
    v = _dot(xb, wv)
    g = _dot(xb, wg)
    a_lr = _dot(xb, wa)
    z = _dot(a_lr.astype(BF16), wa2) + ba
    log_a = _log_sigmoid(z) * (1.0 / GLA_TAU)
    return q, k, v, g, log_a


def _chunk_decay_matrices(rows, chunk):
    half = chunk // 2
    i = lax.broadcasted_iota(jnp.int32, (rows, rows), 0)
    j = lax.broadcasted_iota(jnp.int32, (rows, rows), 1)
    same = (i // chunk) == (j // chunk)
    jl = j % chunk
    il = i % chunk
    pos = same & (jl >= half) & (jl <= il)
    neg = same & (jl < half) & (jl > il)
    rel = jnp.where(pos, 1.0, jnp.where(neg, -1.0, 0.0)).astype(BF16)
    n_sel = max(8, 2 * rows // chunk)
    s = lax.broadcasted_iota(jnp.int32, (n_sel, rows), 0)
    t = lax.broadcasted_iota(jnp.int32, (n_sel, rows), 1)
    halves = jnp.where((t // half) == s, 1.0, 0.0).astype(BF16)
    return rel, halves


def _gla_prompt_kernel(x_ref, p_ref, wq_ref, wk_ref, wv_ref, wg_ref, wa_ref, wa2_ref, ba_ref, ng_ref,
                       wout_ref, lng_ref, lnb_ref, wgate_ref, wproj_ref,
                       y_ref, st_ref, s_scr, o_scr, *, chunk):
    blk = pl.program_id(1)
    rows = x_ref.shape[1]

    @pl.when(blk == 0)
    def _():
        s_scr[...] = jnp.zeros_like(s_scr)

    x = x_ref[0]
    xb = x.astype(BF16)
    q, k, v, g, log_a = _gla_project(xb, wq_ref[...], wk_ref[...], wv_ref[...], wg_ref[...],
                                     wa_ref[...], wa2_ref[...], ba_ref[...])
    rel, halves = _chunk_decay_matrices(rows, chunk)
    la_hi, la_mid = _split_bf16(log_a)
    d = _dot(rel, la_hi) + _dot(rel, la_mid)
    hs = _dot(halves, la_hi) + _dot(halves, la_mid)
    ehs = jnp.exp(hs)
    q_dec = (q * jnp.exp(d)).astype(BF16)
    k_inv = (k * jnp.exp(-d)).astype(BF16)
    vb = v.astype(BF16)

    ci = lax.broadcasted_iota(jnp.int32, (chunk, chunk), 0)
    cj = lax.broadcasted_iota(jnp.int32, (chunk, chunk), 1)
    causal = ci >= cj
    for c in range(rows // chunk):
        r0 = c * chunk
        for h in range(GLA_HEADS):
            ks = slice(h * GLA_DK, (h + 1) * GLA_DK)
            vs = slice(h * GLA_DV, (h + 1) * GLA_DV)
            qd = q_dec[r0:r0 + chunk, ks]
            ki = k_inv[r0:r0 + chunk, ks]
            vh = vb[r0:r0 + chunk, vs]
            e_first = ehs[2 * c:2 * c + 1, ks]
            e_second = ehs[2 * c + 1:2 * c + 2, ks]
            s_mid = s_scr[h] * e_first
            att = jnp.where(causal, _dot_tb(qd, ki), 0.0).astype(BF16)
            o_scr[r0:r0 + chunk, vs] = _dot(att, vh) + _dot_tb(qd, s_mid.astype(BF16))
            s_scr[h] = (s_mid + _dot_ta(vh, ki)) * e_second

    on = _head_rmsnorm_gate(o_scr[...], g, ng_ref[...], GLA_HEADS, GLA_DV, 1.0)
    f = _dot(on.astype(BF16), wout_ref[...])
    y_ref[0] = _residual_update(x, f, p_ref[0], lng_ref[...], lnb_ref[...], wgate_ref[...], wproj_ref[...])

    @pl.when(blk == pl.num_programs(1) - 1)
    def _():
        for h in range(GLA_HEADS):
            st_ref[0, h] = s_scr[h].T


def _const_spec(shape):
    nd = len(shape)
    return pl.BlockSpec(shape, lambda *_: (0,) * nd)


def _gla_weights(w_in, w_a2, b_a, norm_g, w_out, ln_g, ln_b, w_gate, w_proj):
    hk, hv = GLA_HK, GLA_HV
    wq = w_in[:, :hk].astype(BF16)
    wk = w_in[:, hk:2 * hk].astype(BF16)
    wv = w_in[:, 2 * hk:2 * hk + hv].astype(BF16)
    wg = w_in[:, 2 * hk + hv:2 * hk + 2 * hv].astype(BF16)
    wa = jnp.pad(w_in[:, 2 * hk + 2 * hv:], ((0, 0), (0, LANES - GLA_LOWRANK))).astype(BF16)
    wa2 = jnp.pad(w_a2, ((0, LANES - GLA_LOWRANK), (0, 0))).astype(BF16)
    return (wq, wk, wv, wg, wa, wa2, b_a.reshape(1, hk), norm_g.reshape(1, hv),
            w_out.astype(BF16), ln_g.reshape(1, D_MODEL), ln_b.reshape(1, D_MODEL),
            w_gate.astype(BF16), w_proj.astype(BF16))


def gla_layer_prompt(x, p, weights, *, block=256, chunk=GLA_CHUNK):
    bsz, t, _ = x.shape
    assert t % block == 0 and block % chunk == 0
    in_specs = [pl.BlockSpec((1, block, D_MODEL), lambda b, i: (b, i, 0)),
                pl.BlockSpec((1, block, PLE_DIM), lambda b, i: (b, i, 0))]
    in_specs += [_const_spec(w.shape) for w in weights]
    return pl.pallas_call(
        functools.partial(_gla_prompt_kernel, chunk=chunk),
        grid=(bsz, t // block),
        in_specs=in_specs,
        out_specs=[pl.BlockSpec((1, block, D_MODEL), lambda b, i: (b, i, 0)),
                   pl.BlockSpec((1, GLA_HEADS, GLA_DK, GLA_DV), lambda b, i: (b, 0, 0, 0))],
        out_shape=[jax.ShapeDtypeStruct((bsz, t, D_MODEL), F32),
                   jax.ShapeDtypeStruct((bsz, GLA_HEADS, GLA_DK, GLA_DV), F32)],
        scratch_shapes=[pltpu.VMEM((GLA_HEADS, GLA_DV, GLA_DK), F32),
                        pltpu.VMEM((block, GLA_HV), F32)],
        compiler_params=pltpu.CompilerParams(dimension_semantics=("arbitrary", "arbitrary"),
                                             vmem_limit_bytes=VMEM_LIMIT),
        name="gla_prompt",
    )(x, p, *weights)


def _gla_sample_kernel(x_ref, p_ref, s0_ref, wq_ref, wk_ref, wv_ref, wg_ref, wa_ref, wa2_ref, ba_ref, ng_ref,
                       wout_ref, lng_ref, lnb_ref, wgate_ref, wproj_ref,
                       y_ref, st_ref, o_scr, *, seq):
    rows = x_ref.shape[0]
    n_seq = rows // seq
    half = seq // 2
    x = x_ref[...]
    xb = x.astype(BF16)
    q, k, v, g, log_a = _gla_project(xb, wq_ref[...], wk_ref[...], wv_ref[...], wg_ref[...],
                                     wa_ref[...], wa2_ref[...], ba_ref[...])
    rel, _ = _chunk_decay_matrices(rows, seq)
    la_hi, la_mid = _split_bf16(log_a)
    d = _dot(rel, la_hi) + _dot(rel, la_mid)
    s_i = lax.broadcasted_iota(jnp.int32, (n_seq, rows), 0)
    t_i = lax.broadcasted_iota(jnp.int32, (n_seq, rows), 1)
    in_seq = (t_i // seq) == s_i
    sel_first = jnp.where(in_seq & ((t_i % seq) < half), 1.0, 0.0).astype(BF16)
    sel_second = jnp.where(in_seq & ((t_i % seq) >= half), 1.0, 0.0).astype(BF16)
    t_c = lax.broadcasted_iota(jnp.int32, (rows, n_seq), 0)
    s_c = lax.broadcasted_iota(jnp.int32, (rows, n_seq), 1)
    sel_tot_t = jnp.where((t_c // seq) == s_c, 1.0, 0.0).astype(BF16)
    e_first = jnp.exp(_dot(sel_first, la_hi) + _dot(sel_first, la_mid))
    e_second = jnp.exp(_dot(sel_second, la_hi) + _dot(sel_second, la_mid))
    e_tot_col = jnp.exp(_dot_ta(la_hi, sel_tot_t) + _dot_ta(la_mid, sel_tot_t))
    q_dec = q * jnp.exp(d)
    k_inv = k * jnp.exp(-d)

    ci = lax.broadcasted_iota(jnp.int32, (seq, seq), 0)
    cj = lax.broadcasted_iota(jnp.int32, (seq, seq), 1)
    causal = ci >= cj
    for c in range(n_seq):
        r0 = c * seq
        for h in range(GLA_HEADS):
            ks = slice(h * GLA_DK, (h + 1) * GLA_DK)
            vs = slice(h * GLA_DV, (h + 1) * GLA_DV)
            qd = q_dec[r0:r0 + seq, ks]
            ki = k_inv[r0:r0 + seq, ks]
            vh = v[r0:r0 + seq, vs].astype(BF16)
            s_old = s0_ref[c, h]
            att = jnp.where(causal, _dot_tb(qd.astype(BF16), ki.astype(BF16)), 0.0).astype(BF16)
            q_mid = (qd * e_first[c:c + 1, ks]).astype(BF16)
            o_scr[r0:r0 + seq, vs] = _dot(att, vh) + _dot(q_mid, s_old.astype(BF16))
            k_end = (ki * e_second[c:c + 1, ks]).astype(BF16)
            st_ref[c, h] = s_old * e_tot_col[h * GLA_DK:(h + 1) * GLA_DK, c:c + 1] + _dot_ta(k_end, vh)

    on = _head_rmsnorm_gate(o_scr[...], g, ng_ref[...], GLA_HEADS, GLA_DV, 1.0)
    f = _dot(on.astype(BF16), wout_ref[...])
    y_ref[...] = _residual_update(x, f, p_ref[...], lng_ref[...], lnb_ref[...], wgate_ref[...], wproj_ref[...])


def gla_layer_sample(x, p, s0, weights, *, group=8):
    n, seq, _ = x.shape
    assert n % group == 0 and seq % 2 == 0
    rows = group * seq
    in_specs = [pl.BlockSpec((rows, D_MODEL), lambda i: (i, 0)),
                pl.BlockSpec((rows, PLE_DIM), lambda i: (i, 0)),
                pl.BlockSpec((group, GLA_HEADS, GLA_DK, GLA_DV), lambda i: (i, 0, 0, 0))]
    in_specs += [_const_spec(w.shape) for w in weights]
    y, st = pl.pallas_call(
        functools.partial(_gla_sample_kernel, seq=seq),
        grid=(n // group,),
        in_specs=in_specs,
        out_specs=[pl.BlockSpec((rows, D_MODEL), lambda i: (i, 0)),
                   pl.BlockSpec((group, GLA_HEADS, GLA_DK, GLA_DV), lambda i: (i, 0, 0, 0))],
        out_shape=[jax.ShapeDtypeStruct((n * seq, D_MODEL), F32),
                   jax.ShapeDtypeStruct((n, GLA_HEADS, GLA_DK, GLA_DV), F32)],
        scratch_shapes=[pltpu.VMEM((rows, GLA_HV), F32)],
        compiler_params=pltpu.CompilerParams(dimension_semantics=("arbitrary",),
                                             vmem_limit_bytes=VMEM_LIMIT),
        name="gla_sample",
    )(x.reshape(n * seq, D_MODEL), p.reshape(n * seq, PLE_DIM), s0, *weights)
    return y.reshape(n, seq, D_MODEL), st


def _resident_spec(shape):
    nd = len(shape)
    return pl.BlockSpec(shape, lambda *_: (0,) * nd, pipeline_mode=pl.Buffered(1))


def _t5_bias(rel_bias, dist):
    max_exact = N_BUCKETS // 2
    n = jnp.maximum(dist, 0)
    nf = jnp.maximum(n, 1).astype(F32)
    large = max_exact + (jnp.log(nf / max_exact) / math.log(MAX_DISTANCE / max_exact)
                         * (N_BUCKETS - max_exact)).astype(jnp.int32)
    large = jnp.minimum(large, N_BUCKETS - 1)
    bucket = jnp.where(n < max_exact, n, large)[None]
    table = rel_bias.astype(F32)
    per_head = (DIFF_HEADS,) + (1,) * dist.ndim
    out = jnp.zeros((DIFF_HEADS,) + dist.shape, F32)
    for b in range(N_BUCKETS):
        out = jnp.where(bucket == b, table[b].reshape(per_head), out)
    return out


def _store_token_head_rows(ref, lead, x):
    tokens = x.shape[0]
    for h in range(DIFF_HEADS):
        ref[(*lead, pl.ds(h, tokens, stride=DIFF_HEADS), slice(None))] = x[:, h * DIFF_VD:(h + 1) * DIFF_VD]


def _load_token_head_rows(ref, lead, tokens):
    return jnp.concatenate(
        [ref[(*lead, pl.ds(h, tokens, stride=DIFF_HEADS), slice(None))] for h in range(DIFF_HEADS)], axis=1)


def _diff_lambda(lam_ref, lam_init):
    lv = lam_ref[...]
    a = jnp.sum(lv[0:1] * lv[1:2], axis=-1, keepdims=True)
    b = jnp.sum(lv[2:3] * lv[3:4], axis=-1, keepdims=True)
    return jnp.exp(a) - jnp.exp(b) + lam_init


def _diff_prompt_kernel(x_ref, p_ref, wq_ref, wk_ref, wv_ref, wg_ref, lam_ref, bias_ref, ng_ref,
                        wout_ref, lng_ref, lnb_ref, wgate_ref, wproj_ref,
                        y_ref, kout_ref, vout_ref,
                        k_scr, v_scr, q_scr, o_scr, acc_scr, m_scr, l_scr, alpha_scr, *bufs,
                        lam_init, heads_per_iter):
    i = pl.program_id(1)
    tq = x_ref.shape[1]
    x = x_ref[0]
    xb = x.astype(BF16)
    q = _dot(xb, wq_ref[...]) * (DIFF_DH ** -0.5 * LOG2_E)
    k = _dot(xb, wk_ref[...])
    v = _dot(xb, wv_ref[...])
    _store_token_head_rows(kout_ref, (0,), k)
    _store_token_head_rows(vout_ref, (0,), v)
    first_half = lax.broadcasted_iota(jnp.int32, (tq, DIFF_VD), 1) < DIFF_DH
    row0 = pl.multiple_of(i * tq, tq)
    for h in range(DIFF_HEADS):
        hs = slice(h * DIFF_VD, (h + 1) * DIFF_VD)
        qh = q[:, hs]
        q_scr[h, 0:tq, :] = jnp.where(first_half, qh, 0.0).astype(BF16)
        q_scr[h, tq:2 * tq, :] = jnp.where(first_half, 0.0, qh).astype(BF16)
        k_scr[h, pl.ds(row0, tq), :] = k[:, hs].astype(BF16)
        v_scr[h, pl.ds(row0, tq), :] = v[:, hs].astype(BF16)
    lam = _diff_lambda(lam_ref, lam_init)

    n_blocks = i + 1

    s_buf = [bufs[2 * u:2 * u + 2] for u in range(heads_per_iter)]
    p_buf = [bufs[2 * heads_per_iter + 2 * u:2 * heads_per_iter + 2 * u + 2] for u in range(heads_per_iter)]

    def group_body(grp, carry):
        heads = [grp * heads_per_iter + u for u in range(heads_per_iter)]
        m_scr[...] = jnp.full(m_scr.shape, MASK_VALUE, F32)
        l_scr[...] = jnp.zeros(l_scr.shape, F32)
        acc_scr[...] = jnp.zeros(acc_scr.shape, F32)
        alpha_scr[...] = jnp.ones(alpha_scr.shape, F32)
        for u in range(heads_per_iter):
            p_buf[u][1][...] = jnp.zeros(p_buf[u][1].shape, BF16)

        def key_rows(j):
            return pl.ds(pl.multiple_of(jnp.minimum(j, n_blocks - 1) * tq, tq), tq)

        def scores(j, slot):
            for u, h in enumerate(heads):
                s_buf[u][slot][...] = _dot_tb(k_scr[h, key_rows(j), :], q_scr[h])

        def softmax(j, slot):
            bias_slot = jnp.where(j > i, 3, jnp.minimum(i - j, 2))
            for u, h in enumerate(heads):
                t = bias_ref[h, bias_slot]
                s = s_buf[u][slot][...] + jnp.concatenate([t, t], axis=1)
                m_old = m_scr[u]
                m_new = jnp.maximum(m_old, jnp.max(s, axis=0, keepdims=True))
                alpha = jnp.exp2(m_old - m_new)
                pr = jnp.exp2(s - m_new)
                l_scr[u] = alpha * l_scr[u] + jnp.sum(pr, axis=0, keepdims=True)
                m_scr[u] = m_new
                alpha_scr[u] = alpha
                p_buf[u][slot][...] = pr.astype(BF16)

        def weighted_values(j, slot):
            for u, h in enumerate(heads):
                acc_scr[u] = acc_scr[u] * alpha_scr[u] + _dot_ta(v_scr[h, key_rows(j), :], p_buf[u][slot][...])

        def step(t, slot):
            weighted_values(jnp.maximum(t - 1, 0), 1 - slot)
            softmax(t, slot)
            scores(t + 1, 1 - slot)

        scores(0, 0)

        def two_steps(r, c):
            step(2 * r, 0)
            step(2 * r + 1, 1)
            return c

        lax.fori_loop(0, (n_blocks + 1) // 2, two_steps, 0)
        weighted_values(2 * ((n_blocks + 1) // 2) - 1, 1)

        for u, h in enumerate(heads):
            inv_l = 1.0 / l_scr[u]
            acc = acc_scr[u]
            o_t = acc[:, :tq] * inv_l[:, :tq] - lam * (acc[:, tq:] * inv_l[:, tq:])
            o_scr[h] = o_t.T
        return carry

    lax.fori_loop(0, DIFF_HEADS // heads_per_iter, group_body, 0)
    o = jnp.concatenate([o_scr[h] for h in range(DIFF_HEADS)], axis=-1)
    gate = _dot(xb, wg_ref[...])
    on = _head_rmsnorm_gate(o, gate, ng_ref[...], DIFF_HEADS, DIFF_VD, 1.0 - lam_init)
    f = _dot(on.astype(BF16), wout_ref[...])
    y_ref[0] = _residual_update(x, f, p_ref[0], lng_ref[...], lnb_ref[...], wgate_ref[...], wproj_ref[...])


def _diff_weights(w_in, lam_q1, lam_k1, lam_q2, lam_k2, norm_g, w_out, ln_g, ln_b, w_gate, w_proj):
    w = DIFF_WIDTH
    return dict(
        wq=w_in[:, :w].astype(BF16), wk=w_in[:, w:2 * w].astype(BF16),
        wv=w_in[:, 2 * w:3 * w].astype(BF16), wg=w_in[:, 3 * w:].astype(BF16),
        lam=jnp.stack([lam_q1, lam_k1, lam_q2, lam_k2]).astype(F32),
        ng=norm_g.reshape(1, w), wout=w_out.astype(BF16),
        lng=ln_g.reshape(1, D_MODEL), lnb=ln_b.reshape(1, D_MODEL),
        wgate=w_gate.astype(BF16), wproj=w_proj.astype(BF16))


def diff_layer_prompt(x, p, dw, rel_bias, lam_init, *, block=256, heads_per_iter=2):
    bsz, t, _ = x.shape
    assert t % block == 0 and block >= MAX_DISTANCE and DIFF_HEADS % heads_per_iter == 0
    kk = jnp.arange(block, dtype=jnp.int32)[:, None]
    qq = jnp.arange(block, dtype=jnp.int32)[None, :]
    none = jnp.full((block, block), -1, jnp.int32)
    dist = jnp.stack([qq - kk, block + qq - kk, 2 * block + qq - kk, none])
    far = rel_bias.astype(F32)[N_BUCKETS - 1].reshape(DIFF_HEADS, 1, 1, 1)
    bias = jnp.where(dist >= 0, (_t5_bias(rel_bias, dist) - far) * LOG2_E, MASK_VALUE)
    tok = lambda width: pl.BlockSpec((1, block, width), lambda b, i: (b, i, 0))
    kv_spec = pl.BlockSpec((1, block * DIFF_HEADS, DIFF_VD), lambda b, i: (b, i, 0))
    consts = [dw["wq"], dw["wk"], dw["wv"], dw["wg"], dw["lam"], bias, dw["ng"], dw["wout"],
              dw["lng"], dw["lnb"], dw["wgate"], dw["wproj"]]
    return pl.pallas_call(
        functools.partial(_diff_prompt_kernel, lam_init=lam_init, heads_per_iter=heads_per_iter),
        grid=(bsz, t // block),
        in_specs=[tok(D_MODEL), tok(PLE_DIM)] + [_resident_spec(c.shape) for c in consts],
        out_specs=[tok(D_MODEL), kv_spec, kv_spec],
        out_shape=[jax.ShapeDtypeStruct((bsz, t, D_MODEL), F32),
                   jax.ShapeDtypeStruct((bsz, t * DIFF_HEADS, DIFF_VD), F32),
                   jax.ShapeDtypeStruct((bsz, t * DIFF_HEADS, DIFF_VD), F32)],
        scratch_shapes=[pltpu.VMEM((DIFF_HEADS, t, DIFF_VD), BF16),
                        pltpu.VMEM((DIFF_HEADS, t, DIFF_VD), BF16),
                        pltpu.VMEM((DIFF_HEADS, 2 * block, DIFF_VD), BF16),
                        pltpu.VMEM((DIFF_HEADS, block, DIFF_VD), F32),
                        pltpu.VMEM((heads_per_iter, DIFF_VD, 2 * block), F32),
                        pltpu.VMEM((heads_per_iter, 1, 2 * block), F32),
                        pltpu.VMEM((heads_per_iter, 1, 2 * block), F32),
                        pltpu.VMEM((heads_per_iter, 1, 2 * block), F32),
                        *[pltpu.VMEM((block, 2 * block), F32)] * (2 * heads_per_iter),
                        *[pltpu.VMEM((block, 2 * block), BF16)] * (2 * heads_per_iter)],
        compiler_params=pltpu.CompilerParams(dimension_semantics=("arbitrary", "arbitrary"),
                                             vmem_limit_bytes=VMEM_LIMIT),
        name="diff_prompt",
    )(x, p, *consts)


def _diff_sample_project_kernel(x_ref, wqt_ref, wk_ref, wv_ref, qt_ref, k_ref, v_ref):
    xb = x_ref[...].astype(BF16)
    qt_ref[...] = (_dot_tb(wqt_ref[...], xb) * (DIFF_DH ** -0.5)).astype(BF16)
    _store_token_head_rows(k_ref, (), _dot(xb, wk_ref[...]))
    _store_token_head_rows(v_ref, (), _dot(xb, wv_ref[...]))


def _row_to_col(row, n):
    eye = lax.broadcasted_iota(jnp.int32, (n, n), 0) == lax.broadcasted_iota(jnp.int32, (n, n), 1)
    return jnp.sum(jnp.where(eye, row, 0.0), axis=1, keepdims=True)


def _paged_attn_kernel(pt_ref, qt_ref, kn_ref, vn_ref, lam_ref, bias_ref, bnew_ref, *rest, pages, seq, lam_init):
    k_pages = rest[:pages]
    v_pages = rest[pages:2 * pages]
    o_ref = rest[2 * pages]
    wq_scr, acc_scr, m_scr, l_scr = rest[2 * pages + 1:]
    b = pl.program_id(0)
    g = pl.program_id(1)
    ncol = 2 * DIFF_HEADS * seq

    @pl.when(g == 0)
    def _():
        local = (b % (LANES // seq)) * seq
        src = lax.broadcasted_iota(jnp.int32, (LANES, ncol), 0)
        col = lax.broadcasted_iota(jnp.int32, (LANES, ncol), 1)
        pick = jnp.where(src == local + col % seq, 1.0, 0.0).astype(BF16)
        rep = _dot(qt_ref[...], pick)
        r = lax.broadcasted_iota(jnp.int32, (DIFF_WIDTH, ncol), 0)
        c = lax.broadcasted_iota(jnp.int32, (DIFF_WIDTH, ncol), 1)
        own = ((r // DIFF_VD) == ((c % (DIFF_HEADS * seq)) // seq)) & \
              (((r % DIFF_VD) // DIFF_DH) == (c // (DIFF_HEADS * seq)))
        wq_scr[...] = jnp.where(own, rep, 0.0).astype(BF16)
        m_scr[...] = jnp.full(m_scr.shape, MASK_VALUE, F32)
        l_scr[...] = jnp.zeros(l_scr.shape, F32)
        acc_scr[...] = jnp.zeros(acc_scr.shape, F32)

    def flash_step(scores, values):
        m_old = m_scr[...]
        m_new = m_old
        for s in scores:
            m_new = jnp.maximum(m_new, jnp.max(s, axis=0, keepdims=True))
        alpha = jnp.exp(m_old - m_new)
        l_new = alpha * l_scr[...]
        pv = None
        for s, val in zip(scores, values):
            pr = jnp.exp(s - m_new)
            l_new = l_new + jnp.sum(pr, axis=0, keepdims=True)
            t = _dot_ta(pr.astype(BF16), val)
            pv = t if pv is None else pv + t
        acc_scr[...] = acc_scr[...] * _row_to_col(alpha, ncol) + pv
        l_scr[...] = l_new
        m_scr[...] = m_new

    def load_page(ref):
        return _load_token_head_rows(ref, (0,), PAGE_SIZE).astype(BF16)

    wq = wq_scr[...]
    scores, values = [], []
    for i in range(pages):
        scores.append(_dot(load_page(k_pages[i]), wq) + bias_ref[0, i * PAGE_SIZE:(i + 1) * PAGE_SIZE, :])
        values.append(load_page(v_pages[i]))
    flash_step(scores, values)

    @pl.when(g == pl.num_programs(1) - 1)
    def _():
        pad = jnp.zeros((16 - seq, DIFF_WIDTH), F32)
        kn = jnp.concatenate([_load_token_head_rows(kn_ref, (0,), seq), pad], axis=0).astype(BF16)
        vn = jnp.concatenate([_load_token_head_rows(vn_ref, (0,), seq), pad], axis=0).astype(BF16)
        flash_step([_dot(kn, wq) + bnew_ref[...]], [vn])
        lam = _diff_lambda(lam_ref, lam_init)
        inv_l = _row_to_col(1.0 / l_scr[...], ncol)
        half_rows = DIFF_HEADS * seq
        for h in range(DIFF_HEADS):
            cs = slice(h * DIFF_VD, (h + 1) * DIFF_VD)
            r1 = slice(h * seq, (h + 1) * seq)
            r2 = slice(half_rows + h * seq, half_rows + (h + 1) * seq)
            o_ref[0, :, cs] = acc_scr[r1, cs] * inv_l[r1] - lam * (acc_scr[r2, cs] * inv_l[r2])


def _diff_sample_out_kernel(x_ref, o_ref, p_ref, wg_ref, ng_ref, wout_ref, lng_ref, lnb_ref, wgate_ref, wproj_ref,
                            y_ref, *, lam_init):
    x = x_ref[...]
    gate = _dot(x.astype(BF16), wg_ref[...])
    on = _head_rmsnorm_gate(o_ref[...], gate, ng_ref[...], DIFF_HEADS, DIFF_VD, 1.0 - lam_init)
    f = _dot(on.astype(BF16), wout_ref[...])
    y_ref[...] = _residual_update(x, f, p_ref[...], lng_ref[...], lnb_ref[...], wgate_ref[...], wproj_ref[...])


def diff_layer_sample(x, p, cache_k, cache_v, page_table, dw, rel_bias, lam_init, *, pages=16, block=256):
    n, seq, _ = x.shape
    n_pages = page_table.shape[1]
    rows = n * seq
    assert n_pages % pages == 0 and rows % block == 0 and LANES % seq == 0 and seq <= 16
    assert PAGE_SIZE >= MAX_DISTANCE
    x2 = x.reshape(rows, D_MODEL)
    row_spec = lambda width: pl.BlockSpec((block, width), lambda i: (i, 0))
    qt, kn, vn = pl.pallas_call(
        _diff_sample_project_kernel,
        grid=(rows // block,),
        in_specs=[row_spec(D_MODEL), _const_spec((DIFF_WIDTH, D_MODEL)),
                  _const_spec((D_MODEL, DIFF_WIDTH)), _const_spec((D_MODEL, DIFF_WIDTH))],
        out_specs=[pl.BlockSpec((DIFF_WIDTH, block), lambda i: (0, i)),
                   pl.BlockSpec((block * DIFF_HEADS, DIFF_VD), lambda i: (i, 0)),
                   pl.BlockSpec((block * DIFF_HEADS, DIFF_VD), lambda i: (i, 0))],
        out_shape=[jax.ShapeDtypeStruct((DIFF_WIDTH, rows), BF16),
                   jax.ShapeDtypeStruct((rows * DIFF_HEADS, DIFF_VD), F32),
                   jax.ShapeDtypeStruct((rows * DIFF_HEADS, DIFF_VD), F32)],
        compiler_params=pltpu.CompilerParams(dimension_semantics=("arbitrary",), vmem_limit_bytes=VMEM_LIMIT),
        name="diff_sample_project",
    )(x2, dw["wq"].T, dw["wk"], dw["wv"])

    ncol = 2 * DIFF_HEADS * seq
    group_keys = pages * PAGE_SIZE
    col = jnp.arange(ncol, dtype=jnp.int32)[None, :]
    col_h = (col % (DIFF_HEADS * seq)) // seq
    col_t = col % seq
    kk = jnp.arange(group_keys, dtype=jnp.int32)[:, None]
    tk = jnp.arange(16, dtype=jnp.int32)[:, None]
    dist_far = jnp.full((group_keys, ncol), MAX_DISTANCE, jnp.int32)
    dist_last = group_keys + col_t - kk
    dist_new = col_t - tk

    def own_head(per_head):
        return sum(jnp.where(col_h == h, per_head[h], 0.0) for h in range(DIFF_HEADS))

    bias_pages = jnp.stack([own_head(_t5_bias(rel_bias, dist_far)),
                            own_head(_t5_bias(rel_bias, dist_last))])
    bias_new = jnp.where((tk < seq) & (dist_new >= 0), own_head(_t5_bias(rel_bias, dist_new)), MASK_VALUE)

    n_groups = n_pages // pages
    seq_per_blk = LANES // seq

    def page_spec(i):
        return pl.BlockSpec((1, PAGE_SIZE * DIFF_HEADS, DIFF_VD),
                            lambda b, g, pt: (pt[b * n_pages + g * pages + i], 0, 0))

    tok_spec = pl.BlockSpec((1, seq, DIFF_WIDTH), lambda b, g, pt: (b, 0, 0))
    new_spec = pl.BlockSpec((1, seq * DIFF_HEADS, DIFF_VD), lambda b, g, pt: (b, 0, 0))
    grid_spec = pltpu.PrefetchScalarGridSpec(
        num_scalar_prefetch=1,
        grid=(n, n_groups),
        in_specs=[pl.BlockSpec((DIFF_WIDTH, LANES), lambda b, g, pt: (0, b // seq_per_blk)),
                  new_spec, new_spec,
                  pl.BlockSpec((4, DIFF_DH), lambda b, g, pt: (0, 0)),
                  pl.BlockSpec((1, pages * PAGE_SIZE, ncol), lambda b, g, pt: ((g + 1) // n_groups, 0, 0)),
                  pl.BlockSpec((16, ncol), lambda b, g, pt: (0, 0))]
                 + [page_spec(i) for i in range(pages)] + [page_spec(i) for i in range(pages)],
        out_specs=tok_spec,
        scratch_shapes=[pltpu.VMEM((DIFF_WIDTH, ncol), BF16),
                        pltpu.VMEM((ncol, DIFF_WIDTH), F32),
                        pltpu.VMEM((1, ncol), F32),
                        pltpu.VMEM((1, ncol), F32)])
    o = pl.pallas_call(
        functools.partial(_paged_attn_kernel, pages=pages, seq=seq, lam_init=lam_init),
        grid_spec=grid_spec,
        out_shape=jax.ShapeDtypeStruct((n, seq, DIFF_WIDTH), F32),
        compiler_params=pltpu.CompilerParams(dimension_semantics=("arbitrary", "arbitrary"),
                                             vmem_limit_bytes=VMEM_LIMIT),
        name="diff_sample_attn",
    )(page_table.reshape(-1), qt, kn.reshape(n, seq * DIFF_HEADS, DIFF_VD), vn.reshape(n, seq * DIFF_HEADS, DIFF_VD),
      dw["lam"], bias_pages, bias_new, *([cache_k] * pages), *([cache_v] * pages))

    consts = [dw["wg"], dw["ng"], dw["wout"], dw["lng"], dw["lnb"], dw["wgate"], dw["wproj"]]
    y = pl.pallas_call(
        functools.partial(_diff_sample_out_kernel, lam_init=lam_init),
        grid=(rows // block,),
        in_specs=[row_spec(D_MODEL), row_spec(DIFF_WIDTH), row_spec(PLE_DIM)] + [_const_spec(c.shape) for c in consts],
        out_specs=row_spec(D_MODEL),
        out_shape=jax.ShapeDtypeStruct((rows, D_MODEL), F32),
        compiler_params=pltpu.CompilerParams(dimension_semantics=("arbitrary",), vmem_limit_bytes=VMEM_LIMIT),
        name="diff_sample_out",
    )(x2, o.reshape(rows, DIFF_WIDTH), p.reshape(rows, PLE_DIM), *consts)
    return y.reshape(n, seq, D_MODEL), kn, vn


def kernel(x_prompt, x_sample, state_gla, cache_k, cache_v, page_table, p_prompt, p_sample, rel_bias,
           gla_w_in, gla_w_a2, gla_b_a, gla_norm_g, gla_w_out,
           diff_w_in, diff_lam_q1, diff_lam_k1, diff_lam_q2, diff_lam_k2, diff_norm_g, diff_w_out,
           ln_g, ln_b, ple_w_proj, ple_w_gate):
    w0 = _gla_weights(gla_w_in[0], gla_w_a2[0], gla_b_a[0], gla_norm_g[0], gla_w_out[0],
                      ln_g[0], ln_b[0], ple_w_gate[0], ple_w_proj[0])
    xp1, sp = gla_layer_prompt(x_prompt, p_prompt[0], w0)
    xs1, ss = gla_layer_sample(x_sample, p_sample[0], state_gla[0], w0)
    lam_init = 0.8 - 0.6 * math.exp(-0.3 * 1)
    dw = _diff_weights(diff_w_in[0], diff_lam_q1[0], diff_lam_k1[0], diff_lam_q2[0], diff_lam_k2[0],
                       diff_norm_g[0], diff_w_out[0], ln_g[1], ln_b[1], ple_w_gate[1], ple_w_proj[1])
    yp, kp, vp = diff_layer_prompt(xp1, p_prompt[1], dw, rel_bias, lam_init)
    pool = cache_k.shape[1]
    ys, ks, vs = diff_layer_sample(xs1, p_sample[1],
                                   cache_k[0].reshape(pool, PAGE_SIZE * DIFF_HEADS, DIFF_VD),
                                   cache_v[0].reshape(pool, PAGE_SIZE * DIFF_HEADS, DIFF_VD),
                                   page_table, dw, rel_bias, lam_init)
    bsz, t, _ = x_prompt.shape
    n, seq, _ = x_sample.shape
    heads = (DIFF_HEADS, DIFF_VD)
    return (yp, ys, sp[None], ss[None],
            kp.reshape(1, bsz, t, *heads), vp.reshape(1, bsz, t, *heads),
            ks.reshape(1, n, seq, *heads), vs.reshape(1, n, seq, *heads))
```

```python
import functools
import math

import jax
import jax.numpy as jnp
from jax import lax
from jax.experimental import pallas as pl
from jax.experimental.pallas import tpu as pltpu

F32 = jnp.float32
BF16 = jnp.bfloat16

D_MODEL = 1024
DEPTH = 2
GLA_HEADS = 4
GLA_DK = 128
GLA_DV = 256
GLA_HK = GLA_HEADS * GLA_DK
GLA_HV = GLA_HEADS * GLA_DV
GLA_LOWRANK = 16
GLA_TAU = 16.0
DIFF_HEADS = 8
DIFF_DH = 64
DIFF_VD = 128
DIFF_WIDTH = DIFF_HEADS * DIFF_VD
N_BUCKETS = 32
MAX_DISTANCE = 128
PLE_DIM = 256
PAGE_SIZE = 128
ALPHA = (2 * DEPTH) ** 0.25
EPS = 1e-5

LANES = 128
GLA_CHUNK = 64
VMEM_LIMIT = 56 * 1024 * 1024
MASK_VALUE = -1e30
LOG2_E = math.log2(math.e)


def _dot(a, b):
    return jnp.dot(a, b, preferred_element_type=F32)


def _dot_tb(a, b):
    return lax.dot_general(a, b, (((1,), (1,)), ((), ())), preferred_element_type=F32)


def _dot_ta(a, b):
    return lax.dot_general(a, b, (((0,), (0,)), ((), ())), preferred_element_type=F32)


def _split_bf16(x):
    hi = x.astype(BF16)
    mid = (x - hi.astype(F32)).astype(BF16)
    return hi, mid


def _log_sigmoid(z):
    return jnp.minimum(z, 0.0) - jnp.log(1.0 + jnp.exp(-jnp.abs(z)))


def _sigmoid(z):
    return 1.0 / (1.0 + jnp.exp(-z))


def _head_rmsnorm_gate(o, gate, norm_g, n_heads, head_dim, scale):
    parts = []
    for h in range(n_heads):
        oh = o[:, h * head_dim:(h + 1) * head_dim]
        ms = jnp.mean(oh * oh, axis=-1, keepdims=True)
        parts.append(oh * lax.rsqrt(ms + EPS))
    on = jnp.concatenate(parts, axis=-1) * norm_g
    if scale != 1.0:
        on = on * scale
    return on * (gate * _sigmoid(gate))


def _residual_update(x, f, p, ln_g, ln_b, w_gate, w_proj):
    hp = ALPHA * x + f
    mu = jnp.mean(hp, axis=-1, keepdims=True)
    hc = hp - mu
    var = jnp.mean(hc * hc, axis=-1, keepdims=True)
    h = hc * lax.rsqrt(var + EPS) * ln_g + ln_b
    gate = _sigmoid(_dot(h.astype(BF16), w_gate))
    return h + gate * _dot(p.astype(BF16), w_proj)


def _gla_project(xb, wq, wk, wv, wg, wa, wa2, ba):
    q = _dot(xb, wq) * (GLA_DK ** -0.5)
    k = _dot(xb, wk)
    v = _dot(xb, wv)
    g = _dot(xb, wg)
    a_lr = _dot(xb, wa)
    z = _dot(a_lr.astype(BF16), wa2) + ba
    log_a = _log_sigmoid(z) * (1.0 / GLA_TAU)
    return q, k, v, g, log_a


def _chunk_decay_matrices(rows, chunk):
    half = chunk // 2
    i = lax.broadcasted_iota(jnp.int32, (rows, rows), 0)
    j = lax.broadcasted_iota(jnp.int32, (rows, rows), 1)
    same = (i // chunk) == (j // chunk)
    jl = j % chunk
    il = i % chunk
    pos = same & (jl >= half) & (jl <= il)
    neg = same & (jl < half) & (jl > il)
    rel = jnp.where(pos, 1.0, jnp.where(neg, -1.0, 0.0)).astype(BF16)
    n_sel = max(8, 2 * rows // chunk)
    s = lax.broadcasted_iota(jnp.int32, (n_sel, rows), 0)
    t = lax.broadcasted_iota(jnp.int32, (n_sel, rows), 1)
    halves = jnp.where((t // half) == s, 1.0, 0.0).astype(BF16)
    return rel, halves


def _gla_prompt_kernel(x_ref, p_ref, wq_ref, wk_ref, wv_ref, wg_ref, wa_ref, wa2_ref, ba_ref, ng_ref,
                       wout_ref, lng_ref, lnb_ref, wgate_ref, wproj_ref,
                       y_ref, st_ref, s_scr, o_scr, *, chunk):
    blk = pl.program_id(1)
    rows = x_ref.shape[1]

    @pl.when(blk == 0)
    def _():
        s_scr[...] = jnp.zeros_like(s_scr)

    x = x_ref[0]
    xb = x.astype(BF16)
    q, k, v, g, log_a = _gla_project(xb, wq_ref[...], wk_ref[...], wv_ref[...], wg_ref[...],
                                     wa_ref[...], wa2_ref[...], ba_ref[...])
    rel, halves = _chunk_decay_matrices(rows, chunk)
    la_hi, la_mid = _split_bf16(log_a)
    d = _dot(rel, la_hi) + _dot(rel, la_mid)
    hs = _dot(halves, la_hi) + _dot(halves, la_mid)
    ehs = jnp.exp(hs)
    q_dec = (q * jnp.exp(d)).astype(BF16)
    k_inv = (k * jnp.exp(-d)).astype(BF16)
    vb = v.astype(BF16)

    ci = lax.broadcasted_iota(jnp.int32, (chunk, chunk), 0)
    cj = lax.broadcasted_iota(jnp.int32, (chunk, chunk), 1)
    causal = ci >= cj
    for c in range(rows // chunk):
        r0 = c * chunk
        for h in range(GLA_HEADS):
            ks = slice(h * GLA_DK, (h + 1) * GLA_DK)
            vs = slice(h * GLA_DV, (h + 1) * GLA_DV)
            qd = q_dec[r0:r0 + chunk, ks]
            ki = k_inv[r0:r0 + chunk, ks]
            vh = vb[r0:r0 + chunk, vs]
            e_first = ehs[2 * c:2 * c + 1, ks]
            e_second = ehs[2 * c + 1:2 * c + 2, ks]
            s_mid = s_scr[h] * e_first
            att = jnp.where(causal, _dot_tb(qd, ki), 0.0).astype(BF16)
            o_scr[r0:r0 + chunk, vs] = _dot(att, vh) + _dot_tb(qd, s_mid.astype(BF16))
            s_scr[h] = (s_mid + _dot_ta(vh, ki)) * e_second

    on = _head_rmsnorm_gate(o_scr[...], g, ng_ref[...], GLA_HEADS, GLA_DV, 1.0)
    f = _dot(on.astype(BF16), wout_ref[...])
    y_ref[0] = _residual_update(x, f, p_ref[0], lng_ref[...], lnb_ref[...], wgate_ref[...], wproj_ref[...])

    @pl.when(blk == pl.num_programs(1) - 1)
    def _():
        for h in range(GLA_HEADS):
            st_ref[0, h] = s_scr[h].T


def _const_spec(shape):
    nd = len(shape)
    return pl.BlockSpec(shape, lambda *_: (0,) * nd)


def _gla_weights(w_in, w_a2, b_a, norm_g, w_out, ln_g, ln_b, w_gate, w_proj):
    hk, hv = GLA_HK, GLA_HV
    wq = w_in[:, :hk].astype(BF16)
    wk = w_in[:, hk:2 * hk].astype(BF16)
    wv = w_in[:, 2 * hk:2 * hk + hv].astype(BF16)
    wg = w_in[:, 2 * hk + hv:2 * hk + 2 * hv].astype(BF16)
    wa = jnp.pad(w_in[:, 2 * hk + 2 * hv:], ((0, 0), (0, LANES - GLA_LOWRANK))).astype(BF16)
    wa2 = jnp.pad(w_a2, ((0, LANES - GLA_LOWRANK), (0, 0))).astype(BF16)
    return (wq, wk, wv, wg, wa, wa2, b_a.reshape(1, hk), norm_g.reshape(1, hv),
            w_out.astype(BF16), ln_g.reshape(1, D_MODEL), ln_b.reshape(1, D_MODEL),
            w_gate.astype(BF16), w_proj.astype(BF16))


def gla_layer_prompt(x, p, weights, *, block=256, chunk=GLA_CHUNK):
    bsz, t, _ = x.shape
    assert t % block == 0 and block % chunk == 0
    in_specs = [pl.BlockSpec((1, block, D_MODEL), lambda b, i: (b, i, 0)),
                pl.BlockSpec((1, block, PLE_DIM), lambda b, i: (b, i, 0))]
    in_specs += [_const_spec(w.shape) for w in weights]
    return pl.pallas_call(
        functools.partial(_gla_prompt_kernel, chunk=chunk),
        grid=(bsz, t // block),
        in_specs=in_specs,
        out_specs=[pl.BlockSpec((1, block, D_MODEL), lambda b, i: (b, i, 0)),
                   pl.BlockSpec((1, GLA_HEADS, GLA_DK, GLA_DV), lambda b, i: (b, 0, 0, 0))],
        out_shape=[jax.ShapeDtypeStruct((bsz, t, D_MODEL), F32),
                   jax.ShapeDtypeStruct((bsz, GLA_HEADS, GLA_DK, GLA_DV), F32)],
        scratch_shapes=[pltpu.VMEM((GLA_HEADS, GLA_DV, GLA_DK), F32),
                        pltpu.VMEM((block, GLA_HV), F32)],
        compiler_params=pltpu.CompilerParams(dimension_semantics=("arbitrary", "arbitrary"),
                                             vmem_limit_bytes=VMEM_LIMIT),
        name="gla_prompt",
    )(x, p, *weights)


def _gla_sample_kernel(x_ref, p_ref, s0_ref, wq_ref, wk_ref, wv_ref, wg_ref, wa_ref, wa2_ref, ba_ref, ng_ref,
                       wout_ref, lng_ref, lnb_ref, wgate_ref, wproj_ref,
                       y_ref, st_ref, o_scr, *, seq):
    rows = x_ref.shape[0]
    n_seq = rows // seq
    half = seq // 2
    x = x_ref[...]
    xb = x.astype(BF16)
    q, k, v, g, log_a = _gla_project(xb, wq_ref[...], wk_ref[...], wv_ref[...], wg_ref[...],
                                     wa_ref[...], wa2_ref[...], ba_ref[...])
    rel, _ = _chunk_decay_matrices(rows, seq)
    la_hi, la_mid = _split_bf16(log_a)
    d = _dot(rel, la_hi) + _dot(rel, la_mid)
    s_i = lax.broadcasted_iota(jnp.int32, (n_seq, rows), 0)
    t_i = lax.broadcasted_iota(jnp.int32, (n_seq, rows), 1)
    in_seq = (t_i // seq) == s_i
    sel_first = jnp.where(in_seq & ((t_i % seq) < half), 1.0, 0.0).astype(BF16)
    sel_second = jnp.where(in_seq & ((t_i % seq) >= half), 1.0, 0.0).astype(BF16)
    t_c = lax.broadcasted_iota(jnp.int32, (rows, n_seq), 0)
    s_c = lax.broadcasted_iota(jnp.int32, (rows, n_seq), 1)
    sel_tot_t = jnp.where((t_c // seq) == s_c, 1.0, 0.0).astype(BF16)
    e_first = jnp.exp(_dot(sel_first, la_hi) + _dot(sel_first, la_mid))
    e_second = jnp.exp(_dot(sel_second, la_hi) + _dot(sel_second, la_mid))
    e_tot_col = jnp.exp(_dot_ta(la_hi, sel_tot_t) + _dot_ta(la_mid, sel_tot_t))
    q_dec = q * jnp.exp(d)
    k_inv = k * jnp.exp(-d)

    ci = lax.broadcasted_iota(jnp.int32, (seq, seq), 0)
    cj = lax.broadcasted_iota(jnp.int32, (seq, seq), 1)
    causal = ci >= cj
    for c in range(n_seq):
        r0 = c * seq
        for h in range(GLA_HEADS):
            ks = slice(h * GLA_DK, (h + 1) * GLA_DK)
            vs = slice(h * GLA_DV, (h + 1) * GLA_DV)
            qd = q_dec[r0:r0 + seq, ks]
            ki = k_inv[r0:r0 + seq, ks]
            vh = v[r0:r0 + seq, vs].astype(BF16)
            s_old = s0_ref[c, h]
            att = jnp.where(causal, _dot_tb(qd.astype(BF16), ki.astype(BF16)), 0.0).astype(BF16)
            q_mid = (qd * e_first[c:c + 1, ks]).astype(BF16)
            o_scr[r0:r0 + seq, vs] = _dot(att, vh) + _dot(q_mid, s_old.astype(BF16))
            k_end = (ki * e_second[c:c + 1, ks]).astype(BF16)
            st_ref[c, h] = s_old * e_tot_col[h * GLA_DK:(h + 1) * GLA_DK, c:c + 1] + _dot_ta(k_end, vh)

    on = _head_rmsnorm_gate(o_scr[...], g, ng_ref[...], GLA_HEADS, GLA_DV, 1.0)
    f = _dot(on.astype(BF16), wout_ref[...])
    y_ref[...] = _residual_update(x, f, p_ref[...], lng_ref[...], lnb_ref[...], wgate_ref[...], wproj_ref[...])


def gla_layer_sample(x, p, s0, weights, *, group=8):
    n, seq, _ = x.shape
    assert n % group == 0 and seq % 2 == 0
    rows = group * seq
    in_specs = [pl.BlockSpec((rows, D_MODEL), lambda i: (i, 0)),
                pl.BlockSpec((rows, PLE_DIM), lambda i: (i, 0)),
                pl.BlockSpec((group, GLA_HEADS, GLA_DK, GLA_DV), lambda i: (i, 0, 0, 0))]
    in_specs += [_const_spec(w.shape) for w in weights]
    y, st = pl.pallas_call(
        functools.partial(_gla_sample_kernel, seq=seq),
        grid=(n // group,),
        in_specs=in_specs,
        out_specs=[pl.BlockSpec((rows, D_MODEL), lambda i: (i, 0)),
                   pl.BlockSpec((group, GLA_HEADS, GLA_DK, GLA_DV), lambda i: (i, 0, 0, 0))],
        out_shape=[jax.ShapeDtypeStruct((n * seq, D_MODEL), F32),
                   jax.ShapeDtypeStruct((n, GLA_HEADS, GLA_DK, GLA_DV), F32)],
        scratch_shapes=[pltpu.VMEM((rows, GLA_HV), F32)],
        compiler_params=pltpu.CompilerParams(dimension_semantics=("arbitrary",),
                                             vmem_limit_bytes=VMEM_LIMIT),
        name="gla_sample",
    )(x.reshape(n * seq, D_MODEL), p.reshape(n * seq, PLE_DIM), s0, *weights)
    return y.reshape(n, seq, D_MODEL), st


def _resident_spec(shape):
    nd = len(shape)
    return pl.BlockSpec(shape, lambda *_: (0,) * nd, pipeline_mode=pl.Buffered(1))


def _t5_bias(rel_bias, dist):
    max_exact = N_BUCKETS // 2
    n = jnp.maximum(dist, 0)
    nf = jnp.maximum(n, 1).astype(F32)
    large = max_exact + (jnp.log(nf / max_exact) / math.log(MAX_DISTANCE / max_exact)
                         * (N_BUCKETS - max_exact)).astype(jnp.int32)
    large = jnp.minimum(large, N_BUCKETS - 1)
    bucket = jnp.where(n < max_exact, n, large)[None]
    table = rel_bias.astype(F32)
    per_head = (DIFF_HEADS,) + (1,) * dist.ndim
    out = jnp.zeros((DIFF_HEADS,) + dist.shape, F32)
    for b in range(N_BUCKETS):
        out = jnp.where(bucket == b, table[b].reshape(per_head), out)
    return out


def _store_token_head_rows(ref, lead, x):
    tokens = x.shape[0]
    for h in range(DIFF_HEADS):
        ref[(*lead, pl.ds(h, tokens, stride=DIFF_HEADS), slice(None))] = x[:, h * DIFF_VD:(h + 1) * DIFF_VD]


def _load_token_head_rows(ref, lead, tokens):
    return jnp.concatenate(
        [ref[(*lead, pl.ds(h, tokens, stride=DIFF_HEADS), slice(None))] for h in range(DIFF_HEADS)], axis=1)


def _diff_lambda(lam_ref, lam_init):
    lv = lam_ref[...]
    a = jnp.sum(lv[0:1] * lv[1:2], axis=-1, keepdims=True)
    b = jnp.sum(lv[2:3] * lv[3:4], axis=-1, keepdims=True)
    return jnp.exp(a) - jnp.exp(b) + lam_init


def _diff_prompt_kernel(x_ref, p_ref, wq_ref, wk_ref, wv_ref, wg_ref, lam_ref, bias_ref, ng_ref,
                        wout_ref, lng_ref, lnb_ref, wgate_ref, wproj_ref,
                        y_ref, kout_ref, vout_ref,
                        k_scr, vt_scr, q_scr, o_scr, acc_scr, m_scr, l_scr, alpha_scr, *bufs,
                        lam_init, heads_per_iter):
    i = pl.program_id(1)
    tq = x_ref.shape[1]
    x = x_ref[0]
    xb = x.astype(BF16)
    q = _dot(xb, wq_ref[...]) * (DIFF_DH ** -0.5 * LOG2_E)
    k = _dot(xb, wk_ref[...])
    v = _dot(xb, wv_ref[...])
    _store_token_head_rows(kout_ref, (0,), k)
    _store_token_head_rows(vout_ref, (0,), v)
    first_half = lax.broadcasted_iota(jnp.int32, (DIFF_VD, tq), 0) < DIFF_DH
    row0 = pl.multiple_of(i * tq, tq)
    for h in range(DIFF_HEADS):
        hs = slice(h * DIFF_VD, (h + 1) * DIFF_VD)
        qh_t = q[:, hs].T
        q_scr[h, :, 0:tq] = jnp.where(first_half, qh_t, 0.0).astype(BF16)
        q_scr[h, :, tq:2 * tq] = jnp.where(first_half, 0.0, qh_t).astype(BF16)
        k_scr[h, pl.ds(row0, tq), :] = k[:, hs].astype(BF16)
        vt_scr[h, i] = v[:, hs].T.astype(BF16)
    lam = _diff_lambda(lam_ref, lam_init)

    s_buf = [bufs[2 * u:2 * u + 2] for u in range(heads_per_iter)]
    p_buf = [bufs[2 * heads_per_iter + 2 * u:2 * heads_per_iter + 2 * u + 2] for u in range(heads_per_iter)]

    def block_of(t):
        near = jnp.maximum(i - t, 0)
        if isinstance(t, int):
            return near if t < 2 else jnp.minimum(t - 2, i)
        return jnp.where(t < 2, near, jnp.minimum(t - 2, i))

    def group_body(grp, carry):
        heads = [grp * heads_per_iter + u for u in range(heads_per_iter)]
        m_scr[...] = jnp.full(m_scr.shape, MASK_VALUE, F32)
        l_scr[...] = jnp.zeros(l_scr.shape, F32)
        acc_scr[...] = jnp.zeros(acc_scr.shape, F32)

        def scores(t, slot):
            rows = pl.ds(pl.multiple_of(block_of(t) * tq, tq), tq)
            for u, h in enumerate(heads):
                s_buf[u][slot][...] = _dot(k_scr[h, rows, :], q_scr[h])

        def softmax(t, slot, bias_slot):
            valid = t <= i
            for u, h in enumerate(heads):
                s = s_buf[u][slot][...]
                if bias_slot is not None:
                    tile = bias_ref[h, bias_slot]
                    s = s + jnp.concatenate([tile, tile], axis=1)
                m_old = m_scr[u]
                m_new = jnp.where(valid, jnp.maximum(m_old, jnp.max(s, axis=0, keepdims=True)), m_old)
                alpha = jnp.exp2(m_old - m_new)
                pr = jnp.exp2(s - jnp.where(valid, m_new, -MASK_VALUE))
                l_scr[u] = alpha * l_scr[u] + jnp.sum(pr, axis=0, keepdims=True)
                m_scr[u] = m_new
                alpha_scr[u] = alpha
                p_buf[u][slot][...] = pr.astype(BF16)

        def weighted_values(t, slot):
            j = block_of(t)
            for u, h in enumerate(heads):
                acc_scr[u] = acc_scr[u] * alpha_scr[u] + _dot(vt_scr[h, j], p_buf[u][slot][...])

        def step(t, slot, bias_slot):
            weighted_values(t - 1, 1 - slot)
            softmax(t, slot, bias_slot)
            scores(t + 1, 1 - slot)

        scores(0, 0)
        softmax(0, 0, 0)
        scores(1, 1)
        step(1, 1, 1)

        def two_far_steps(r, c):
            step(2 * r, 0, None)
            step(2 * r + 1, 1, None)
            return c

        n_pairs = (i + 2) // 2
        lax.fori_loop(1, n_pairs, two_far_steps, 0)
        weighted_values(2 * n_pairs - 1, 1)

        for u, h in enumerate(heads):
            inv_l = 1.0 / l_scr[u]
            acc = acc_scr[u]
            o_t = acc[:, :tq] * inv_l[:, :tq] - lam * (acc[:, tq:] * inv_l[:, tq:])
            o_scr[h] = o_t.T
        return carry

    lax.fori_loop(0, DIFF_HEADS // heads_per_iter, group_body, 0)
    o = jnp.concatenate([o_scr[h] for h in range(DIFF_HEADS)], axis=-1)
    gate = _dot(xb, wg_ref[...])
    on = _head_rmsnorm_gate(o, gate, ng_ref[...], DIFF_HEADS, DIFF_VD, 1.0 - lam_init)
    f = _dot(on.astype(BF16), wout_ref[...])
    y_ref[0] = _residual_update(x, f, p_ref[0], lng_ref[...], lnb_ref[...], wgate_ref[...], wproj_ref[...])


def _diff_weights(w_in, lam_q1, lam_k1, lam_q2, lam_k2, norm_g, w_out, ln_g, ln_b, w_gate, w_proj):
    w = DIFF_WIDTH
    return dict(
        wq=w_in[:, :w].astype(BF16), wk=w_in[:, w:2 * w].astype(BF16),
        wv=w_in[:, 2 * w:3 * w].astype(BF16), wg=w_in[:, 3 * w:].astype(BF16),
        lam=jnp.stack([lam_q1, lam_k1, lam_q2, lam_k2]).astype(F32),
        ng=norm_g.reshape(1, w), wout=w_out.astype(BF16),
        lng=ln_g.reshape(1, D_MODEL), lnb=ln_b.reshape(1, D_MODEL),
        wgate=w_gate.astype(BF16), wproj=w_proj.astype(BF16))


def diff_layer_prompt(x, p, dw, rel_bias, lam_init, *, block=256, heads_per_iter=2):
    bsz, t, _ = x.shape
    assert t % block == 0 and block >= MAX_DISTANCE and DIFF_HEADS % heads_per_iter == 0
    kk = jnp.arange(block, dtype=jnp.int32)[:, None]
    qq = jnp.arange(block, dtype=jnp.int32)[None, :]
    dist = jnp.stack([qq - kk, block + qq - kk])
    far = rel_bias.astype(F32)[N_BUCKETS - 1].reshape(DIFF_HEADS, 1, 1, 1)
    bias = jnp.where(dist >= 0, (_t5_bias(rel_bias, dist) - far) * LOG2_E, MASK_VALUE)
    tok = lambda width: pl.BlockSpec((1, block, width), lambda b, i: (b, i, 0))
    kv_spec = pl.BlockSpec((1, block * DIFF_HEADS, DIFF_VD), lambda b, i: (b, i, 0))
    consts = [dw["wq"], dw["wk"], dw["wv"], dw["wg"], dw["lam"], bias, dw["ng"], dw["wout"],
              dw["lng"], dw["lnb"], dw["wgate"], dw["wproj"]]
    return pl.pallas_call(
        functools.partial(_diff_prompt_kernel, lam_init=lam_init, heads_per_iter=heads_per_iter),
        grid=(bsz, t // block),
        in_specs=[tok(D_MODEL), tok(PLE_DIM)] + [_resident_spec(c.shape) for c in consts],
        out_specs=[tok(D_MODEL), kv_spec, kv_spec],
        out_shape=[jax.ShapeDtypeStruct((bsz, t, D_MODEL), F32),
                   jax.ShapeDtypeStruct((bsz, t * DIFF_HEADS, DIFF_VD), F32),
                   jax.ShapeDtypeStruct((bsz, t * DIFF_HEADS, DIFF_VD), F32)],
        scratch_shapes=[pltpu.VMEM((DIFF_HEADS, t, DIFF_VD), BF16),
                        pltpu.VMEM((DIFF_HEADS, t // block, DIFF_VD, block), BF16),
                        pltpu.VMEM((DIFF_HEADS, DIFF_VD, 2 * block), BF16),
                        pltpu.VMEM((DIFF_HEADS, block, DIFF_VD), F32),
                        pltpu.VMEM((heads_per_iter, DIFF_VD, 2 * block), F32),
                        pltpu.VMEM((heads_per_iter, 1, 2 * block), F32),
                        pltpu.VMEM((heads_per_iter, 1, 2 * block), F32),
                        pltpu.VMEM((heads_per_iter, 1, 2 * block), F32),
                        *[pltpu.VMEM((block, 2 * block), F32)] * (2 * heads_per_iter),
                        *[pltpu.VMEM((block, 2 * block), BF16)] * (2 * heads_per_iter)],
        compiler_params=pltpu.CompilerParams(dimension_semantics=("arbitrary", "arbitrary"),
                                             vmem_limit_bytes=VMEM_LIMIT),
        name="diff_prompt",
    )(x, p, *consts)


def _diff_sample_project_kernel(x_ref, wqt_ref, wk_ref, wv_ref, qt_ref, k_ref, v_ref):
    xb = x_ref[...].astype(BF16)
    qt_ref[...] = (_dot_tb(wqt_ref[...], xb) * (DIFF_DH ** -0.5)).astype(BF16)
    _store_token_head_rows(k_ref, (), _dot(xb, wk_ref[...]))
    _store_token_head_rows(v_ref, (), _dot(xb, wv_ref[...]))


def _row_to_col(row, n):
    eye = lax.broadcasted_iota(jnp.int32, (n, n), 0) == lax.broadcasted_iota(jnp.int32, (n, n), 1)
    return jnp.sum(jnp.where(eye, row, 0.0), axis=1, keepdims=True)


def _paged_attn_kernel(pt_ref, qt_ref, kn_ref, vn_ref, lam_ref, bias_ref, bnew_ref, *rest, pages, seq, lam_init):
    k_pages = rest[:pages]
    v_pages = rest[pages:2 * pages]
    o_ref = rest[2 * pages]
    wq_scr, acc_scr, m_scr, l_scr = rest[2 * pages + 1:]
    b = pl.program_id(0)
    g = pl.program_id(1)
    ncol = 2 * DIFF_HEADS * seq

    @pl.when(g == 0)
    def _():
        local = (b % (LANES // seq)) * seq
        src = lax.broadcasted_iota(jnp.int32, (LANES, ncol), 0)
        col = lax.broadcasted_iota(jnp.int32, (LANES, ncol), 1)
        pick = jnp.where(src == local + col % seq, 1.0, 0.0).astype(BF16)
        rep = _dot(qt_ref[...], pick)
        r = lax.broadcasted_iota(jnp.int32, (DIFF_WIDTH, ncol), 0)
        c = lax.broadcasted_iota(jnp.int32, (DIFF_WIDTH, ncol), 1)
        own = ((r // DIFF_VD) == ((c % (DIFF_HEADS * seq)) // seq)) & \
              (((r % DIFF_VD) // DIFF_DH) == (c // (DIFF_HEADS * seq)))
        wq_scr[...] = jnp.where(own, rep, 0.0).astype(BF16)
        m_scr[...] = jnp.full(m_scr.shape, MASK_VALUE, F32)
        l_scr[...] = jnp.zeros(l_scr.shape, F32)
        acc_scr[...] = jnp.zeros(acc_scr.shape, F32)

    def flash_step(scores, values):
        m_old = m_scr[...]
        m_new = m_old
        for s in scores:
            m_new = jnp.maximum(m_new, jnp.max(s, axis=0, keepdims=True))
        alpha = jnp.exp(m_old - m_new)
        l_new = alpha * l_scr[...]
        pv = None
        for s, val in zip(scores, values):
            pr = jnp.exp(s - m_new)
            l_new = l_new + jnp.sum(pr, axis=0, keepdims=True)
            t = _dot_ta(pr.astype(BF16), val)
            pv = t if pv is None else pv + t
        acc_scr[...] = acc_scr[...] * _row_to_col(alpha, ncol) + pv
        l_scr[...] = l_new
        m_scr[...] = m_new

    def load_page(ref):
        return _load_token_head_rows(ref, (0,), PAGE_SIZE).astype(BF16)

    wq = wq_scr[...]
    scores, values = [], []
    for i in range(pages):
        scores.append(_dot(load_page(k_pages[i]), wq) + bias_ref[0, i * PAGE_SIZE:(i + 1) * PAGE_SIZE, :])
        values.append(load_page(v_pages[i]))
    flash_step(scores, values)

    @pl.when(g == pl.num_programs(1) - 1)
    def _():
        pad = jnp.zeros((16 - seq, DIFF_WIDTH), F32)
        kn = jnp.concatenate([_load_token_head_rows(kn_ref, (0,), seq), pad], axis=0).astype(BF16)
        vn = jnp.concatenate([_load_token_head_rows(vn_ref, (0,), seq), pad], axis=0).astype(BF16)
        flash_step([_dot(kn, wq) + bnew_ref[...]], [vn])
        lam = _diff_lambda(lam_ref, lam_init)
        inv_l = _row_to_col(1.0 / l_scr[...], ncol)
        half_rows = DIFF_HEADS * seq
        for h in range(DIFF_HEADS):
            cs = slice(h * DIFF_VD, (h + 1) * DIFF_VD)
            r1 = slice(h * seq, (h + 1) * seq)
            r2 = slice(half_rows + h * seq, half_rows + (h + 1) * seq)
            o_ref[0, :, cs] = acc_scr[r1, cs] * inv_l[r1] - lam * (acc_scr[r2, cs] * inv_l[r2])


def _diff_sample_out_kernel(x_ref, o_ref, p_ref, wg_ref, ng_ref, wout_ref, lng_ref, lnb_ref, wgate_ref, wproj_ref,
                            y_ref, *, lam_init):
    x = x_ref[...]
    gate = _dot(x.astype(BF16), wg_ref[...])
    on = _head_rmsnorm_gate(o_ref[...], gate, ng_ref[...], DIFF_HEADS, DIFF_VD, 1.0 - lam_init)
    f = _dot(on.astype(BF16), wout_ref[...])
    y_ref[...] = _residual_update(x, f, p_ref[...], lng_ref[...], lnb_ref[...], wgate_ref[...], wproj_ref[...])


def diff_layer_sample(x, p, cache_k, cache_v, page_table, dw, rel_bias, lam_init, *, pages=16, block=256):
    n, seq, _ = x.shape
    n_pages = page_table.shape[1]
    rows = n * seq
    assert n_pages % pages == 0 and rows % block == 0 and LANES % seq == 0 and seq <= 16
    assert PAGE_SIZE >= MAX_DISTANCE
    x2 = x.reshape(rows, D_MODEL)
    row_spec = lambda width: pl.BlockSpec((block, width), lambda i: (i, 0))
    qt, kn, vn = pl.pallas_call(
        _diff_sample_project_kernel,
        grid=(rows // block,),
        in_specs=[row_spec(D_MODEL), _const_spec((DIFF_WIDTH, D_MODEL)),
                  _const_spec((D_MODEL, DIFF_WIDTH)), _const_spec((D_MODEL, DIFF_WIDTH))],
        out_specs=[pl.BlockSpec((DIFF_WIDTH, block), lambda i: (0, i)),
                   pl.BlockSpec((block * DIFF_HEADS, DIFF_VD), lambda i: (i, 0)),
                   pl.BlockSpec((block * DIFF_HEADS, DIFF_VD), lambda i: (i, 0))],
        out_shape=[jax.ShapeDtypeStruct((DIFF_WIDTH, rows), BF16),
                   jax.ShapeDtypeStruct((rows * DIFF_HEADS, DIFF_VD), F32),
                   jax.ShapeDtypeStruct((rows * DIFF_HEADS, DIFF_VD), F32)],
        compiler_params=pltpu.CompilerParams(dimension_semantics=("arbitrary",), vmem_limit_bytes=VMEM_LIMIT),
        name="diff_sample_project",
    )(x2, dw["wq"].T, dw["wk"], dw["wv"])

    ncol = 2 * DIFF_HEADS * seq
    group_keys = pages * PAGE_SIZE
    col = jnp.arange(ncol, dtype=jnp.int32)[None, :]
    col_h = (col % (DIFF_HEADS * seq)) // seq
    col_t = col % seq
    kk = jnp.arange(group_keys, dtype=jnp.int32)[:, None]
    tk = jnp.arange(16, dtype=jnp.int32)[:, None]
    dist_far = jnp.full((group_keys, ncol), MAX_DISTANCE, jnp.int32)
    dist_last = group_keys + col_t - kk
    dist_new = col_t - tk

    def own_head(per_head):
        return sum(jnp.where(col_h == h, per_head[h], 0.0) for h in range(DIFF_HEADS))

    bias_pages = jnp.stack([own_head(_t5_bias(rel_bias, dist_far)),
                            own_head(_t5_bias(rel_bias, dist_last))])
    bias_new = jnp.where((tk < seq) & (dist_new >= 0), own_head(_t5_bias(rel_bias, dist_new)), MASK_VALUE)

    n_groups = n_pages // pages
    seq_per_blk = LANES // seq

    def page_spec(i):
        return pl.BlockSpec((1, PAGE_SIZE * DIFF_HEADS, DIFF_VD),
                            lambda b, g, pt: (pt[b * n_pages + g * pages + i], 0, 0))

    tok_spec = pl.BlockSpec((1, seq, DIFF_WIDTH), lambda b, g, pt: (b, 0, 0))
    new_spec = pl.BlockSpec((1, seq * DIFF_HEADS, DIFF_VD), lambda b, g, pt: (b, 0, 0))
    grid_spec = pltpu.PrefetchScalarGridSpec(
        num_scalar_prefetch=1,
        grid=(n, n_groups),
        in_specs=[pl.BlockSpec((DIFF_WIDTH, LANES), lambda b, g, pt: (0, b // seq_per_blk)),
                  new_spec, new_spec,
                  pl.BlockSpec((4, DIFF_DH), lambda b, g, pt: (0, 0)),
                  pl.BlockSpec((1, pages * PAGE_SIZE, ncol), lambda b, g, pt: ((g + 1) // n_groups, 0, 0)),
                  pl.BlockSpec((16, ncol), lambda b, g, pt: (0, 0))]
                 + [page_spec(i) for i in range(pages)] + [page_spec(i) for i in range(pages)],
        out_specs=tok_spec,
        scratch_shapes=[pltpu.VMEM((DIFF_WIDTH, ncol), BF16),
                        pltpu.VMEM((ncol, DIFF_WIDTH), F32),
                        pltpu.VMEM((1, ncol), F32),
                        pltpu.VMEM((1, ncol), F32)])
    o = pl.pallas_call(
        functools.partial(_paged_attn_kernel, pages=pages, seq=seq, lam_init=lam_init),
        grid_spec=grid_spec,
        out_shape=jax.ShapeDtypeStruct((n, seq, DIFF_WIDTH), F32),
        compiler_params=pltpu.CompilerParams(dimension_semantics=("arbitrary", "arbitrary"),
                                             vmem_limit_bytes=VMEM_LIMIT),
        name="diff_sample_attn",
    )(page_table.reshape(-1), qt, kn.reshape(n, seq * DIFF_HEADS, DIFF_VD), vn.reshape(n, seq * DIFF_HEADS, DIFF_VD),
      dw["lam"], bias_pages, bias_new, *([cache_k] * pages), *([cache_v] * pages))

    consts = [dw["wg"], dw["ng"], dw["wout"], dw["lng"], dw["lnb"], dw["wgate"], dw["wproj"]]
    y = pl.pallas_call(
        functools.partial(_diff_sample_out_kernel, lam_init=lam_init),
        grid=(rows // block,),
        in_specs=[row_spec(D_MODEL), row_spec(DIFF_WIDTH), row_spec(PLE_DIM)] + [_const_spec(c.shape) for c in consts],
        out_specs=row_spec(D_MODEL),
        out_shape=jax.ShapeDtypeStruct((rows, D_MODEL), F32),
        compiler_params=pltpu.CompilerParams(dimension_semantics=("arbitrary",), vmem_limit_bytes=VMEM_LIMIT),
        name="diff_sample_out",
    )(x2, o.reshape(rows, DIFF_WIDTH), p.reshape(rows, PLE_DIM), *consts)
    return y.reshape(n, seq, D_MODEL), kn, vn


def kernel(x_prompt, x_sample, state_gla, cache_k, cache_v, page_table, p_prompt, p_sample, rel_bias,
           gla_w_in, gla_w_a2, gla_b_a, gla_norm_g, gla_w_out,
           diff_w_in, diff_lam_q1, diff_lam_k1, diff_lam_q2, diff_lam_k2, diff_norm_g, diff_w_out,
           ln_g, ln_b, ple_w_proj, ple_w_gate):
    w0 = _gla_weights(gla_w_in[0], gla_w_a2[0], gla_b_a[0], gla_norm_g[0], gla_w_out[0],
                      ln_g[0], ln_b[0], ple_w_gate[0], ple_w_proj[0])
    xp1, sp = gla_layer_prompt(x_prompt, p_prompt[0], w0)
    xs1, ss = gla_layer_sample(x_sample, p_sample[0], state_gla[0], w0)
    lam_init = 0.8 - 0.6 * math.exp(-0.3 * 1)
    dw = _diff_weights(diff_w_in[0], diff_lam_q1[0], diff_lam_k1[0], diff_lam_q2[0], diff_lam_k2[0],
                       diff_norm_g[0], diff_w_out[0], ln_g[1], ln_b[1], ple_w_gate[1], ple_w_proj[1])
    yp, kp, vp = diff_layer_prompt(xp1, p_prompt[1], dw, rel_bias, lam_init)
    pool = cache_k.shape[1]
    ys, ks, vs = diff_layer_sample(xs1, p_sample[1],
                                   cache_k[0].reshape(pool, PAGE_SIZE * DIFF_HEADS, DIFF_VD),
                                   cache_v[0].reshape(pool, PAGE_SIZE * DIFF_HEADS, DIFF_VD),
                                   page_table, dw, rel_bias, lam_init)
    bsz, t, _ = x_prompt.shape
    n, seq, _ = x_sample.shape
    heads = (DIFF_HEADS, DIFF_VD)
    return (yp, ys, sp[None], ss[None],
            kp.reshape(1, bsz, t, *heads), vp.reshape(1, bsz, t, *heads),
            ks.reshape(1, n, seq, *heads), vs.reshape(1, n, seq, *heads))
```

```python
import functools
import math

import jax
import jax.numpy as jnp
from jax import lax
from jax.experimental import pallas as pl
from jax.experimental.pallas import tpu as pltpu

F32 = jnp.float32
BF16 = jnp.bfloat16

D_MODEL = 1024
DEPTH = 2
GLA_HEADS = 4
GLA_DK = 128
GLA_DV = 256
GLA_HK = GLA_HEADS * GLA_DK
GLA_HV = GLA_HEADS * GLA_DV
GLA_LOWRANK = 16
GLA_TAU = 16.0
DIFF_HEADS = 8
DIFF_DH = 64
DIFF_VD = 128
DIFF_WIDTH = DIFF_HEADS * DIFF_VD
N_BUCKETS = 32
MAX_DISTANCE = 128
PLE_DIM = 256
PAGE_SIZE = 128
ALPHA = (2 * DEPTH) ** 0.25
EPS = 1e-5

LANES = 128
GLA_CHUNK = 64
VMEM_LIMIT = 56 * 1024 * 1024
MASK_VALUE = -1e30
LOG2_E = math.log2(math.e)
SUM_ROWS = 16


def _dot(a, b):
    return jnp.dot(a, b, preferred_element_type=F32)


def _dot_tb(a, b):
    return lax.dot_general(a, b, (((1,), (1,)), ((), ())), preferred_element_type=F32)


def _dot_ta(a, b):
    return lax.dot_general(a, b, (((0,), (0,)), ((), ())), preferred_element_type=F32)


def _split_bf16(x):
    hi = x.astype(BF16)
    mid = (x - hi.astype(F32)).astype(BF16)
    return hi, mid


def _log_sigmoid(z):
    return jnp.minimum(z, 0.0) - jnp.log(1.0 + jnp.exp(-jnp.abs(z)))


def _sigmoid(z):
    return 1.0 / (1.0 + jnp.exp(-z))


def _head_rmsnorm_gate(o, gate, norm_g, n_heads, head_dim, scale):
    parts = []
    for h in range(n_heads):
        oh = o[:, h * head_dim:(h + 1) * head_dim]
        ms = jnp.mean(oh * oh, axis=-1, keepdims=True)
        parts.append(oh * lax.rsqrt(ms + EPS))
    on = jnp.concatenate(parts, axis=-1) * norm_g
    if scale != 1.0:
        on = on * scale
    return on * (gate * _sigmoid(gate))


def _residual_update(x, f, p, ln_g, ln_b, w_gate, w_proj):
    hp = ALPHA * x + f
    mu = jnp.mean(hp, axis=-1, keepdims=True)
    hc = hp - mu
    var = jnp.mean(hc * hc, axis=-1, keepdims=True)
    h = hc * lax.rsqrt(var + EPS) * ln_g + ln_b
    gate = _sigmoid(_dot(h.astype(BF16), w_gate))
    return h + gate * _dot(p.astype(BF16), w_proj)


def _gla_project(xb, wq, wk, wv, wg, wa, wa2, ba):
    q = _dot(xb, wq) * (GLA_DK ** -0.5)
    k = _dot(xb, wk)
    v = _dot(xb, wv)
    g = _dot(xb, wg)
    a_lr = _dot(xb, wa)
    z = _dot(a_lr.astype(BF16), wa2) + ba
    log_a = _log_sigmoid(z) * (1.0 / GLA_TAU)
    return q, k, v, g, log_a


def _chunk_decay_matrices(rows, chunk):
    half = chunk // 2
    i = lax.broadcasted_iota(jnp.int32, (rows, rows), 0)
    j = lax.broadcasted_iota(jnp.int32, (rows, rows), 1)
    same = (i // chunk) == (j // chunk)
    jl = j % chunk
    il = i % chunk
    pos = same & (jl >= half) & (jl <= il)
    neg = same & (jl < half) & (jl > il)
    rel = jnp.where(pos, 1.0, jnp.where(neg, -1.0, 0.0)).astype(BF16)
    n_sel = max(8, 2 * rows // chunk)
    s = lax.broadcasted_iota(jnp.int32, (n_sel, rows), 0)
    t = lax.broadcasted_iota(jnp.int32, (n_sel, rows), 1)
    halves = jnp.where((t // half) == s, 1.0, 0.0).astype(BF16)
    return rel, halves


def _gla_prompt_kernel(x_ref, p_ref, wq_ref, wk_ref, wv_ref, wg_ref, wa_ref, wa2_ref, ba_ref, ng_ref,
                       wout_ref, lng_ref, lnb_ref, wgate_ref, wproj_ref,
                       y_ref, st_ref, s_scr, o_scr, *, chunk):
    blk = pl.program_id(1)
    rows = x_ref.shape[1]

    @pl.when(blk == 0)
    def _():
        s_scr[...] = jnp.zeros_like(s_scr)

    x = x_ref[0]
    xb = x.astype(BF16)
    q, k, v, g, log_a = _gla_project(xb, wq_ref[...], wk_ref[...], wv_ref[...], wg_ref[...],
                                     wa_ref[...], wa2_ref[...], ba_ref[...])
    rel, halves = _chunk_decay_matrices(rows, chunk)
    la_hi, la_mid = _split_bf16(log_a)
    d = _dot(rel, la_hi) + _dot(rel, la_mid)
    hs = _dot(halves, la_hi) + _dot(halves, la_mid)
    ehs = jnp.exp(hs)
    q_dec = (q * jnp.exp(d)).astype(BF16)
    k_inv = (k * jnp.exp(-d)).astype(BF16)
    vb = v.astype(BF16)

    ci = lax.broadcasted_iota(jnp.int32, (chunk, chunk), 0)
    cj = lax.broadcasted_iota(jnp.int32, (chunk, chunk), 1)
    causal = ci >= cj
    for c in range(rows // chunk):
        r0 = c * chunk
        for h in range(GLA_HEADS):
            ks = slice(h * GLA_DK, (h + 1) * GLA_DK)
            vs = slice(h * GLA_DV, (h + 1) * GLA_DV)
            qd = q_dec[r0:r0 + chunk, ks]
            ki = k_inv[r0:r0 + chunk, ks]
            vh = vb[r0:r0 + chunk, vs]
            e_first = ehs[2 * c:2 * c + 1, ks]
            e_second = ehs[2 * c + 1:2 * c + 2, ks]
            s_mid = s_scr[h] * e_first
            att = jnp.where(causal, _dot_tb(qd, ki), 0.0).astype(BF16)
            o_scr[r0:r0 + chunk, vs] = _dot(att, vh) + _dot_tb(qd, s_mid.astype(BF16))
            s_scr[h] = (s_mid + _dot_ta(vh, ki)) * e_second

    on = _head_rmsnorm_gate(o_scr[...], g, ng_ref[...], GLA_HEADS, GLA_DV, 1.0)
    f = _dot(on.astype(BF16), wout_ref[...])
    y_ref[0] = _residual_update(x, f, p_ref[0], lng_ref[...], lnb_ref[...], wgate_ref[...], wproj_ref[...])

    @pl.when(blk == pl.num_programs(1) - 1)
    def _():
        for h in range(GLA_HEADS):
            st_ref[0, h] = s_scr[h].T


def _const_spec(shape):
    nd = len(shape)
    return pl.BlockSpec(shape, lambda *_: (0,) * nd)


def _gla_weights(w_in, w_a2, b_a, norm_g, w_out, ln_g, ln_b, w_gate, w_proj):
    hk, hv = GLA_HK, GLA_HV
    wq = w_in[:, :hk].astype(BF16)
    wk = w_in[:, hk:2 * hk].astype(BF16)
    wv = w_in[:, 2 * hk:2 * hk + hv].astype(BF16)
    wg = w_in[:, 2 * hk + hv:2 * hk + 2 * hv].astype(BF16)
    wa = jnp.pad(w_in[:, 2 * hk + 2 * hv:], ((0, 0), (0, LANES - GLA_LOWRANK))).astype(BF16)
    wa2 = jnp.pad(w_a2, ((0, LANES - GLA_LOWRANK), (0, 0))).astype(BF16)
    return (wq, wk, wv, wg, wa, wa2, b_a.reshape(1, hk), norm_g.reshape(1, hv),
            w_out.astype(BF16), ln_g.reshape(1, D_MODEL), ln_b.reshape(1, D_MODEL),
            w_gate.astype(BF16), w_proj.astype(BF16))


def gla_layer_prompt(x, p, weights, *, block=256, chunk=GLA_CHUNK):
    bsz, t, _ = x.shape
    assert t % block == 0 and block % chunk == 0
    in_specs = [pl.BlockSpec((1, block, D_MODEL), lambda b, i: (b, i, 0)),
                pl.BlockSpec((1, block, PLE_DIM), lambda b, i: (b, i, 0))]
    in_specs += [_const_spec(w.shape) for w in weights]
    return pl.pallas_call(
        functools.partial(_gla_prompt_kernel, chunk=chunk),
        grid=(bsz, t // block),
        in_specs=in_specs,
        out_specs=[pl.BlockSpec((1, block, D_MODEL), lambda b, i: (b, i, 0)),
                   pl.BlockSpec((1, GLA_HEADS, GLA_DK, GLA_DV), lambda b, i: (b, 0, 0, 0))],
        out_shape=[jax.ShapeDtypeStruct((bsz, t, D_MODEL), F32),
                   jax.ShapeDtypeStruct((bsz, GLA_HEADS, GLA_DK, GLA_DV), F32)],
        scratch_shapes=[pltpu.VMEM((GLA_HEADS, GLA_DV, GLA_DK), F32),
                        pltpu.VMEM((block, GLA_HV), F32)],
        compiler_params=pltpu.CompilerParams(dimension_semantics=("arbitrary", "arbitrary"),
                                             vmem_limit_bytes=VMEM_LIMIT),
        name="gla_prompt",
    )(x, p, *weights)


def _gla_sample_kernel(x_ref, p_ref, s0_ref, wq_ref, wk_ref, wv_ref, wg_ref, wa_ref, wa2_ref, ba_ref, ng_ref,
                       wout_ref, lng_ref, lnb_ref, wgate_ref, wproj_ref,
                       y_ref, st_ref, o_scr, *, seq):
    rows = x_ref.shape[0]
    n_seq = rows // seq
    half = seq // 2
    x = x_ref[...]
    xb = x.astype(BF16)
    q, k, v, g, log_a = _gla_project(xb, wq_ref[...], wk_ref[...], wv_ref[...], wg_ref[...],
                                     wa_ref[...], wa2_ref[...], ba_ref[...])
    rel, _ = _chunk_decay_matrices(rows, seq)
    la_hi, la_mid = _split_bf16(log_a)
    d = _dot(rel, la_hi) + _dot(rel, la_mid)
    s_i = lax.broadcasted_iota(jnp.int32, (n_seq, rows), 0)
    t_i = lax.broadcasted_iota(jnp.int32, (n_seq, rows), 1)
    in_seq = (t_i // seq) == s_i
    sel_first = jnp.where(in_seq & ((t_i % seq) < half), 1.0, 0.0).astype(BF16)
    sel_second = jnp.where(in_seq & ((t_i % seq) >= half), 1.0, 0.0).astype(BF16)
    t_c = lax.broadcasted_iota(jnp.int32, (rows, n_seq), 0)
    s_c = lax.broadcasted_iota(jnp.int32, (rows, n_seq), 1)
    sel_tot_t = jnp.where((t_c // seq) == s_c, 1.0, 0.0).astype(BF16)
    e_first = jnp.exp(_dot(sel_first, la_hi) + _dot(sel_first, la_mid))
    e_second = jnp.exp(_dot(sel_second, la_hi) + _dot(sel_second, la_mid))
    e_tot_col = jnp.exp(_dot_ta(la_hi, sel_tot_t) + _dot_ta(la_mid, sel_tot_t))
    q_dec = q * jnp.exp(d)
    k_inv = k * jnp.exp(-d)

    ci = lax.broadcasted_iota(jnp.int32, (seq, seq), 0)
    cj = lax.broadcasted_iota(jnp.int32, (seq, seq), 1)
    causal = ci >= cj
    for c in range(n_seq):
        r0 = c * seq
        for h in range(GLA_HEADS):
            ks = slice(h * GLA_DK, (h + 1) * GLA_DK)
            vs = slice(h * GLA_DV, (h + 1) * GLA_DV)
            qd = q_dec[r0:r0 + seq, ks]
            ki = k_inv[r0:r0 + seq, ks]
            vh = v[r0:r0 + seq, vs].astype(BF16)
            s_old = s0_ref[c, h]
            att = jnp.where(causal, _dot_tb(qd.astype(BF16), ki.astype(BF16)), 0.0).astype(BF16)
            q_mid = (qd * e_first[c:c + 1, ks]).astype(BF16)
            o_scr[r0:r0 + seq, vs] = _dot(att, vh) + _dot(q_mid, s_old.astype(BF16))
            k_end = (ki * e_second[c:c + 1, ks]).astype(BF16)
            st_ref[c, h] = s_old * e_tot_col[h * GLA_DK:(h + 1) * GLA_DK, c:c + 1] + _dot_ta(k_end, vh)

    on = _head_rmsnorm_gate(o_scr[...], g, ng_ref[...], GLA_HEADS, GLA_DV, 1.0)
    f = _dot(on.astype(BF16), wout_ref[...])
    y_ref[...] = _residual_update(x, f, p_ref[...], lng_ref[...], lnb_ref[...], wgate_ref[...], wproj_ref[...])


def gla_layer_sample(x, p, s0, weights, *, group=8):
    n, seq, _ = x.shape
    assert n % group == 0 and seq % 2 == 0
    rows = group * seq
    in_specs = [pl.BlockSpec((rows, D_MODEL), lambda i: (i, 0)),
                pl.BlockSpec((rows, PLE_DIM), lambda i: (i, 0)),
                pl.BlockSpec((group, GLA_HEADS, GLA_DK, GLA_DV), lambda i: (i, 0, 0, 0))]
    in_specs += [_const_spec(w.shape) for w in weights]
    y, st = pl.pallas_call(
        functools.partial(_gla_sample_kernel, seq=seq),
        grid=(n // group,),
        in_specs=in_specs,
        out_specs=[pl.BlockSpec((rows, D_MODEL), lambda i: (i, 0)),
                   pl.BlockSpec((group, GLA_HEADS, GLA_DK, GLA_DV), lambda i: (i, 0, 0, 0))],
        out_shape=[jax.ShapeDtypeStruct((n * seq, D_MODEL), F32),
                   jax.ShapeDtypeStruct((n, GLA_HEADS, GLA_DK, GLA_DV), F32)],
        scratch_shapes=[pltpu.VMEM((rows, GLA_HV), F32)],
        compiler_params=pltpu.CompilerParams(dimension_semantics=("arbitrary",),
                                             vmem_limit_bytes=VMEM_LIMIT),
        name="gla_sample",
    )(x.reshape(n * seq, D_MODEL), p.reshape(n * seq, PLE_DIM), s0, *weights)
    return y.reshape(n, seq, D_MODEL), st


def _resident_spec(shape):
    nd = len(shape)
    return pl.BlockSpec(shape, lambda *_: (0,) * nd, pipeline_mode=pl.Buffered(1))


def _t5_bias(rel_bias, dist):
    max_exact = N_BUCKETS // 2
    n = jnp.maximum(dist, 0)
    nf = jnp.maximum(n, 1).astype(F32)
    large = max_exact + (jnp.log(nf / max_exact) / math.log(MAX_DISTANCE / max_exact)
                         * (N_BUCKETS - max_exact)).astype(jnp.int32)
    large = jnp.minimum(large, N_BUCKETS - 1)
    bucket = jnp.where(n < max_exact, n, large)[None]
    table = rel_bias.astype(F32)
    per_head = (DIFF_HEADS,) + (1,) * dist.ndim
    out = jnp.zeros((DIFF_HEADS,) + dist.shape, F32)
    for b in range(N_BUCKETS):
        out = jnp.where(bucket == b, table[b].reshape(per_head), out)
    return out


def _store_token_head_rows(ref, lead, x):
    tokens = x.shape[0]
    for h in range(DIFF_HEADS):
        ref[(*lead, pl.ds(h, tokens, stride=DIFF_HEADS), slice(None))] = x[:, h * DIFF_VD:(h + 1) * DIFF_VD]


def _load_token_head_rows(ref, lead, tokens):
    return jnp.concatenate(
        [ref[(*lead, pl.ds(h, tokens, stride=DIFF_HEADS), slice(None))] for h in range(DIFF_HEADS)], axis=1)


def _diff_lambda(lam_ref, lam_init):
    lv = lam_ref[...]
    a = jnp.sum(lv[0:1] * lv[1:2], axis=-1, keepdims=True)
    b = jnp.sum(lv[2:3] * lv[3:4], axis=-1, keepdims=True)
    return jnp.exp(a) - jnp.exp(b) + lam_init


def _diff_prompt_kernel(x_ref, p_ref, wq_ref, wk_ref, wv_ref, wg_ref, lam_ref, bias_ref, ng_ref,
                        wout_ref, lng_ref, lnb_ref, wgate_ref, wproj_ref,
                        y_ref, kout_ref, vout_ref,
                        k_scr, vt_scr, q_scr, o_scr, acc_scr, m_scr, alpha_scr, *bufs,
                        lam_init, heads_per_iter):
    i = pl.program_id(1)
    tq = x_ref.shape[1]
    x = x_ref[0]
    xb = x.astype(BF16)
    q = _dot(xb, wq_ref[...]) * (DIFF_DH ** -0.5 * LOG2_E)
    k = _dot(xb, wk_ref[...])
    v = _dot(xb, wv_ref[...])
    _store_token_head_rows(kout_ref, (0,), k)
    _store_token_head_rows(vout_ref, (0,), v)
    first_half = lax.broadcasted_iota(jnp.int32, (DIFF_VD, tq), 0) < DIFF_DH
    row0 = pl.multiple_of(i * tq, tq)
    for h in range(DIFF_HEADS):
        hs = slice(h * DIFF_VD, (h + 1) * DIFF_VD)
        qh_t = q[:, hs].T
        q_scr[h, :, 0:tq] = jnp.where(first_half, qh_t, 0.0).astype(BF16)
        q_scr[h, :, tq:2 * tq] = jnp.where(first_half, 0.0, qh_t).astype(BF16)
        k_scr[h, pl.ds(row0, tq), :] = k[:, hs].astype(BF16)
        vt_scr[h, i, 0:DIFF_VD, :] = v[:, hs].T.astype(BF16)
        vt_scr[h, i, DIFF_VD:, :] = jnp.ones((SUM_ROWS, tq), BF16)
    lam = _diff_lambda(lam_ref, lam_init)

    s_buf = [bufs[2 * u:2 * u + 2] for u in range(heads_per_iter)]
    p_buf = [bufs[2 * heads_per_iter + 2 * u:2 * heads_per_iter + 2 * u + 2] for u in range(heads_per_iter)]

    def block_of(t):
        near = jnp.maximum(i - t, 0)
        if isinstance(t, int):
            return near if t < 2 else jnp.minimum(t - 2, i)
        return jnp.where(t < 2, near, jnp.minimum(t - 2, i))

    def group_body(grp, carry):
        heads = [grp * heads_per_iter + u for u in range(heads_per_iter)]
        m_scr[...] = jnp.full(m_scr.shape, MASK_VALUE, F32)
        acc_scr[...] = jnp.zeros(acc_scr.shape, F32)

        def scores(t, slot):
            rows = pl.ds(pl.multiple_of(block_of(t) * tq, tq), tq)
            for u, h in enumerate(heads):
                s_buf[u][slot][...] = _dot(k_scr[h, rows, :], q_scr[h])

        def softmax(t, slot, bias_slot):
            valid = t <= i
            for u, h in enumerate(heads):
                s = s_buf[u][slot][...]
                if bias_slot is not None:
                    tile = bias_ref[h, bias_slot]
                    s = s + jnp.concatenate([tile, tile], axis=1)
                m_old = m_scr[u]
                m_new = jnp.where(valid, jnp.maximum(m_old, jnp.max(s, axis=0, keepdims=True)), m_old)
                alpha = jnp.exp2(m_old - m_new)
                pr = jnp.exp2(s - jnp.where(valid, m_new, -MASK_VALUE))
                m_scr[u] = m_new
                alpha_scr[u] = alpha
                p_buf[u][slot][...] = pr.astype(BF16)

        def weighted_values(t, slot):
            j = block_of(t)
            for u, h in enumerate(heads):
                acc_scr[u] = acc_scr[u] * alpha_scr[u] + _dot(vt_scr[h, j], p_buf[u][slot][...])

        def step(t, slot, bias_slot):
            weighted_values(t - 1, 1 - slot)
            scores(t + 1, 1 - slot)
            softmax(t, slot, bias_slot)

        scores(0, 0)
        softmax(0, 0, 0)
        scores(1, 1)
        step(1, 1, 1)

        def two_far_steps(r, c):
            step(2 * r, 0, None)
            step(2 * r + 1, 1, None)
            return c

        n_pairs = (i + 2) // 2
        lax.fori_loop(1, n_pairs, two_far_steps, 0)
        weighted_values(2 * n_pairs - 1, 1)

        for u, h in enumerate(heads):
            inv_l = 1.0 / acc_scr[u, DIFF_VD:DIFF_VD + 1, :]
            acc = acc_scr[u, 0:DIFF_VD, :]
            o_t = acc[:, :tq] * inv_l[:, :tq] - lam * (acc[:, tq:] * inv_l[:, tq:])
            o_scr[h] = o_t.T
        return carry

    lax.fori_loop(0, DIFF_HEADS // heads_per_iter, group_body, 0)
    o = jnp.concatenate([o_scr[h] for h in range(DIFF_HEADS)], axis=-1)
    gate = _dot(xb, wg_ref[...])
    on = _head_rmsnorm_gate(o, gate, ng_ref[...], DIFF_HEADS, DIFF_VD, 1.0 - lam_init)
    f = _dot(on.astype(BF16), wout_ref[...])
    y_ref[0] = _residual_update(x, f, p_ref[0], lng_ref[...], lnb_ref[...], wgate_ref[...], wproj_ref[...])


def _diff_weights(w_in, lam_q1, lam_k1, lam_q2, lam_k2, norm_g, w_out, ln_g, ln_b, w_gate, w_proj):
    w = DIFF_WIDTH
    return dict(
        wq=w_in[:, :w].astype(BF16), wk=w_in[:, w:2 * w].astype(BF16),
        wv=w_in[:, 2 * w:3 * w].astype(BF16), wg=w_in[:, 3 * w:].astype(BF16),
        lam=jnp.stack([lam_q1, lam_k1, lam_q2, lam_k2]).astype(F32),
        ng=norm_g.reshape(1, w), wout=w_out.astype(BF16),
        lng=ln_g.reshape(1, D_MODEL), lnb=ln_b.reshape(1, D_MODEL),
        wgate=w_gate.astype(BF16), wproj=w_proj.astype(BF16))


def diff_layer_prompt(x, p, dw, rel_bias, lam_init, *, block=256, heads_per_iter=2):
    bsz, t, _ = x.shape
    assert t % block == 0 and block >= MAX_DISTANCE and DIFF_HEADS % heads_per_iter == 0
    kk = jnp.arange(block, dtype=jnp.int32)[:, None]
    qq = jnp.arange(block, dtype=jnp.int32)[None, :]
    dist = jnp.stack([qq - kk, block + qq - kk])
    far = rel_bias.astype(F32)[N_BUCKETS - 1].reshape(DIFF_HEADS, 1, 1, 1)
    bias = jnp.where(dist >= 0, (_t5_bias(rel_bias, dist) - far) * LOG2_E, MASK_VALUE)
    tok = lambda width: pl.BlockSpec((1, block, width), lambda b, i: (b, i, 0))
    kv_spec = pl.BlockSpec((1, block * DIFF_HEADS, DIFF_VD), lambda b, i: (b, i, 0))
    consts = [dw["wq"], dw["wk"], dw["wv"], dw["wg"], dw["lam"], bias, dw["ng"], dw["wout"],
              dw["lng"], dw["lnb"], dw["wgate"], dw["wproj"]]
    return pl.pallas_call(
        functools.partial(_diff_prompt_kernel, lam_init=lam_init, heads_per_iter=heads_per_iter),
        grid=(bsz, t // block),
        in_specs=[tok(D_MODEL), tok(PLE_DIM)] + [_resident_spec(c.shape) for c in consts],
        out_specs=[tok(D_MODEL), kv_spec, kv_spec],
        out_shape=[jax.ShapeDtypeStruct((bsz, t, D_MODEL), F32),
                   jax.ShapeDtypeStruct((bsz, t * DIFF_HEADS, DIFF_VD), F32),
                   jax.ShapeDtypeStruct((bsz, t * DIFF_HEADS, DIFF_VD), F32)],
        scratch_shapes=[pltpu.VMEM((DIFF_HEADS, t, DIFF_VD), BF16),
                        pltpu.VMEM((DIFF_HEADS, t // block, DIFF_VD + SUM_ROWS, block), BF16),
                        pltpu.VMEM((DIFF_HEADS, DIFF_VD, 2 * block), BF16),
                        pltpu.VMEM((DIFF_HEADS, block, DIFF_VD), F32),
                        pltpu.VMEM((heads_per_iter, DIFF_VD + SUM_ROWS, 2 * block), F32),
                        pltpu.VMEM((heads_per_iter, 1, 2 * block), F32),
                        pltpu.VMEM((heads_per_iter, 1, 2 * block), F32),
                        *[pltpu.VMEM((block, 2 * block), F32)] * (2 * heads_per_iter),
                        *[pltpu.VMEM((block, 2 * block), BF16)] * (2 * heads_per_iter)],
        compiler_params=pltpu.CompilerParams(dimension_semantics=("arbitrary", "arbitrary"),
                                             vmem_limit_bytes=VMEM_LIMIT),
        name="diff_prompt",
    )(x, p, *consts)


def _diff_sample_project_kernel(x_ref, wq_ref, wk_ref, wv_ref, qt_ref, k_ref, v_ref):
    xb = x_ref[...].astype(BF16)
    qt_ref[...] = (_dot(xb, wq_ref[...]) * (DIFF_DH ** -0.5)).T.astype(BF16)
    _store_token_head_rows(k_ref, (), _dot(xb, wk_ref[...]))
    _store_token_head_rows(v_ref, (), _dot(xb, wv_ref[...]))


def _row_to_col(row, n):
    eye = lax.broadcasted_iota(jnp.int32, (n, n), 0) == lax.broadcasted_iota(jnp.int32, (n, n), 1)
    return jnp.sum(jnp.where(eye, row, 0.0), axis=1, keepdims=True)


def _paged_attn_kernel(pt_ref, qt_ref, kn_ref, vn_ref, lam_ref, bias_ref, bnew_ref, *rest, pages, seq, lam_init):
    k_pages = rest[:pages]
    v_pages = rest[pages:2 * pages]
    o_ref = rest[2 * pages]
    wq_scr, acc_scr, m_scr, l_scr = rest[2 * pages + 1:]
    b = pl.program_id(0)
    g = pl.program_id(1)
    ncol = 2 * DIFF_HEADS * seq

    @pl.when(g == 0)
    def _():
        local = (b % (LANES // seq)) * seq
        src = lax.broadcasted_iota(jnp.int32, (LANES, ncol), 0)
        col = lax.broadcasted_iota(jnp.int32, (LANES, ncol), 1)
        pick = jnp.where(src == local + col % seq, 1.0, 0.0).astype(BF16)
        rep = _dot(qt_ref[...], pick)
        r = lax.broadcasted_iota(jnp.int32, (DIFF_WIDTH, ncol), 0)
        c = lax.broadcasted_iota(jnp.int32, (DIFF_WIDTH, ncol), 1)
        own = ((r // DIFF_VD) == ((c % (DIFF_HEADS * seq)) // seq)) & \
              (((r % DIFF_VD) // DIFF_DH) == (c // (DIFF_HEADS * seq)))
        wq_scr[...] = jnp.where(own, rep, 0.0).astype(BF16)
        m_scr[...] = jnp.full(m_scr.shape, MASK_VALUE, F32)
        l_scr[...] = jnp.zeros(l_scr.shape, F32)
        acc_scr[...] = jnp.zeros(acc_scr.shape, F32)

    def flash_step(scores, values):
        m_old = m_scr[...]
        m_new = m_old
        for s in scores:
            m_new = jnp.maximum(m_new, jnp.max(s, axis=0, keepdims=True))
        alpha = jnp.exp(m_old - m_new)
        l_new = alpha * l_scr[...]
        pv = None
        for s, val in zip(scores, values):
            pr = jnp.exp(s - m_new)
            l_new = l_new + jnp.sum(pr, axis=0, keepdims=True)
            t = _dot_ta(pr.astype(BF16), val)
            pv = t if pv is None else pv + t
        acc_scr[...] = acc_scr[...] * _row_to_col(alpha, ncol) + pv
        l_scr[...] = l_new
        m_scr[...] = m_new

    def load_page(ref):
        return _load_token_head_rows(ref, (0,), PAGE_SIZE).astype(BF16)

    wq = wq_scr[...]
    scores, values = [], []
    for i in range(pages):
        s = _dot(load_page(k_pages[i]), wq)
        scores.append(s + bias_ref[0] if i == pages - 1 else s)
        values.append(load_page(v_pages[i]))
    flash_step(scores, values)

    @pl.when(g == pl.num_programs(1) - 1)
    def _():
        pad = jnp.zeros((16 - seq, DIFF_WIDTH), F32)
        kn = jnp.concatenate([_load_token_head_rows(kn_ref, (0,), seq), pad], axis=0).astype(BF16)
        vn = jnp.concatenate([_load_token_head_rows(vn_ref, (0,), seq), pad], axis=0).astype(BF16)
        flash_step([_dot(kn, wq) + bnew_ref[...]], [vn])
        lam = _diff_lambda(lam_ref, lam_init)
        inv_l = _row_to_col(1.0 / l_scr[...], ncol)
        half_rows = DIFF_HEADS * seq
        for h in range(DIFF_HEADS):
            cs = slice(h * DIFF_VD, (h + 1) * DIFF_VD)
            r1 = slice(h * seq, (h + 1) * seq)
            r2 = slice(half_rows + h * seq, half_rows + (h + 1) * seq)
            o_ref[0, :, cs] = acc_scr[r1, cs] * inv_l[r1] - lam * (acc_scr[r2, cs] * inv_l[r2])


def _diff_sample_out_kernel(x_ref, o_ref, p_ref, wg_ref, ng_ref, wout_ref, lng_ref, lnb_ref, wgate_ref, wproj_ref,
                            y_ref, *, lam_init):
    x = x_ref[...]
    gate = _dot(x.astype(BF16), wg_ref[...])
    on = _head_rmsnorm_gate(o_ref[...], gate, ng_ref[...], DIFF_HEADS, DIFF_VD, 1.0 - lam_init)
    f = _dot(on.astype(BF16), wout_ref[...])
    y_ref[...] = _residual_update(x, f, p_ref[...], lng_ref[...], lnb_ref[...], wgate_ref[...], wproj_ref[...])


def diff_layer_sample(x, p, cache_k, cache_v, page_table, dw, rel_bias, lam_init, *, pages=16, block=256):
    n, seq, _ = x.shape
    n_pages = page_table.shape[1]
    rows = n * seq
    assert n_pages % pages == 0 and rows % block == 0 and LANES % seq == 0 and seq <= 16
    assert PAGE_SIZE >= MAX_DISTANCE
    x2 = x.reshape(rows, D_MODEL)
    row_spec = lambda width: pl.BlockSpec((block, width), lambda i: (i, 0))
    qt, kn, vn = pl.pallas_call(
        _diff_sample_project_kernel,
        grid=(rows // block,),
        in_specs=[row_spec(D_MODEL)] + [_const_spec((D_MODEL, DIFF_WIDTH))] * 3,
        out_specs=[pl.BlockSpec((DIFF_WIDTH, block), lambda i: (0, i)),
                   pl.BlockSpec((block * DIFF_HEADS, DIFF_VD), lambda i: (i, 0)),
                   pl.BlockSpec((block * DIFF_HEADS, DIFF_VD), lambda i: (i, 0))],
        out_shape=[jax.ShapeDtypeStruct((DIFF_WIDTH, rows), BF16),
                   jax.ShapeDtypeStruct((rows * DIFF_HEADS, DIFF_VD), F32),
                   jax.ShapeDtypeStruct((rows * DIFF_HEADS, DIFF_VD), F32)],
        compiler_params=pltpu.CompilerParams(dimension_semantics=("arbitrary",), vmem_limit_bytes=VMEM_LIMIT),
        name="diff_sample_project",
    )(x2, dw["wq"], dw["wk"], dw["wv"])

    ncol = 2 * DIFF_HEADS * seq
    col = jnp.arange(ncol, dtype=jnp.int32)[None, :]
    col_h = (col % (DIFF_HEADS * seq)) // seq
    col_t = col % seq
    kk = jnp.arange(PAGE_SIZE, dtype=jnp.int32)[:, None]
    tk = jnp.arange(16, dtype=jnp.int32)[:, None]
    dist_last = PAGE_SIZE + col_t - kk
    dist_new = col_t - tk
    far = rel_bias.astype(F32)[N_BUCKETS - 1].reshape(DIFF_HEADS, 1, 1)

    def own_head(per_head):
        return sum(jnp.where(col_h == h, per_head[h], 0.0) for h in range(DIFF_HEADS))

    bias_last = own_head(_t5_bias(rel_bias, dist_last) - far)
    bias_pages = jnp.stack([jnp.zeros_like(bias_last), bias_last])
    bias_new = jnp.where((tk < seq) & (dist_new >= 0), own_head(_t5_bias(rel_bias, dist_new) - far), MASK_VALUE)

    n_groups = n_pages // pages
    seq_per_blk = LANES // seq

    def page_spec(i):
        return pl.BlockSpec((1, PAGE_SIZE * DIFF_HEADS, DIFF_VD),
                            lambda b, g, pt: (pt[b * n_pages + g * pages + i], 0, 0))

    tok_spec = pl.BlockSpec((1, seq, DIFF_WIDTH), lambda b, g, pt: (b, 0, 0))
    new_spec = pl.BlockSpec((1, seq * DIFF_HEADS, DIFF_VD), lambda b, g, pt: (b, 0, 0))
    grid_spec = pltpu.PrefetchScalarGridSpec(
        num_scalar_prefetch=1,
        grid=(n, n_groups),
        in_specs=[pl.BlockSpec((DIFF_WIDTH, LANES), lambda b, g, pt: (0, b // seq_per_blk)),
                  new_spec, new_spec,
                  pl.BlockSpec((4, DIFF_DH), lambda b, g, pt: (0, 0)),
                  pl.BlockSpec((1, PAGE_SIZE, ncol), lambda b, g, pt: ((g + 1) // n_groups, 0, 0)),
                  pl.BlockSpec((16, ncol), lambda b, g, pt: (0, 0))]
                 + [page_spec(i) for i in range(pages)] + [page_spec(i) for i in range(pages)],
        out_specs=tok_spec,
        scratch_shapes=[pltpu.VMEM((DIFF_WIDTH, ncol), BF16),
                        pltpu.VMEM((ncol, DIFF_WIDTH), F32),
                        pltpu.VMEM((1, ncol), F32),
                        pltpu.VMEM((1, ncol), F32)])
    o = pl.pallas_call(
        functools.partial(_paged_attn_kernel, pages=pages, seq=seq, lam_init=lam_init),
        grid_spec=grid_spec,
        out_shape=jax.ShapeDtypeStruct((n, seq, DIFF_WIDTH), F32),
        compiler_params=pltpu.CompilerParams(dimension_semantics=("arbitrary", "arbitrary"),
                                             vmem_limit_bytes=VMEM_LIMIT),
        name="diff_sample_attn",
    )(page_table.reshape(-1), qt, kn.reshape(n, seq * DIFF_HEADS, DIFF_VD), vn.reshape(n, seq * DIFF_HEADS, DIFF_VD),
      dw["lam"], bias_pages, bias_new, *([cache_k] * pages), *([cache_v] * pages))

    consts = [dw["wg"], dw["ng"], dw["wout"], dw["lng"], dw["lnb"], dw["wgate"], dw["wproj"]]
    y = pl.pallas_call(
        functools.partial(_diff_sample_out_kernel, lam_init=lam_init),
        grid=(rows // block,),
        in_specs=[row_spec(D_MODEL), row_spec(DIFF_WIDTH), row_spec(PLE_DIM)] + [_const_spec(c.shape) for c in consts],
        out_specs=row_spec(D_MODEL),
        out_shape=jax.ShapeDtypeStruct((rows, D_MODEL), F32),
        compiler_params=pltpu.CompilerParams(dimension_semantics=("arbitrary",), vmem_limit_bytes=VMEM_LIMIT),
        name="diff_sample_out",
    )(x2, o.reshape(rows, DIFF_WIDTH), p.reshape(rows, PLE_DIM), *consts)
    return y.reshape(n, seq, D_MODEL), kn, vn


def kernel(x_prompt, x_sample, state_gla, cache_k, cache_v, page_table, p_prompt, p_sample, rel_bias,
           gla_w_in, gla_w_a2, gla_b_a, gla_norm_g, gla_w_out,
           diff_w_in, diff_lam_q1, diff_lam_k1, diff_lam_q2, diff_lam_k2, diff_norm_g, diff_w_out,
           ln_g, ln_b, ple_w_proj, ple_w_gate):
    w0 = _gla_weights(gla_w_in[0], gla_w_a2[0], gla_b_a[0], gla_norm_g[0], gla_w_out[0],
                      ln_g[0], ln_b[0], ple_w_gate[0], ple_w_proj[0])
    xp1, sp = gla_layer_prompt(x_prompt, p_prompt[0], w0)
    xs1, ss = gla_layer_sample(x_sample, p_sample[0], state_gla[0], w0)
    lam_init = 0.8 - 0.6 * math.exp(-0.3 * 1)
    dw = _diff_weights(diff_w_in[0], diff_lam_q1[0], diff_lam_k1[0], diff_lam_q2[0], diff_lam_k2[0],
                       diff_norm_g[0], diff_w_out[0], ln_g[1], ln_b[1], ple_w_gate[1], ple_w_proj[1])
    yp, kp, vp = diff_layer_prompt(xp1, p_prompt[1], dw, rel_bias, lam_init)
    pool = cache_k.shape[1]
    ys, ks, vs = diff_layer_sample(xs1, p_sample[1],
                                   cache_k[0].reshape(pool, PAGE_SIZE * DIFF_HEADS, DIFF_VD),
                                   cache_v[0].reshape(pool, PAGE_SIZE * DIFF_HEADS, DIFF_VD),
                                   page_table, dw, rel_bias, lam_init)
    bsz, t, _ = x_prompt.shape
    n, seq, _ = x_sample.shape
    heads = (DIFF_HEADS, DIFF_VD)
    return (yp, ys, sp[None], ss[None],
            kp.reshape(1, bsz, t, *heads), vp.reshape(1, bsz, t, *heads),
            ks.reshape(1, n, seq, *heads), vs.reshape(1, n, seq, *heads))
```

```python
import functools
import math

import jax
import jax.numpy as jnp
from jax import lax
from jax.experimental import pallas as pl
from jax.experimental.pallas import tpu as pltpu

F32 = jnp.float32
BF16 = jnp.bfloat16

D_MODEL = 1024
DEPTH = 2
GLA_HEADS = 4
GLA_DK = 128
GLA_DV = 256
GLA_HK = GLA_HEADS * GLA_DK
GLA_HV = GLA_HEADS * GLA_DV
GLA_LOWRANK = 16
GLA_TAU = 16.0
DIFF_HEADS = 8
DIFF_DH = 64
DIFF_VD = 128
DIFF_WIDTH = DIFF_HEADS * DIFF_VD
N_BUCKETS = 32
MAX_DISTANCE = 128
PLE_DIM = 256
PAGE_SIZE = 128
ALPHA = (2 * DEPTH) ** 0.25
EPS = 1e-5

LANES = 128
GLA_CHUNK = 64
VMEM_LIMIT = 56 * 1024 * 1024
MASK_VALUE = -1e30
LOG2_E = math.log2(math.e)
SUM_ROWS = 16


def _dot(a, b):
    return jnp.dot(a, b, preferred_element_type=F32)


def _dot_tb(a, b):
    return lax.dot_general(a, b, (((1,), (1,)), ((), ())), preferred_element_type=F32)


def _dot_ta(a, b):
    return lax.dot_general(a, b, (((0,), (0,)), ((), ())), preferred_element_type=F32)


def _split_bf16(x):
    hi = x.astype(BF16)
    mid = (x - hi.astype(F32)).astype(BF16)
    return hi, mid


def _log_sigmoid(z):
    return jnp.minimum(z, 0.0) - jnp.log(1.0 + jnp.exp(-jnp.abs(z)))


def _sigmoid(z):
    return 1.0 / (1.0 + jnp.exp(-z))


def _head_rmsnorm_gate(o, gate, norm_g, n_heads, head_dim, scale):
    parts = []
    for h in range(n_heads):
        oh = o[:, h * head_dim:(h + 1) * head_dim]
        ms = jnp.mean(oh * oh, axis=-1, keepdims=True)
        parts.append(oh * lax.rsqrt(ms + EPS))
    on = jnp.concatenate(parts, axis=-1) * norm_g
    if scale != 1.0:
        on = on * scale
    return on * (gate * _sigmoid(gate))


def _residual_update(x, f, p, ln_g, ln_b, w_gate, w_proj):
    hp = ALPHA * x + f
    mu = jnp.mean(hp, axis=-1, keepdims=True)
    hc = hp - mu
    var = jnp.mean(hc * hc, axis=-1, keepdims=True)
    h = hc * lax.rsqrt(var + EPS) * ln_g + ln_b
    gate = _sigmoid(_dot(h.astype(BF16), w_gate))
    return h + gate * _dot(p.astype(BF16), w_proj)


def _gla_project(xb, wq, wk, wv, wg, wa, wa2, ba):
    q = _dot(xb, wq) * (GLA_DK ** -0.5)
    k = _dot(xb, wk)
    v = _dot(xb, wv)
    g = _dot(xb, wg)
    a_lr = _dot(xb, wa)
    z = _dot(a_lr.astype(BF16), wa2) + ba
    log_a = _log_sigmoid(z) * (1.0 / GLA_TAU)
    return q, k, v, g, log_a


def _chunk_decay_matrices(rows, chunk):
    half = chunk // 2
    i = lax.broadcasted_iota(jnp.int32, (rows, rows), 0)
    j = lax.broadcasted_iota(jnp.int32, (rows, rows), 1)
    same = (i // chunk) == (j // chunk)
    jl = j % chunk
    il = i % chunk
    pos = same & (jl >= half) & (jl <= il)
    neg = same & (jl < half) & (jl > il)
    rel = jnp.where(pos, 1.0, jnp.where(neg, -1.0, 0.0)).astype(BF16)
    n_sel = max(8, 2 * rows // chunk)
    s = lax.broadcasted_iota(jnp.int32, (n_sel, rows), 0)
    t = lax.broadcasted_iota(jnp.int32, (n_sel, rows), 1)
    halves = jnp.where((t // half) == s, 1.0, 0.0).astype(BF16)
    return rel, halves


def _gla_prompt_kernel(x_ref, p_ref, wq_ref, wk_ref, wv_ref, wg_ref, wa_ref, wa2_ref, ba_ref, ng_ref,
                       wout_ref, lng_ref, lnb_ref, wgate_ref, wproj_ref,
                       y_ref, st_ref, s_scr, o_scr, *, chunk):
    blk = pl.program_id(1)
    rows = x_ref.shape[1]

    @pl.when(blk == 0)
    def _():
        s_scr[...] = jnp.zeros_like(s_scr)

    x = x_ref[0]
    xb = x.astype(BF16)
    q, k, v, g, log_a = _gla_project(xb, wq_ref[...], wk_ref[...], wv_ref[...], wg_ref[...],
                                     wa_ref[...], wa2_ref[...], ba_ref[...])
    rel, halves = _chunk_decay_matrices(rows, chunk)
    la_hi, la_mid = _split_bf16(log_a)
    d = _dot(rel, la_hi) + _dot(rel, la_mid)
    hs = _dot(halves, la_hi) + _dot(halves, la_mid)
    ehs = jnp.exp(hs)
    q_dec = (q * jnp.exp(d)).astype(BF16)
    k_inv = (k * jnp.exp(-d)).astype(BF16)
    vb = v.astype(BF16)

    ci = lax.broadcasted_iota(jnp.int32, (chunk, chunk), 0)
    cj = lax.broadcasted_iota(jnp.int32, (chunk, chunk), 1)
    causal = ci >= cj
    for c in range(rows // chunk):
        r0 = c * chunk
        for h in range(GLA_HEADS):
            ks = slice(h * GLA_DK, (h + 1) * GLA_DK)
            vs = slice(h * GLA_DV, (h + 1) * GLA_DV)
            qd = q_dec[r0:r0 + chunk, ks]
            ki = k_inv[r0:r0 + chunk, ks]
            vh = vb[r0:r0 + chunk, vs]
            e_first = ehs[2 * c:2 * c + 1, ks]
            e_second = ehs[2 * c + 1:2 * c + 2, ks]
            s_mid = s_scr[h] * e_first
            att = jnp.where(causal, _dot_tb(qd, ki), 0.0).astype(BF16)
            o_scr[r0:r0 + chunk, vs] = _dot(att, vh) + _dot_tb(qd, s_mid.astype(BF16))
            s_scr[h] = (s_mid + _dot_ta(vh, ki)) * e_second

    on = _head_rmsnorm_gate(o_scr[...], g, ng_ref[...], GLA_HEADS, GLA_DV, 1.0)
    f = _dot(on.astype(BF16), wout_ref[...])
    y_ref[0] = _residual_update(x, f, p_ref[0], lng_ref[...], lnb_ref[...], wgate_ref[...], wproj_ref[...])

    @pl.when(blk == pl.num_programs(1) - 1)
    def _():
        for h in range(GLA_HEADS):
            st_ref[0, h] = s_scr[h].T


def _const_spec(shape):
    nd = len(shape)
    return pl.BlockSpec(shape, lambda *_: (0,) * nd)


def _gla_weights(w_in, w_a2, b_a, norm_g, w_out, ln_g, ln_b, w_gate, w_proj):
    hk, hv = GLA_HK, GLA_HV
    wq = w_in[:, :hk].astype(BF16)
    wk = w_in[:, hk:2 * hk].astype(BF16)
    wv = w_in[:, 2 * hk:2 * hk + hv].astype(BF16)
    wg = w_in[:, 2 * hk + hv:2 * hk + 2 * hv].astype(BF16)
    wa = jnp.pad(w_in[:, 2 * hk + 2 * hv:], ((0, 0), (0, LANES - GLA_LOWRANK))).astype(BF16)
    wa2 = jnp.pad(w_a2, ((0, LANES - GLA_LOWRANK), (0, 0))).astype(BF16)
    return (wq, wk, wv, wg, wa, wa2, b_a.reshape(1, hk), norm_g.reshape(1, hv),
            w_out.astype(BF16), ln_g.reshape(1, D_MODEL), ln_b.reshape(1, D_MODEL),
            w_gate.astype(BF16), w_proj.astype(BF16))


def gla_layer_prompt(x, p, weights, *, block=256, chunk=GLA_CHUNK):
    bsz, t, _ = x.shape
    assert t % block == 0 and block % chunk == 0
    in_specs = [pl.BlockSpec((1, block, D_MODEL), lambda b, i: (b, i, 0)),
                pl.BlockSpec((1, block, PLE_DIM), lambda b, i: (b, i, 0))]
    in_specs += [_const_spec(w.shape) for w in weights]
    return pl.pallas_call(
        functools.partial(_gla_prompt_kernel, chunk=chunk),
        grid=(bsz, t // block),
        in_specs=in_specs,
        out_specs=[pl.BlockSpec((1, block, D_MODEL), lambda b, i: (b, i, 0)),
                   pl.BlockSpec((1, GLA_HEADS, GLA_DK, GLA_DV), lambda b, i: (b, 0, 0, 0))],
        out_shape=[jax.ShapeDtypeStruct((bsz, t, D_MODEL), F32),
                   jax.ShapeDtypeStruct((bsz, GLA_HEADS, GLA_DK, GLA_DV), F32)],
        scratch_shapes=[pltpu.VMEM((GLA_HEADS, GLA_DV, GLA_DK), F32),
                        pltpu.VMEM((block, GLA_HV), F32)],
        compiler_params=pltpu.CompilerParams(dimension_semantics=("arbitrary", "arbitrary"),
                                             vmem_limit_bytes=VMEM_LIMIT),
        name="gla_prompt",
    )(x, p, *weights)


def _gla_sample_kernel(x_ref, p_ref, s0_ref, wq_ref, wk_ref, wv_ref, wg_ref, wa_ref, wa2_ref, ba_ref, ng_ref,
                       wout_ref, lng_ref, lnb_ref, wgate_ref, wproj_ref,
                       y_ref, st_ref, o_scr, *, seq):
    rows = x_ref.shape[0]
    n_seq = rows // seq
    half = seq // 2
    x = x_ref[...]
    xb = x.astype(BF16)
    q, k, v, g, log_a = _gla_project(xb, wq_ref[...], wk_ref[...], wv_ref[...], wg_ref[...],
                                     wa_ref[...], wa2_ref[...], ba_ref[...])
    rel, _ = _chunk_decay_matrices(rows, seq)
    la_hi, la_mid = _split_bf16(log_a)
    d = _dot(rel, la_hi) + _dot(rel, la_mid)
    s_i = lax.broadcasted_iota(jnp.int32, (n_seq, rows), 0)
    t_i = lax.broadcasted_iota(jnp.int32, (n_seq, rows), 1)
    in_seq = (t_i // seq) == s_i
    sel_first = jnp.where(in_seq & ((t_i % seq) < half), 1.0, 0.0).astype(BF16)
    sel_second = jnp.where(in_seq & ((t_i % seq) >= half), 1.0, 0.0).astype(BF16)
    t_c = lax.broadcasted_iota(jnp.int32, (rows, n_seq), 0)
    s_c = lax.broadcasted_iota(jnp.int32, (rows, n_seq), 1)
    sel_tot_t = jnp.where((t_c // seq) == s_c, 1.0, 0.0).astype(BF16)
    e_first = jnp.exp(_dot(sel_first, la_hi) + _dot(sel_first, la_mid))
    e_second = jnp.exp(_dot(sel_second, la_hi) + _dot(sel_second, la_mid))
    e_tot_col = jnp.exp(_dot_ta(la_hi, sel_tot_t) + _dot_ta(la_mid, sel_tot_t))
    q_dec = q * jnp.exp(d)
    k_inv = k * jnp.exp(-d)

    ci = lax.broadcasted_iota(jnp.int32, (seq, seq), 0)
    cj = lax.broadcasted_iota(jnp.int32, (seq, seq), 1)
    causal = ci >= cj
    for c in range(n_seq):
        r0 = c * seq
        for h in range(GLA_HEADS):
            ks = slice(h * GLA_DK, (h + 1) * GLA_DK)
            vs = slice(h * GLA_DV, (h + 1) * GLA_DV)
            qd = q_dec[r0:r0 + seq, ks]
            ki = k_inv[r0:r0 + seq, ks]
            vh = v[r0:r0 + seq, vs].astype(BF16)
            s_old = s0_ref[c, h]
            att = jnp.where(causal, _dot_tb(qd.astype(BF16), ki.astype(BF16)), 0.0).astype(BF16)
            q_mid = (qd * e_first[c:c + 1, ks]).astype(BF16)
            o_scr[r0:r0 + seq, vs] = _dot(att, vh) + _dot(q_mid, s_old.astype(BF16))
            k_end = (ki * e_second[c:c + 1, ks]).astype(BF16)
            st_ref[c, h] = s_old * e_tot_col[h * GLA_DK:(h + 1) * GLA_DK, c:c + 1] + _dot_ta(k_end, vh)

    on = _head_rmsnorm_gate(o_scr[...], g, ng_ref[...], GLA_HEADS, GLA_DV, 1.0)
    f = _dot(on.astype(BF16), wout_ref[...])
    y_ref[...] = _residual_update(x, f, p_ref[...], lng_ref[...], lnb_ref[...], wgate_ref[...], wproj_ref[...])


def gla_layer_sample(x, p, s0, weights, *, group=16):
    n, seq, _ = x.shape
    assert n % group == 0 and seq % 2 == 0
    rows = group * seq
    in_specs = [pl.BlockSpec((rows, D_MODEL), lambda i: (i, 0)),
                pl.BlockSpec((rows, PLE_DIM), lambda i: (i, 0)),
                pl.BlockSpec((group, GLA_HEADS, GLA_DK, GLA_DV), lambda i: (i, 0, 0, 0))]
    in_specs += [_resident_spec(w.shape) for w in weights]
    y, st = pl.pallas_call(
        functools.partial(_gla_sample_kernel, seq=seq),
        grid=(n // group,),
        in_specs=in_specs,
        out_specs=[pl.BlockSpec((rows, D_MODEL), lambda i: (i, 0)),
                   pl.BlockSpec((group, GLA_HEADS, GLA_DK, GLA_DV), lambda i: (i, 0, 0, 0))],
        out_shape=[jax.ShapeDtypeStruct((n * seq, D_MODEL), F32),
                   jax.ShapeDtypeStruct((n, GLA_HEADS, GLA_DK, GLA_DV), F32)],
        scratch_shapes=[pltpu.VMEM((rows, GLA_HV), F32)],
        compiler_params=pltpu.CompilerParams(dimension_semantics=("arbitrary",),
                                             vmem_limit_bytes=VMEM_LIMIT),
        name="gla_sample",
    )(x.reshape(n * seq, D_MODEL), p.reshape(n * seq, PLE_DIM), s0, *weights)
    return y.reshape(n, seq, D_MODEL), st


def _resident_spec(shape):
    nd = len(shape)
    return pl.BlockSpec(shape, lambda *_: (0,) * nd, pipeline_mode=pl.Buffered(1))


def _t5_bias(rel_bias, dist):
    max_exact = N_BUCKETS // 2
    n = jnp.maximum(dist, 0)[None]
    nf = jnp.maximum(n, 1).astype(F32)
    steps = jnp.log(nf / max_exact) / math.log(MAX_DISTANCE / max_exact) * (N_BUCKETS - max_exact)
    table = rel_bias.astype(F32)
    per_head = (DIFF_HEADS,) + (1,) * dist.ndim
    out = jnp.broadcast_to(table[N_BUCKETS - 1].reshape(per_head), (DIFF_HEADS,) + dist.shape)
    for b in range(N_BUCKETS - 2, max_exact - 1, -1):
        out = jnp.where(steps < (b + 1 - max_exact), table[b].reshape(per_head), out)
    for b in range(max_exact):
        out = jnp.where(n == b, table[b].reshape(per_head), out)
    return out


def _store_token_head_rows(ref, lead, x):
    tokens = x.shape[0]
    for h in range(DIFF_HEADS):
        ref[(*lead, pl.ds(h, tokens, stride=DIFF_HEADS), slice(None))] = x[:, h * DIFF_VD:(h + 1) * DIFF_VD]


def _load_token_head_rows(ref, lead, tokens):
    return jnp.concatenate(
        [ref[(*lead, pl.ds(h, tokens, stride=DIFF_HEADS), slice(None))] for h in range(DIFF_HEADS)], axis=1)


def _diff_lambda(lam_ref, lam_init):
    lv = lam_ref[...]
    a = jnp.sum(lv[0:1] * lv[1:2], axis=-1, keepdims=True)
    b = jnp.sum(lv[2:3] * lv[3:4], axis=-1, keepdims=True)
    return jnp.exp(a) - jnp.exp(b) + lam_init


def _diff_prompt_kernel(x_ref, p_ref, wq_ref, wk_ref, wv_ref, wg_ref, lam_ref, bias_ref, ng_ref,
                        wout_ref, lng_ref, lnb_ref, wgate_ref, wproj_ref,
                        y_ref, kout_ref, vout_ref,
                        k_scr, vt_scr, q_scr, o_scr, acc_scr, m_scr, alpha_scr, *bufs,
                        lam_init, heads_per_iter):
    i = pl.program_id(1)
    tq = x_ref.shape[1]
    x = x_ref[0]
    xb = x.astype(BF16)
    q = _dot(xb, wq_ref[...]) * (DIFF_DH ** -0.5 * LOG2_E)
    k = _dot(xb, wk_ref[...])
    v = _dot(xb, wv_ref[...])
    _store_token_head_rows(kout_ref, (0,), k)
    _store_token_head_rows(vout_ref, (0,), v)
    first_half = lax.broadcasted_iota(jnp.int32, (DIFF_VD, tq), 0) < DIFF_DH
    row0 = pl.multiple_of(i * tq, tq)
    for h in range(DIFF_HEADS):
        hs = slice(h * DIFF_VD, (h + 1) * DIFF_VD)
        qh_t = q[:, hs].T
        q_scr[h, :, 0:tq] = jnp.where(first_half, qh_t, 0.0).astype(BF16)
        q_scr[h, :, tq:2 * tq] = jnp.where(first_half, 0.0, qh_t).astype(BF16)
        k_scr[h, pl.ds(row0, tq), :] = k[:, hs].astype(BF16)
        vt_scr[h, i, 0:DIFF_VD, :] = v[:, hs].T.astype(BF16)
        vt_scr[h, i, DIFF_VD:, :] = jnp.ones((SUM_ROWS, tq), BF16)
    lam = _diff_lambda(lam_ref, lam_init)

    s_buf = [bufs[2 * u:2 * u + 2] for u in range(heads_per_iter)]
    p_buf = [bufs[2 * heads_per_iter + 2 * u:2 * heads_per_iter + 2 * u + 2] for u in range(heads_per_iter)]

    def block_of(t):
        near = jnp.maximum(i - t, 0)
        if isinstance(t, int):
            return near if t < 2 else jnp.minimum(t - 2, i)
        return jnp.where(t < 2, near, jnp.minimum(t - 2, i))

    def group_body(grp, carry):
        heads = [grp * heads_per_iter + u for u in range(heads_per_iter)]
        m_scr[...] = jnp.full(m_scr.shape, MASK_VALUE, F32)
        acc_scr[...] = jnp.zeros(acc_scr.shape, F32)

        def scores(t, slot):
            rows = pl.ds(pl.multiple_of(block_of(t) * tq, tq), tq)
            for u, h in enumerate(heads):
                s_buf[u][slot][...] = _dot(k_scr[h, rows, :], q_scr[h])

        def softmax(t, slot, bias_slot):
            valid = t <= i
            for u, h in enumerate(heads):
                s = s_buf[u][slot][...]
                if bias_slot is not None:
                    tile = bias_ref[h, bias_slot]
                    s = s + jnp.concatenate([tile, tile], axis=1)
                m_old = m_scr[u]
                m_new = jnp.where(valid, jnp.maximum(m_old, jnp.max(s, axis=0, keepdims=True)), m_old)
                alpha = jnp.exp2(m_old - m_new)
                pr = jnp.exp2(s - jnp.where(valid, m_new, -MASK_VALUE))
                m_scr[u] = m_new
                alpha_scr[u] = alpha
                p_buf[u][slot][...] = pr.astype(BF16)

        def weighted_values(t, slot):
            j = block_of(t)
            for u, h in enumerate(heads):
                acc_scr[u] = acc_scr[u] * alpha_scr[u] + _dot(vt_scr[h, j], p_buf[u][slot][...])

        def step(t, slot, bias_slot):
            weighted_values(t - 1, 1 - slot)
            scores(t + 1, 1 - slot)
            softmax(t, slot, bias_slot)

        scores(0, 0)
        softmax(0, 0, 0)
        scores(1, 1)
        step(1, 1, 1)

        def two_far_steps(r, c):
            step(2 * r, 0, None)
            step(2 * r + 1, 1, None)
            return c

        n_pairs = (i + 2) // 2
        lax.fori_loop(1, n_pairs, two_far_steps, 0)
        weighted_values(2 * n_pairs - 1, 1)

        for u, h in enumerate(heads):
            inv_l = 1.0 / acc_scr[u, DIFF_VD:DIFF_VD + 1, :]
            acc = acc_scr[u, 0:DIFF_VD, :]
            o_t = acc[:, :tq] * inv_l[:, :tq] - lam * (acc[:, tq:] * inv_l[:, tq:])
            o_scr[h] = o_t.T
        return carry

    lax.fori_loop(0, DIFF_HEADS // heads_per_iter, group_body, 0)
    o = jnp.concatenate([o_scr[h] for h in range(DIFF_HEADS)], axis=-1)
    gate = _dot(xb, wg_ref[...])
    on = _head_rmsnorm_gate(o, gate, ng_ref[...], DIFF_HEADS, DIFF_VD, 1.0 - lam_init)
    f = _dot(on.astype(BF16), wout_ref[...])
    y_ref[0] = _residual_update(x, f, p_ref[0], lng_ref[...], lnb_ref[...], wgate_ref[...], wproj_ref[...])


def _diff_weights(w_in, lam_q1, lam_k1, lam_q2, lam_k2, norm_g, w_out, ln_g, ln_b, w_gate, w_proj):
    w = DIFF_WIDTH
    return dict(
        wq=w_in[:, :w].astype(BF16), wk=w_in[:, w:2 * w].astype(BF16),
        wv=w_in[:, 2 * w:3 * w].astype(BF16), wg=w_in[:, 3 * w:].astype(BF16),
        lam=jnp.stack([lam_q1, lam_k1, lam_q2, lam_k2]).astype(F32),
        ng=norm_g.reshape(1, w), wout=w_out.astype(BF16),
        lng=ln_g.reshape(1, D_MODEL), lnb=ln_b.reshape(1, D_MODEL),
        wgate=w_gate.astype(BF16), wproj=w_proj.astype(BF16))


def diff_layer_prompt(x, p, dw, rel_bias, lam_init, *, block=256, heads_per_iter=2):
    bsz, t, _ = x.shape
    assert t % block == 0 and block >= MAX_DISTANCE and DIFF_HEADS % heads_per_iter == 0
    kk = jnp.arange(block, dtype=jnp.int32)[:, None]
    qq = jnp.arange(block, dtype=jnp.int32)[None, :]
    dist = jnp.stack([qq - kk, block + qq - kk])
    far = rel_bias.astype(F32)[N_BUCKETS - 1].reshape(DIFF_HEADS, 1, 1, 1)
    bias = jnp.where(dist >= 0, (_t5_bias(rel_bias, dist) - far) * LOG2_E, MASK_VALUE)
    tok = lambda width: pl.BlockSpec((1, block, width), lambda b, i: (b, i, 0))
    kv_spec = pl.BlockSpec((1, block * DIFF_HEADS, DIFF_VD), lambda b, i: (b, i, 0))
    consts = [dw["wq"], dw["wk"], dw["wv"], dw["wg"], dw["lam"], bias, dw["ng"], dw["wout"],
              dw["lng"], dw["lnb"], dw["wgate"], dw["wproj"]]
    return pl.pallas_call(
        functools.partial(_diff_prompt_kernel, lam_init=lam_init, heads_per_iter=heads_per_iter),
        grid=(bsz, t // block),
        in_specs=[tok(D_MODEL), tok(PLE_DIM)] + [_resident_spec(c.shape) for c in consts],
        out_specs=[tok(D_MODEL), kv_spec, kv_spec],
        out_shape=[jax.ShapeDtypeStruct((bsz, t, D_MODEL), F32),
                   jax.ShapeDtypeStruct((bsz, t * DIFF_HEADS, DIFF_VD), F32),
                   jax.ShapeDtypeStruct((bsz, t * DIFF_HEADS, DIFF_VD), F32)],
        scratch_shapes=[pltpu.VMEM((DIFF_HEADS, t, DIFF_VD), BF16),
                        pltpu.VMEM((DIFF_HEADS, t // block, DIFF_VD + SUM_ROWS, block), BF16),
                        pltpu.VMEM((DIFF_HEADS, DIFF_VD, 2 * block), BF16),
                        pltpu.VMEM((DIFF_HEADS, block, DIFF_VD), F32),
                        pltpu.VMEM((heads_per_iter, DIFF_VD + SUM_ROWS, 2 * block), F32),
                        pltpu.VMEM((heads_per_iter, 1, 2 * block), F32),
                        pltpu.VMEM((heads_per_iter, 1, 2 * block), F32),
                        *[pltpu.VMEM((block, 2 * block), F32)] * (2 * heads_per_iter),
                        *[pltpu.VMEM((block, 2 * block), BF16)] * (2 * heads_per_iter)],
        compiler_params=pltpu.CompilerParams(dimension_semantics=("arbitrary", "arbitrary"),
                                             vmem_limit_bytes=VMEM_LIMIT),
        name="diff_prompt",
    )(x, p, *consts)


def _diff_sample_project_kernel(x_ref, wq_ref, wk_ref, wv_ref, qt_ref, k_ref, v_ref):
    xb = x_ref[...].astype(BF16)
    qt_ref[...] = (_dot(xb, wq_ref[...]) * (DIFF_DH ** -0.5)).T.astype(BF16)
    _store_token_head_rows(k_ref, (), _dot(xb, wk_ref[...]))
    _store_token_head_rows(v_ref, (), _dot(xb, wv_ref[...]))


def _row_to_col(row, n):
    eye = lax.broadcasted_iota(jnp.int32, (n, n), 0) == lax.broadcasted_iota(jnp.int32, (n, n), 1)
    return jnp.sum(jnp.where(eye, row, 0.0), axis=1, keepdims=True)


def _paged_attn_kernel(pt_ref, qt_ref, kn_ref, vn_ref, lam_ref, bias_ref, bnew_ref, *rest, pages, seq, lam_init):
    k_pages = rest[:pages]
    v_pages = rest[pages:2 * pages]
    o_ref = rest[2 * pages]
    wq_scr, acc_scr, m_scr, l_scr = rest[2 * pages + 1:]
    b = pl.program_id(0)
    g = pl.program_id(1)
    ncol = 2 * DIFF_HEADS * seq

    @pl.when(g == 0)
    def _():
        local = (b % (LANES // seq)) * seq
        src = lax.broadcasted_iota(jnp.int32, (LANES, ncol), 0)
        col = lax.broadcasted_iota(jnp.int32, (LANES, ncol), 1)
        pick = jnp.where(src == local + col % seq, 1.0, 0.0).astype(BF16)
        rep = _dot(qt_ref[...], pick)
        r = lax.broadcasted_iota(jnp.int32, (DIFF_WIDTH, ncol), 0)
        c = lax.broadcasted_iota(jnp.int32, (DIFF_WIDTH, ncol), 1)
        own = ((r // DIFF_VD) == ((c % (DIFF_HEADS * seq)) // seq)) & \
              (((r % DIFF_VD) // DIFF_DH) == (c // (DIFF_HEADS * seq)))
        wq_scr[...] = jnp.where(own, rep, 0.0).astype(BF16)
        m_scr[...] = jnp.full(m_scr.shape, MASK_VALUE, F32)
        l_scr[...] = jnp.zeros(l_scr.shape, F32)
        acc_scr[...] = jnp.zeros(acc_scr.shape, F32)

    def flash_step(scores, values):
        m_old = m_scr[...]
        m_new = m_old
        for s in scores:
            m_new = jnp.maximum(m_new, jnp.max(s, axis=0, keepdims=True))
        alpha = jnp.exp(m_old - m_new)
        l_new = alpha * l_scr[...]
        probs = []
        for s in scores:
            pr = jnp.exp(s - m_new)
            l_new = l_new + jnp.sum(pr, axis=0, keepdims=True)
            probs.append(pr.astype(BF16))
        pv = None
        for a in range(0, len(probs), 2):
            t = _dot_ta(jnp.concatenate(probs[a:a + 2], axis=0),
                        jnp.concatenate(values[a:a + 2], axis=0))
            pv = t if pv is None else pv + t
        acc_scr[...] = acc_scr[...] * _row_to_col(alpha, ncol) + pv
        l_scr[...] = l_new
        m_scr[...] = m_new

    def load_page(ref):
        return _load_token_head_rows(ref, (0,), PAGE_SIZE).astype(BF16)

    wq = wq_scr[...]
    scores, values = [], []
    for i in range(pages):
        s = _dot(load_page(k_pages[i]), wq)
        scores.append(s + bias_ref[0] if i == pages - 1 else s)
        values.append(load_page(v_pages[i]))
    flash_step(scores, values)

    @pl.when(g == pl.num_programs(1) - 1)
    def _():
        pad = jnp.zeros((16 - seq, DIFF_WIDTH), F32)
        kn = jnp.concatenate([_load_token_head_rows(kn_ref, (0,), seq), pad], axis=0).astype(BF16)
        vn = jnp.concatenate([_load_token_head_rows(vn_ref, (0,), seq), pad], axis=0).astype(BF16)
        flash_step([_dot(kn, wq) + bnew_ref[...]], [vn])
        lam = _diff_lambda(lam_ref, lam_init)
        inv_l = _row_to_col(1.0 / l_scr[...], ncol)
        half_rows = DIFF_HEADS * seq
        for h in range(DIFF_HEADS):
            cs = slice(h * DIFF_VD, (h + 1) * DIFF_VD)
            r1 = slice(h * seq, (h + 1) * seq)
            r2 = slice(half_rows + h * seq, half_rows + (h + 1) * seq)
            o_ref[0, :, cs] = acc_scr[r1, cs] * inv_l[r1] - lam * (acc_scr[r2, cs] * inv_l[r2])


def _diff_sample_out_kernel(x_ref, o_ref, p_ref, wg_ref, ng_ref, wout_ref, lng_ref, lnb_ref, wgate_ref, wproj_ref,
                            y_ref, *, lam_init):
    x = x_ref[...]
    gate = _dot(x.astype(BF16), wg_ref[...])
    on = _head_rmsnorm_gate(o_ref[...], gate, ng_ref[...], DIFF_HEADS, DIFF_VD, 1.0 - lam_init)
    f = _dot(on.astype(BF16), wout_ref[...])
    y_ref[...] = _residual_update(x, f, p_ref[...], lng_ref[...], lnb_ref[...], wgate_ref[...], wproj_ref[...])


def diff_layer_sample(x, p, cache_k, cache_v, page_table, dw, rel_bias, lam_init, *, pages=16, block=256):
    n, seq, _ = x.shape
    n_pages = page_table.shape[1]
    rows = n * seq
    assert n_pages % pages == 0 and rows % block == 0 and LANES % seq == 0 and seq <= 16
    assert PAGE_SIZE >= MAX_DISTANCE
    x2 = x.reshape(rows, D_MODEL)
    row_spec = lambda width: pl.BlockSpec((block, width), lambda i: (i, 0))
    qt, kn, vn = pl.pallas_call(
        _diff_sample_project_kernel,
        grid=(rows // block,),
        in_specs=[row_spec(D_MODEL)] + [_const_spec((D_MODEL, DIFF_WIDTH))] * 3,
        out_specs=[pl.BlockSpec((DIFF_WIDTH, block), lambda i: (0, i)),
                   pl.BlockSpec((block * DIFF_HEADS, DIFF_VD), lambda i: (i, 0)),
                   pl.BlockSpec((block * DIFF_HEADS, DIFF_VD), lambda i: (i, 0))],
        out_shape=[jax.ShapeDtypeStruct((DIFF_WIDTH, rows), BF16),
                   jax.ShapeDtypeStruct((rows * DIFF_HEADS, DIFF_VD), F32),
                   jax.ShapeDtypeStruct((rows * DIFF_HEADS, DIFF_VD), F32)],
        compiler_params=pltpu.CompilerParams(dimension_semantics=("arbitrary",), vmem_limit_bytes=VMEM_LIMIT),
        name="diff_sample_project",
    )(x2, dw["wq"], dw["wk"], dw["wv"])

    ncol = 2 * DIFF_HEADS * seq
    col = jnp.arange(ncol, dtype=jnp.int32)[None, :]
    col_h = (col % (DIFF_HEADS * seq)) // seq
    col_t = col % seq
    kk = jnp.arange(PAGE_SIZE, dtype=jnp.int32)[:, None]
    tk = jnp.arange(16, dtype=jnp.int32)[:, None]
    dist_last = PAGE_SIZE + col_t - kk
    dist_new = col_t - tk
    far = rel_bias.astype(F32)[N_BUCKETS - 1].reshape(DIFF_HEADS, 1, 1)

    def own_head(per_head):
        return sum(jnp.where(col_h == h, per_head[h], 0.0) for h in range(DIFF_HEADS))

    bias_last = own_head(_t5_bias(rel_bias, dist_last) - far)
    bias_pages = jnp.stack([jnp.zeros_like(bias_last), bias_last])
    bias_new = jnp.where((tk < seq) & (dist_new >= 0), own_head(_t5_bias(rel_bias, dist_new) - far), MASK_VALUE)

    n_groups = n_pages // pages
    seq_per_blk = LANES // seq

    def page_spec(i):
        return pl.BlockSpec((1, PAGE_SIZE * DIFF_HEADS, DIFF_VD),
                            lambda b, g, pt: (pt[b * n_pages + g * pages + i], 0, 0))

    tok_spec = pl.BlockSpec((1, seq, DIFF_WIDTH), lambda b, g, pt: (b, 0, 0))
    new_spec = pl.BlockSpec((1, seq * DIFF_HEADS, DIFF_VD), lambda b, g, pt: (b, 0, 0))
    grid_spec = pltpu.PrefetchScalarGridSpec(
        num_scalar_prefetch=1,
        grid=(n, n_groups),
        in_specs=[pl.BlockSpec((DIFF_WIDTH, LANES), lambda b, g, pt: (0, b // seq_per_blk)),
                  new_spec, new_spec,
                  pl.BlockSpec((4, DIFF_DH), lambda b, g, pt: (0, 0)),
                  pl.BlockSpec((1, PAGE_SIZE, ncol), lambda b, g, pt: ((g + 1) // n_groups, 0, 0)),
                  pl.BlockSpec((16, ncol), lambda b, g, pt: (0, 0))]
                 + [page_spec(i) for i in range(pages)] + [page_spec(i) for i in range(pages)],
        out_specs=tok_spec,
        scratch_shapes=[pltpu.VMEM((DIFF_WIDTH, ncol), BF16),
                        pltpu.VMEM((ncol, DIFF_WIDTH), F32),
                        pltpu.VMEM((1, ncol), F32),
                        pltpu.VMEM((1, ncol), F32)])
    o = pl.pallas_call(
        functools.partial(_paged_attn_kernel, pages=pages, seq=seq, lam_init=lam_init),
        grid_spec=grid_spec,
        out_shape=jax.ShapeDtypeStruct((n, seq, DIFF_WIDTH), F32),
        compiler_params=pltpu.CompilerParams(dimension_semantics=("arbitrary", "arbitrary"),
                                             vmem_limit_bytes=VMEM_LIMIT),
        name="diff_sample_attn",
    )(page_table.reshape(-1), qt, kn.reshape(n, seq * DIFF_HEADS, DIFF_VD), vn.reshape(n, seq * DIFF_HEADS, DIFF_VD),
      dw["lam"], bias_pages, bias_new, *([cache_k] * pages), *([cache_v] * pages))

    consts = [dw["wg"], dw["ng"], dw["wout"], dw["lng"], dw["lnb"], dw["wgate"], dw["wproj"]]
    y = pl.pallas_call(
        functools.partial(_diff_sample_out_kernel, lam_init=lam_init),
        grid=(rows // block,),
        in_specs=[row_spec(D_MODEL), row_spec(DIFF_WIDTH), row_spec(PLE_DIM)] + [_const_spec(c.shape) for c in consts],
        out_specs=row_spec(D_MODEL),
        out_shape=jax.ShapeDtypeStruct((rows, D_MODEL), F32),
        compiler_params=pltpu.CompilerParams(dimension_semantics=("arbitrary",), vmem_limit_bytes=VMEM_LIMIT),
        name="diff_sample_out",
    )(x2, o.reshape(rows, DIFF_WIDTH), p.reshape(rows, PLE_DIM), *consts)
    return y.reshape(n, seq, D_MODEL), kn, vn


def kernel(x_prompt, x_sample, state_gla, cache_k, cache_v, page_table, p_prompt, p_sample, rel_bias,
           gla_w_in, gla_w_a2, gla_b_a, gla_norm_g, gla_w_out,
           diff_w_in, diff_lam_q1, diff_lam_k1, diff_lam_q2, diff_lam_k2, diff_norm_g, diff_w_out,
           ln_g, ln_b, ple_w_proj, ple_w_gate):
    w0 = _gla_weights(gla_w_in[0], gla_w_a2[0], gla_b_a[0], gla_norm_g[0], gla_w_out[0],
                      ln_g[0], ln_b[0], ple_w_gate[0], ple_w_proj[0])
    xp1, sp = gla_layer_prompt(x_prompt, p_prompt[0], w0)
    xs1, ss = gla_layer_sample(x_sample, p_sample[0], state_gla[0], w0)
    lam_init = 0.8 - 0.6 * math.exp(-0.3 * 1)
    dw = _diff_weights(diff_w_in[0], diff_lam_q1[0], diff_lam_k1[0], diff_lam_q2[0], diff_lam_k2[0],
                       diff_norm_g[0], diff_w_out[0], ln_g[1], ln_b[1], ple_w_gate[1], ple_w_proj[1])
    yp, kp, vp = diff_layer_prompt(xp1, p_prompt[1], dw, rel_bias, lam_init)
    pool = cache_k.shape[1]
    ys, ks, vs = diff_layer_sample(xs1, p_sample[1],
                                   cache_k[0].reshape(pool, PAGE_SIZE * DIFF_HEADS, DIFF_VD),
                                   cache_v[0].reshape(pool, PAGE_SIZE * DIFF_HEADS, DIFF_VD),
                                   page_table, dw, rel_bias, lam_init)
    bsz, t, _ = x_prompt.shape
    n, seq, _ = x_sample.shape
    heads = (DIFF_HEADS, DIFF_VD)
    return (yp, ys, sp[None], ss[None],
            kp.reshape(1, bsz, t, *heads), vp.reshape(1, bsz, t, *heads),
            ks.reshape(1, n, seq, *heads), vs.reshape(1, n, seq, *heads))
```

```python
import functools
import math

import jax
import jax.numpy as jnp
from jax import lax
from jax.experimental import pallas as pl
from jax.experimental.pallas import tpu as pltpu

F32 = jnp.float32
BF16 = jnp.bfloat16

D_MODEL = 1024
DEPTH = 2
GLA_HEADS = 4
GLA_DK = 128
GLA_DV = 256
GLA_HK = GLA_HEADS * GLA_DK
GLA_HV = GLA_HEADS * GLA_DV
GLA_LOWRANK = 16
GLA_TAU = 16.0
DIFF_HEADS = 8
DIFF_DH = 64
DIFF_VD = 128
DIFF_WIDTH = DIFF_HEADS * DIFF_VD
N_BUCKETS = 32
MAX_DISTANCE = 128
PLE_DIM = 256
PAGE_SIZE = 128
ALPHA = (2 * DEPTH) ** 0.25
EPS = 1e-5

LANES = 128
GLA_CHUNK = 64
VMEM_LIMIT = 56 * 1024 * 1024
MASK_VALUE = -1e30
LOG2_E = math.log2(math.e)
SUM_ROWS = 16


def _dot(a, b):
    return jnp.dot(a, b, preferred_element_type=F32)


def _dot_tb(a, b):
    return lax.dot_general(a, b, (((1,), (1,)), ((), ())), preferred_element_type=F32)


def _dot_ta(a, b):
    return lax.dot_general(a, b, (((0,), (0,)), ((), ())), preferred_element_type=F32)


def _split_bf16(x):
    hi = x.astype(BF16)
    mid = (x - hi.astype(F32)).astype(BF16)
    return hi, mid


def _log_sigmoid(z):
    return jnp.minimum(z, 0.0) - jnp.log(1.0 + jnp.exp(-jnp.abs(z)))


def _sigmoid(z):
    return 1.0 / (1.0 + jnp.exp(-z))


def _head_rmsnorm_gate(o, gate, norm_g, n_heads, head_dim, scale):
    parts = []
    for h in range(n_heads):
        oh = o[:, h * head_dim:(h + 1) * head_dim]
        ms = jnp.mean(oh * oh, axis=-1, keepdims=True)
        parts.append(oh * lax.rsqrt(ms + EPS))
    on = jnp.concatenate(parts, axis=-1) * norm_g
    if scale != 1.0:
        on = on * scale
    return on * (gate * _sigmoid(gate))


def _residual_update(x, f, p, ln_g, ln_b, w_gate, w_proj):
    hp = ALPHA * x + f
    mu = jnp.mean(hp, axis=-1, keepdims=True)
    hc = hp - mu
    var = jnp.mean(hc * hc, axis=-1, keepdims=True)
    h = hc * lax.rsqrt(var + EPS) * ln_g + ln_b
    gate = _sigmoid(_dot(h.astype(BF16), w_gate))
    return h + gate * _dot(p.astype(BF16), w_proj)


def _gla_project(xb, wq, wk, wv, wg, wa, wa2, ba):
    q = _dot(xb, wq) * (GLA_DK ** -0.5)
    k = _dot(xb, wk)
    v = _dot(xb, wv)
    g = _dot(xb, wg)
    a_lr = _dot(xb, wa)
    z = _dot(a_lr.astype(BF16), wa2) + ba
    log_a = _log_sigmoid(z) * (1.0 / GLA_TAU)
    return q, k, v, g, log_a


def _chunk_decay_matrices(rows, chunk):
    half = chunk // 2
    i = lax.broadcasted_iota(jnp.int32, (rows, rows), 0)
    j = lax.broadcasted_iota(jnp.int32, (rows, rows), 1)
    same = (i // chunk) == (j // chunk)
    jl = j % chunk
    il = i % chunk
    pos = same & (jl >= half) & (jl <= il)
    neg = same & (jl < half) & (jl > il)
    rel = jnp.where(pos, 1.0, jnp.where(neg, -1.0, 0.0)).astype(BF16)
    n_sel = max(8, 2 * rows // chunk)
    s = lax.broadcasted_iota(jnp.int32, (n_sel, rows), 0)
    t = lax.broadcasted_iota(jnp.int32, (n_sel, rows), 1)
    halves = jnp.where((t // half) == s, 1.0, 0.0).astype(BF16)
    return rel, halves


def _gla_prompt_kernel(x_ref, p_ref, wq_ref, wk_ref, wv_ref, wg_ref, wa_ref, wa2_ref, ba_ref, ng_ref,
                       wout_ref, lng_ref, lnb_ref, wgate_ref, wproj_ref,
                       y_ref, st_ref, s_scr, o_scr, *, chunk):
    blk = pl.program_id(1)
    rows = o_scr.shape[1]

    @pl.when(blk == 0)
    def _():
        s_scr[...] = jnp.zeros_like(s_scr)

    ci = lax.broadcasted_iota(jnp.int32, (chunk, chunk), 0)
    cj = lax.broadcasted_iota(jnp.int32, (chunk, chunk), 1)
    causal = ci >= cj
    rel, halves = _chunk_decay_matrices(rows, chunk)
    for sb in range(x_ref.shape[1] // rows):
        tok = slice(sb * rows, (sb + 1) * rows)
        x = x_ref[0, tok, :]
        xb = x.astype(BF16)
        q, k, v, g, log_a = _gla_project(xb, wq_ref[...], wk_ref[...], wv_ref[...], wg_ref[...],
                                         wa_ref[...], wa2_ref[...], ba_ref[...])
        la_hi, la_mid = _split_bf16(log_a)
        d = _dot(rel, la_hi) + _dot(rel, la_mid)
        hs = _dot(halves, la_hi) + _dot(halves, la_mid)
        ehs = jnp.exp(hs)
        q_dec = (q * jnp.exp(d)).astype(BF16)
        k_inv = (k * jnp.exp(-d)).astype(BF16)
        vb = v.astype(BF16)

        for c in range(rows // chunk):
            r0 = c * chunk
            for h in range(GLA_HEADS):
                ks = slice(h * GLA_DK, (h + 1) * GLA_DK)
                vs = slice(h * GLA_DV, (h + 1) * GLA_DV)
                qd = q_dec[r0:r0 + chunk, ks]
                ki = k_inv[r0:r0 + chunk, ks]
                vh = vb[r0:r0 + chunk, vs]
                e_first = ehs[2 * c:2 * c + 1, ks]
                e_second = ehs[2 * c + 1:2 * c + 2, ks]
                s_mid = s_scr[h] * e_first
                att = jnp.where(causal, _dot_tb(qd, ki), 0.0).astype(BF16)
                o_scr[sb, r0:r0 + chunk, vs] = _dot(att, vh) + _dot_tb(qd, s_mid.astype(BF16))
                s_scr[h] = (s_mid + _dot_ta(vh, ki)) * e_second

        on = _head_rmsnorm_gate(o_scr[sb], g, ng_ref[...], GLA_HEADS, GLA_DV, 1.0)
        f = _dot(on.astype(BF16), wout_ref[...])
        y_ref[0, tok, :] = _residual_update(x, f, p_ref[0, tok, :], lng_ref[...], lnb_ref[...],
                                            wgate_ref[...], wproj_ref[...])

    @pl.when(blk == pl.num_programs(1) - 1)
    def _():
        for h in range(GLA_HEADS):
            st_ref[0, h] = s_scr[h].T


def _const_spec(shape):
    nd = len(shape)
    return pl.BlockSpec(shape, lambda *_: (0,) * nd)


def _gla_weights(w_in, w_a2, b_a, norm_g, w_out, ln_g, ln_b, w_gate, w_proj):
    hk, hv = GLA_HK, GLA_HV
    wq = w_in[:, :hk].astype(BF16)
    wk = w_in[:, hk:2 * hk].astype(BF16)
    wv = w_in[:, 2 * hk:2 * hk + hv].astype(BF16)
    wg = w_in[:, 2 * hk + hv:2 * hk + 2 * hv].astype(BF16)
    wa = jnp.pad(w_in[:, 2 * hk + 2 * hv:], ((0, 0), (0, LANES - GLA_LOWRANK))).astype(BF16)
    wa2 = jnp.pad(w_a2, ((0, LANES - GLA_LOWRANK), (0, 0))).astype(BF16)
    return (wq, wk, wv, wg, wa, wa2, b_a.reshape(1, hk), norm_g.reshape(1, hv),
            w_out.astype(BF16), ln_g.reshape(1, D_MODEL), ln_b.reshape(1, D_MODEL),
            w_gate.astype(BF16), w_proj.astype(BF16))


def gla_layer_prompt(x, p_all, layer, weights, *, block=512, sub_block=256, chunk=GLA_CHUNK):
    bsz, t, _ = x.shape
    assert t % block == 0 and block % sub_block == 0 and sub_block % chunk == 0
    in_specs = [pl.BlockSpec((1, block, D_MODEL), lambda b, i: (b, i, 0)),
                pl.BlockSpec((None, 1, block, PLE_DIM), lambda b, i: (layer, b, i, 0))]
    in_specs += [_const_spec(w.shape) for w in weights]
    return pl.pallas_call(
        functools.partial(_gla_prompt_kernel, chunk=chunk),
        grid=(bsz, t // block),
        in_specs=in_specs,
        out_specs=[pl.BlockSpec((1, block, D_MODEL), lambda b, i: (b, i, 0)),
                   pl.BlockSpec((1, GLA_HEADS, GLA_DK, GLA_DV), lambda b, i: (b, 0, 0, 0))],
        out_shape=[jax.ShapeDtypeStruct((bsz, t, D_MODEL), F32),
                   jax.ShapeDtypeStruct((bsz, GLA_HEADS, GLA_DK, GLA_DV), F32)],
        scratch_shapes=[pltpu.VMEM((GLA_HEADS, GLA_DV, GLA_DK), F32),
                        pltpu.VMEM((block // sub_block, sub_block, GLA_HV), F32)],
        compiler_params=pltpu.CompilerParams(dimension_semantics=("arbitrary", "arbitrary"),
                                             vmem_limit_bytes=VMEM_LIMIT),
        name="gla_prompt",
    )(x, p_all, *weights)


def _gla_sample_kernel(x_ref, p_ref, s0_ref, wq_ref, wk_ref, wv_ref, wg_ref, wa_ref, wa2_ref, ba_ref, ng_ref,
                       wout_ref, lng_ref, lnb_ref, wgate_ref, wproj_ref,
                       y_ref, st_ref, o_scr, *, seq):
    rows = x_ref.shape[0]
    n_seq = rows // seq
    half = seq // 2
    x = x_ref[...]
    xb = x.astype(BF16)
    q, k, v, g, log_a = _gla_project(xb, wq_ref[...], wk_ref[...], wv_ref[...], wg_ref[...],
                                     wa_ref[...], wa2_ref[...], ba_ref[...])
    rel, _ = _chunk_decay_matrices(rows, seq)
    la_hi, la_mid = _split_bf16(log_a)
    d = _dot(rel, la_hi) + _dot(rel, la_mid)
    s_i = lax.broadcasted_iota(jnp.int32, (n_seq, rows), 0)
    t_i = lax.broadcasted_iota(jnp.int32, (n_seq, rows), 1)
    in_seq = (t_i // seq) == s_i
    sel_first = jnp.where(in_seq & ((t_i % seq) < half), 1.0, 0.0).astype(BF16)
    sel_second = jnp.where(in_seq & ((t_i % seq) >= half), 1.0, 0.0).astype(BF16)
    t_c = lax.broadcasted_iota(jnp.int32, (rows, n_seq), 0)
    s_c = lax.broadcasted_iota(jnp.int32, (rows, n_seq), 1)
    sel_tot_t = jnp.where((t_c // seq) == s_c, 1.0, 0.0).astype(BF16)
    e_first = jnp.exp(_dot(sel_first, la_hi) + _dot(sel_first, la_mid))
    e_second = jnp.exp(_dot(sel_second, la_hi) + _dot(sel_second, la_mid))
    e_tot_col = jnp.exp(_dot_ta(la_hi, sel_tot_t) + _dot_ta(la_mid, sel_tot_t))
    q_dec = q * jnp.exp(d)
    k_inv = k * jnp.exp(-d)

    ci = lax.broadcasted_iota(jnp.int32, (seq, seq), 0)
    cj = lax.broadcasted_iota(jnp.int32, (seq, seq), 1)
    causal = ci >= cj
    for c in range(n_seq):
        r0 = c * seq
        for h in range(GLA_HEADS):
            ks = slice(h * GLA_DK, (h + 1) * GLA_DK)
            vs = slice(h * GLA_DV, (h + 1) * GLA_DV)
            qd = q_dec[r0:r0 + seq, ks]
            ki = k_inv[r0:r0 + seq, ks]
            vh = v[r0:r0 + seq, vs].astype(BF16)
            s_old = s0_ref[c, h]
            att = jnp.where(causal, _dot_tb(qd.astype(BF16), ki.astype(BF16)), 0.0).astype(BF16)
            q_mid = (qd * e_first[c:c + 1, ks]).astype(BF16)
            o_scr[r0:r0 + seq, vs] = _dot(att, vh) + _dot(q_mid, s_old.astype(BF16))
            k_end = (ki * e_second[c:c + 1, ks]).astype(BF16)
            st_ref[c, h] = s_old * e_tot_col[h * GLA_DK:(h + 1) * GLA_DK, c:c + 1] + _dot_ta(k_end, vh)

    on = _head_rmsnorm_gate(o_scr[...], g, ng_ref[...], GLA_HEADS, GLA_DV, 1.0)
    f = _dot(on.astype(BF16), wout_ref[...])
    y_ref[...] = _residual_update(x, f, p_ref[...], lng_ref[...], lnb_ref[...], wgate_ref[...], wproj_ref[...])


def gla_layer_sample(x, p_all, layer, s0, weights, *, group=16):
    n, seq, _ = x.shape
    assert n % group == 0 and seq % 2 == 0
    rows = group * seq
    in_specs = [pl.BlockSpec((rows, D_MODEL), lambda i: (i, 0)),
                pl.BlockSpec((None, rows, PLE_DIM), lambda i: (layer, i, 0)),
                pl.BlockSpec((group, GLA_HEADS, GLA_DK, GLA_DV), lambda i: (i, 0, 0, 0))]
    in_specs += [_resident_spec(w.shape) for w in weights]
    y, st = pl.pallas_call(
        functools.partial(_gla_sample_kernel, seq=seq),
        grid=(n // group,),
        in_specs=in_specs,
        out_specs=[pl.BlockSpec((rows, D_MODEL), lambda i: (i, 0)),
                   pl.BlockSpec((group, GLA_HEADS, GLA_DK, GLA_DV), lambda i: (i, 0, 0, 0))],
        out_shape=[jax.ShapeDtypeStruct((n * seq, D_MODEL), F32),
                   jax.ShapeDtypeStruct((n, GLA_HEADS, GLA_DK, GLA_DV), F32)],
        scratch_shapes=[pltpu.VMEM((rows, GLA_HV), F32)],
        compiler_params=pltpu.CompilerParams(dimension_semantics=("arbitrary",),
                                             vmem_limit_bytes=VMEM_LIMIT),
        name="gla_sample",
    )(x.reshape(n * seq, D_MODEL), p_all.reshape(-1, n * seq, PLE_DIM), s0, *weights)
    return y.reshape(n, seq, D_MODEL), st


def _resident_spec(shape):
    nd = len(shape)
    return pl.BlockSpec(shape, lambda *_: (0,) * nd, pipeline_mode=pl.Buffered(1))


def _t5_bias(rel_bias, dist):
    max_exact = N_BUCKETS // 2
    n = jnp.maximum(dist, 0)[None]
    nf = jnp.maximum(n, 1).astype(F32)
    steps = jnp.log(nf / max_exact) / math.log(MAX_DISTANCE / max_exact) * (N_BUCKETS - max_exact)
    table = rel_bias.astype(F32)
    per_head = (DIFF_HEADS,) + (1,) * dist.ndim
    out = jnp.broadcast_to(table[N_BUCKETS - 1].reshape(per_head), (DIFF_HEADS,) + dist.shape)
    for b in range(N_BUCKETS - 2, max_exact - 1, -1):
        out = jnp.where(steps < (b + 1 - max_exact), table[b].reshape(per_head), out)
    for b in range(max_exact):
        out = jnp.where(n == b, table[b].reshape(per_head), out)
    return out


def _store_token_head_rows(ref, lead, x):
    tokens = x.shape[0]
    for h in range(DIFF_HEADS):
        ref[(*lead, pl.ds(h, tokens, stride=DIFF_HEADS), slice(None))] = x[:, h * DIFF_VD:(h + 1) * DIFF_VD]


def _load_token_head_rows(ref, lead, tokens):
    return jnp.concatenate(
        [ref[(*lead, pl.ds(h, tokens, stride=DIFF_HEADS), slice(None))] for h in range(DIFF_HEADS)], axis=1)


def _diff_lambda(lam_ref, lam_init):
    lv = lam_ref[...]
    a = jnp.sum(lv[0:1] * lv[1:2], axis=-1, keepdims=True)
    b = jnp.sum(lv[2:3] * lv[3:4], axis=-1, keepdims=True)
    return jnp.exp(a) - jnp.exp(b) + lam_init


def _diff_prompt_kernel(x_ref, p_ref, wq_ref, wk_ref, wv_ref, wg_ref, lam_ref, bias_ref, ng_ref,
                        wout_ref, lng_ref, lnb_ref, wgate_ref, wproj_ref,
                        y_ref, kout_ref, vout_ref,
                        k_scr, vt_scr, q_scr, o_scr, acc_scr, m_scr, alpha_scr, *bufs,
                        lam_init, heads_per_iter):
    i = pl.program_id(1)
    tq = x_ref.shape[1]
    x = x_ref[0]
    xb = x.astype(BF16)
    q = _dot(xb, wq_ref[...]) * (DIFF_DH ** -0.5 * LOG2_E)
    k = _dot(xb, wk_ref[...])
    v = _dot(xb, wv_ref[...])
    _store_token_head_rows(kout_ref, (0,), k)
    _store_token_head_rows(vout_ref, (0,), v)
    first_half = lax.broadcasted_iota(jnp.int32, (DIFF_VD, tq), 0) < DIFF_DH
    row0 = pl.multiple_of(i * tq, tq)
    for h in range(DIFF_HEADS):
        hs = slice(h * DIFF_VD, (h + 1) * DIFF_VD)
        qh_t = q[:, hs].T
        q_scr[h, :, 0:tq] = jnp.where(first_half, qh_t, 0.0).astype(BF16)
        q_scr[h, :, tq:2 * tq] = jnp.where(first_half, 0.0, qh_t).astype(BF16)
        k_scr[h, pl.ds(row0, tq), :] = k[:, hs].astype(BF16)
        vt_scr[h, i, 0:DIFF_VD, :] = v[:, hs].T.astype(BF16)
        vt_scr[h, i, DIFF_VD:, :] = jnp.ones((SUM_ROWS, tq), BF16)
    lam = _diff_lambda(lam_ref, lam_init)

    s_buf = [bufs[2 * u:2 * u + 2] for u in range(heads_per_iter)]
    p_buf = [bufs[2 * heads_per_iter + 2 * u:2 * heads_per_iter + 2 * u + 2] for u in range(heads_per_iter)]

    def block_of(t):
        near = jnp.maximum(i - t, 0)
        if isinstance(t, int):
            return near if t < 2 else jnp.minimum(t - 2, i)
        return jnp.where(t < 2, near, jnp.minimum(t - 2, i))

    def group_body(grp, carry):
        heads = [grp * heads_per_iter + u for u in range(heads_per_iter)]
        m_scr[...] = jnp.full(m_scr.shape, MASK_VALUE, F32)
        acc_scr[...] = jnp.zeros(acc_scr.shape, F32)

        def scores(t, slot):
            rows = pl.ds(pl.multiple_of(block_of(t) * tq, tq), tq)
            for u, h in enumerate(heads):
                s_buf[u][slot][...] = _dot(k_scr[h, rows, :], q_scr[h])

        def softmax(t, slot, bias_slot):
            valid = t <= i
            for u, h in enumerate(heads):
                s = s_buf[u][slot][...]
                if bias_slot is not None:
                    tile = bias_ref[h, bias_slot]
                    s = s + jnp.concatenate([tile, tile], axis=1)
                m_old = m_scr[u]
                m_new = jnp.where(valid, jnp.maximum(m_old, jnp.max(s, axis=0, keepdims=True)), m_old)
                alpha = jnp.exp2(m_old - m_new)
                pr = jnp.exp2(s - jnp.where(valid, m_new, -MASK_VALUE))
                m_scr[u] = m_new
                alpha_scr[u] = alpha
                p_buf[u][slot][...] = pr.astype(BF16)

        def weighted_values(t, slot):
            j = block_of(t)
            for u, h in enumerate(heads):
                acc_scr[u] = acc_scr[u] * alpha_scr[u] + _dot(vt_scr[h, j], p_buf[u][slot][...])

        def step(t, slot, bias_slot):
            weighted_values(t - 1, 1 - slot)
            scores(t + 1, 1 - slot)
            softmax(t, slot, bias_slot)

        scores(0, 0)
        softmax(0, 0, 0)
        scores(1, 1)
        step(1, 1, 1)

        def two_far_steps(r, c):
            step(2 * r, 0, None)
            step(2 * r + 1, 1, None)
            return c

        n_pairs = (i + 2) // 2
        lax.fori_loop(1, n_pairs, two_far_steps, 0)
        weighted_values(2 * n_pairs - 1, 1)

        for u, h in enumerate(heads):
            inv_l = 1.0 / acc_scr[u, DIFF_VD:DIFF_VD + 1, :]
            acc = acc_scr[u, 0:DIFF_VD, :]
            o_t = acc[:, :tq] * inv_l[:, :tq] - lam * (acc[:, tq:] * inv_l[:, tq:])
            o_scr[h] = o_t.T
        return carry

    lax.fori_loop(0, DIFF_HEADS // heads_per_iter, group_body, 0)
    o = jnp.concatenate([o_scr[h] for h in range(DIFF_HEADS)], axis=-1)
    gate = _dot(xb, wg_ref[...])
    on = _head_rmsnorm_gate(o, gate, ng_ref[...], DIFF_HEADS, DIFF_VD, 1.0 - lam_init)
    f = _dot(on.astype(BF16), wout_ref[...])
    y_ref[0] = _residual_update(x, f, p_ref[0], lng_ref[...], lnb_ref[...], wgate_ref[...], wproj_ref[...])


def _diff_weights(w_in, lam_q1, lam_k1, lam_q2, lam_k2, norm_g, w_out, ln_g, ln_b, w_gate, w_proj):
    w = DIFF_WIDTH
    return dict(
        wq=w_in[:, :w].astype(BF16), wk=w_in[:, w:2 * w].astype(BF16),
        wv=w_in[:, 2 * w:3 * w].astype(BF16), wg=w_in[:, 3 * w:].astype(BF16),
        lam=jnp.stack([lam_q1, lam_k1, lam_q2, lam_k2]).astype(F32),
        ng=norm_g.reshape(1, w), wout=w_out.astype(BF16),
        lng=ln_g.reshape(1, D_MODEL), lnb=ln_b.reshape(1, D_MODEL),
        wgate=w_gate.astype(BF16), wproj=w_proj.astype(BF16))


def diff_layer_prompt(x, p_all, layer, dw, rel_bias, lam_init, *, block=256, heads_per_iter=2):
    bsz, t, _ = x.shape
    assert t % block == 0 and block >= MAX_DISTANCE and DIFF_HEADS % heads_per_iter == 0
    kk = jnp.arange(block, dtype=jnp.int32)[:, None]
    qq = jnp.arange(block, dtype=jnp.int32)[None, :]
    dist = jnp.stack([qq - kk, block + qq - kk])
    far = rel_bias.astype(F32)[N_BUCKETS - 1].reshape(DIFF_HEADS, 1, 1, 1)
    bias = jnp.where(dist >= 0, (_t5_bias(rel_bias, dist) - far) * LOG2_E, MASK_VALUE)
    tok = lambda width: pl.BlockSpec((1, block, width), lambda b, i: (b, i, 0))
    kv_spec = pl.BlockSpec((1, block * DIFF_HEADS, DIFF_VD), lambda b, i: (b, i, 0))
    consts = [dw["wq"], dw["wk"], dw["wv"], dw["wg"], dw["lam"], bias, dw["ng"], dw["wout"],
              dw["lng"], dw["lnb"], dw["wgate"], dw["wproj"]]
    return pl.pallas_call(
        functools.partial(_diff_prompt_kernel, lam_init=lam_init, heads_per_iter=heads_per_iter),
        grid=(bsz, t // block),
        in_specs=[tok(D_MODEL), pl.BlockSpec((None, 1, block, PLE_DIM), lambda b, i: (layer, b, i, 0))]
                 + [_resident_spec(c.shape) for c in consts],
        out_specs=[tok(D_MODEL), kv_spec, kv_spec],
        out_shape=[jax.ShapeDtypeStruct((bsz, t, D_MODEL), F32),
                   jax.ShapeDtypeStruct((bsz, t * DIFF_HEADS, DIFF_VD), F32),
                   jax.ShapeDtypeStruct((bsz, t * DIFF_HEADS, DIFF_VD), F32)],
        scratch_shapes=[pltpu.VMEM((DIFF_HEADS, t, DIFF_VD), BF16),
                        pltpu.VMEM((DIFF_HEADS, t // block, DIFF_VD + SUM_ROWS, block), BF16),
                        pltpu.VMEM((DIFF_HEADS, DIFF_VD, 2 * block), BF16),
                        pltpu.VMEM((DIFF_HEADS, block, DIFF_VD), F32),
                        pltpu.VMEM((heads_per_iter, DIFF_VD + SUM_ROWS, 2 * block), F32),
                        pltpu.VMEM((heads_per_iter, 1, 2 * block), F32),
                        pltpu.VMEM((heads_per_iter, 1, 2 * block), F32),
                        *[pltpu.VMEM((block, 2 * block), F32)] * (2 * heads_per_iter),
                        *[pltpu.VMEM((block, 2 * block), BF16)] * (2 * heads_per_iter)],
        compiler_params=pltpu.CompilerParams(dimension_semantics=("arbitrary", "arbitrary"),
                                             vmem_limit_bytes=VMEM_LIMIT),
        name="diff_prompt",
    )(x, p_all, *consts)


def _diff_sample_project_kernel(x_ref, wq_ref, wk_ref, wv_ref, qt_ref, k_ref, v_ref):
    xb = x_ref[...].astype(BF16)
    qt_ref[...] = (_dot(xb, wq_ref[...]) * (DIFF_DH ** -0.5)).T.astype(BF16)
    _store_token_head_rows(k_ref, (), _dot(xb, wk_ref[...]))
    _store_token_head_rows(v_ref, (), _dot(xb, wv_ref[...]))


def _row_to_col(row, n):
    eye = lax.broadcasted_iota(jnp.int32, (n, n), 0) == lax.broadcasted_iota(jnp.int32, (n, n), 1)
    return jnp.sum(jnp.where(eye, row, 0.0), axis=1, keepdims=True)


def _paged_attn_kernel(pt_ref, qt_ref, kn_ref, vn_ref, lam_ref, bias_ref, bnew_ref, *rest, pages, seq, lam_init):
    k_pages = rest[:pages]
    v_pages = rest[pages:2 * pages]
    o_ref = rest[2 * pages]
    wq_scr, acc_scr, m_scr, l_scr = rest[2 * pages + 1:]
    b = pl.program_id(0)
    g = pl.program_id(1)
    ncol = 2 * DIFF_HEADS * seq

    @pl.when(g == 0)
    def _():
        local = (b % (LANES // seq)) * seq
        src = lax.broadcasted_iota(jnp.int32, (LANES, ncol), 0)
        col = lax.broadcasted_iota(jnp.int32, (LANES, ncol), 1)
        pick = jnp.where(src == local + col % seq, 1.0, 0.0).astype(BF16)
        rep = _dot(qt_ref[...], pick)
        r = lax.broadcasted_iota(jnp.int32, (DIFF_WIDTH, ncol), 0)
        c = lax.broadcasted_iota(jnp.int32, (DIFF_WIDTH, ncol), 1)
        own = ((r // DIFF_VD) == ((c % (DIFF_HEADS * seq)) // seq)) & \
              (((r % DIFF_VD) // DIFF_DH) == (c // (DIFF_HEADS * seq)))
        wq_scr[...] = jnp.where(own, rep, 0.0).astype(BF16)
        m_scr[...] = jnp.full(m_scr.shape, MASK_VALUE, F32)
        l_scr[...] = jnp.zeros(l_scr.shape, F32)
        acc_scr[...] = jnp.zeros(acc_scr.shape, F32)

    def flash_step(scores, values):
        m_old = m_scr[...]
        m_new = m_old
        for s in scores:
            m_new = jnp.maximum(m_new, jnp.max(s, axis=0, keepdims=True))
        alpha = jnp.exp(m_old - m_new)
        l_new = alpha * l_scr[...]
        probs = []
        for s in scores:
            pr = jnp.exp(s - m_new)
            l_new = l_new + jnp.sum(pr, axis=0, keepdims=True)
            probs.append(pr.astype(BF16))
        pv = None
        for a in range(0, len(probs), 2):
            t = _dot_ta(jnp.concatenate(probs[a:a + 2], axis=0),
                        jnp.concatenate(values[a:a + 2], axis=0))
            pv = t if pv is None else pv + t
        acc_scr[...] = acc_scr[...] * _row_to_col(alpha, ncol) + pv
        l_scr[...] = l_new
        m_scr[...] = m_new

    def load_page(ref):
        return _load_token_head_rows(ref, (0,), PAGE_SIZE).astype(BF16)

    wq = wq_scr[...]
    scores, values = [], []
    for i in range(pages):
        s = _dot(load_page(k_pages[i]), wq)
        scores.append(s + bias_ref[0] if i == pages - 1 else s)
        values.append(load_page(v_pages[i]))
    flash_step(scores, values)

    @pl.when(g == pl.num_programs(1) - 1)
    def _():
        pad = jnp.zeros((16 - seq, DIFF_WIDTH), F32)
        kn = jnp.concatenate([_load_token_head_rows(kn_ref, (0,), seq), pad], axis=0).astype(BF16)
        vn = jnp.concatenate([_load_token_head_rows(vn_ref, (0,), seq), pad], axis=0).astype(BF16)
        flash_step([_dot(kn, wq) + bnew_ref[...]], [vn])
        lam = _diff_lambda(lam_ref, lam_init)
        inv_l = _row_to_col(1.0 / l_scr[...], ncol)
        half_rows = DIFF_HEADS * seq
        for h in range(DIFF_HEADS):
            cs = slice(h * DIFF_VD, (h + 1) * DIFF_VD)
            r1 = slice(h * seq, (h + 1) * seq)
            r2 = slice(half_rows + h * seq, half_rows + (h + 1) * seq)
            o_ref[0, :, cs] = acc_scr[r1, cs] * inv_l[r1] - lam * (acc_scr[r2, cs] * inv_l[r2])


def _diff_sample_out_kernel(x_ref, o_ref, p_ref, wg_ref, ng_ref, wout_ref, lng_ref, lnb_ref, wgate_ref, wproj_ref,
                            y_ref, *, lam_init):
    x = x_ref[...]
    gate = _dot(x.astype(BF16), wg_ref[...])
    on = _head_rmsnorm_gate(o_ref[...], gate, ng_ref[...], DIFF_HEADS, DIFF_VD, 1.0 - lam_init)
    f = _dot(on.astype(BF16), wout_ref[...])
    y_ref[...] = _residual_update(x, f, p_ref[...], lng_ref[...], lnb_ref[...], wgate_ref[...], wproj_ref[...])


def diff_layer_sample(x, p_all, layer, cache_k, cache_v, page_table, dw, rel_bias, lam_init, *, pages=16, block=256):
    n, seq, _ = x.shape
    n_pages = page_table.shape[1]
    rows = n * seq
    assert n_pages % pages == 0 and rows % block == 0 and LANES % seq == 0 and seq <= 16
    assert PAGE_SIZE >= MAX_DISTANCE
    x2 = x.reshape(rows, D_MODEL)
    row_spec = lambda width: pl.BlockSpec((block, width), lambda i: (i, 0))
    qt, kn, vn = pl.pallas_call(
        _diff_sample_project_kernel,
        grid=(rows // block,),
        in_specs=[row_spec(D_MODEL)] + [_const_spec((D_MODEL, DIFF_WIDTH))] * 3,
        out_specs=[pl.BlockSpec((DIFF_WIDTH, block), lambda i: (0, i)),
                   pl.BlockSpec((block * DIFF_HEADS, DIFF_VD), lambda i: (i, 0)),
                   pl.BlockSpec((block * DIFF_HEADS, DIFF_VD), lambda i: (i, 0))],
        out_shape=[jax.ShapeDtypeStruct((DIFF_WIDTH, rows), BF16),
                   jax.ShapeDtypeStruct((rows * DIFF_HEADS, DIFF_VD), F32),
                   jax.ShapeDtypeStruct((rows * DIFF_HEADS, DIFF_VD), F32)],
        compiler_params=pltpu.CompilerParams(dimension_semantics=("arbitrary",), vmem_limit_bytes=VMEM_LIMIT),
        name="diff_sample_project",
    )(x2, dw["wq"], dw["wk"], dw["wv"])

    ncol = 2 * DIFF_HEADS * seq
    col = jnp.arange(ncol, dtype=jnp.int32)[None, :]
    col_h = (col % (DIFF_HEADS * seq)) // seq
    col_t = col % seq
    kk = jnp.arange(PAGE_SIZE, dtype=jnp.int32)[:, None]
    tk = jnp.arange(16, dtype=jnp.int32)[:, None]
    dist_last = PAGE_SIZE + col_t - kk
    dist_new = col_t - tk
    far = rel_bias.astype(F32)[N_BUCKETS - 1].reshape(DIFF_HEADS, 1, 1)

    def own_head(per_head):
        return sum(jnp.where(col_h == h, per_head[h], 0.0) for h in range(DIFF_HEADS))

    bias_last = own_head(_t5_bias(rel_bias, dist_last) - far)
    bias_pages = jnp.stack([jnp.zeros_like(bias_last), bias_last])
    bias_new = jnp.where((tk < seq) & (dist_new >= 0), own_head(_t5_bias(rel_bias, dist_new) - far), MASK_VALUE)

    n_groups = n_pages // pages
    seq_per_blk = LANES // seq

    def page_spec(i):
        return pl.BlockSpec((1, PAGE_SIZE * DIFF_HEADS, DIFF_VD),
                            lambda b, g, pt: (pt[b * n_pages + g * pages + i], 0, 0))

    tok_spec = pl.BlockSpec((1, seq, DIFF_WIDTH), lambda b, g, pt: (b, 0, 0))
    new_spec = pl.BlockSpec((1, seq * DIFF_HEADS, DIFF_VD), lambda b, g, pt: (b, 0, 0))
    grid_spec = pltpu.PrefetchScalarGridSpec(
        num_scalar_prefetch=1,
        grid=(n, n_groups),
        in_specs=[pl.BlockSpec((DIFF_WIDTH, LANES), lambda b, g, pt: (0, b // seq_per_blk)),
                  new_spec, new_spec,
                  pl.BlockSpec((4, DIFF_DH), lambda b, g, pt: (0, 0)),
                  pl.BlockSpec((1, PAGE_SIZE, ncol), lambda b, g, pt: ((g + 1) // n_groups, 0, 0)),
                  pl.BlockSpec((16, ncol), lambda b, g, pt: (0, 0))]
                 + [page_spec(i) for i in range(pages)] + [page_spec(i) for i in range(pages)],
        out_specs=tok_spec,
        scratch_shapes=[pltpu.VMEM((DIFF_WIDTH, ncol), BF16),
                        pltpu.VMEM((ncol, DIFF_WIDTH), F32),
                        pltpu.VMEM((1, ncol), F32),
                        pltpu.VMEM((1, ncol), F32)])
    o = pl.pallas_call(
        functools.partial(_paged_attn_kernel, pages=pages, seq=seq, lam_init=lam_init),
        grid_spec=grid_spec,
        out_shape=jax.ShapeDtypeStruct((n, seq, DIFF_WIDTH), F32),
        compiler_params=pltpu.CompilerParams(dimension_semantics=("arbitrary", "arbitrary"),
                                             vmem_limit_bytes=VMEM_LIMIT),
        name="diff_sample_attn",
    )(page_table.reshape(-1), qt, kn.reshape(n, seq * DIFF_HEADS, DIFF_VD), vn.reshape(n, seq * DIFF_HEADS, DIFF_VD),
      dw["lam"], bias_pages, bias_new, *([cache_k] * pages), *([cache_v] * pages))

    consts = [dw["wg"], dw["ng"], dw["wout"], dw["lng"], dw["lnb"], dw["wgate"], dw["wproj"]]
    y = pl.pallas_call(
        functools.partial(_diff_sample_out_kernel, lam_init=lam_init),
        grid=(rows // block,),
        in_specs=[row_spec(D_MODEL), row_spec(DIFF_WIDTH),
                  pl.BlockSpec((None, block, PLE_DIM), lambda i: (layer, i, 0))] + [_const_spec(c.shape) for c in consts],
        out_specs=row_spec(D_MODEL),
        out_shape=jax.ShapeDtypeStruct((rows, D_MODEL), F32),
        compiler_params=pltpu.CompilerParams(dimension_semantics=("arbitrary",), vmem_limit_bytes=VMEM_LIMIT),
        name="diff_sample_out",
    )(x2, o.reshape(rows, DIFF_WIDTH), p_all.reshape(-1, rows, PLE_DIM), *consts)
    return y.reshape(n, seq, D_MODEL), kn, vn


def kernel(x_prompt, x_sample, state_gla, cache_k, cache_v, page_table, p_prompt, p_sample, rel_bias,
           gla_w_in, gla_w_a2, gla_b_a, gla_norm_g, gla_w_out,
           diff_w_in, diff_lam_q1, diff_lam_k1, diff_lam_q2, diff_lam_k2, diff_norm_g, diff_w_out,
           ln_g, ln_b, ple_w_proj, ple_w_gate):
    w0 = _gla_weights(gla_w_in[0], gla_w_a2[0], gla_b_a[0], gla_norm_g[0], gla_w_out[0],
                      ln_g[0], ln_b[0], ple_w_gate[0], ple_w_proj[0])
    xp1, sp = gla_layer_prompt(x_prompt, p_prompt, 0, w0)
    xs1, ss = gla_layer_sample(x_sample, p_sample, 0, state_gla[0], w0)
    lam_init = 0.8 - 0.6 * math.exp(-0.3 * 1)
    dw = _diff_weights(diff_w_in[0], diff_lam_q1[0], diff_lam_k1[0], diff_lam_q2[0], diff_lam_k2[0],
                       diff_norm_g[0], diff_w_out[0], ln_g[1], ln_b[1], ple_w_gate[1], ple_w_proj[1])
    yp, kp, vp = diff_layer_prompt(xp1, p_prompt, 1, dw, rel_bias, lam_init)
    pool = cache_k.shape[1]
    ys, ks, vs = diff_layer_sample(xs1, p_sample, 1,
                                   cache_k[0].reshape(pool, PAGE_SIZE * DIFF_HEADS, DIFF_VD),
                                   cache_v[0].reshape(pool, PAGE_SIZE * DIFF_HEADS, DIFF_VD),
                                   page_table, dw, rel_bias, lam_init)
    bsz, t, _ = x_prompt.shape
    n, seq, _ = x_sample.shape
    heads = (DIFF_HEADS, DIFF_VD)
    return (yp, ys, sp[None], ss[None],
            kp.reshape(1, bsz, t, *heads), vp.reshape(1, bsz, t, *heads),
            ks.reshape(1, n, seq, *heads), vs.reshape(1, n, seq, *heads))
```

```python
import functools
import math

import jax
import jax.numpy as jnp
from jax import lax
from jax.experimental import pallas as pl
from jax.experimental.pallas import tpu as pltpu

F32 = jnp.float32
BF16 = jnp.bfloat16

D_MODEL = 1024
DEPTH = 2
GLA_HEADS = 4
GLA_DK = 128
GLA_DV = 256
GLA_HK = GLA_HEADS * GLA_DK
GLA_HV = GLA_HEADS * GLA_DV
GLA_LOWRANK = 16
GLA_TAU = 16.0
DIFF_HEADS = 8
DIFF_DH = 64
DIFF_VD = 128
DIFF_WIDTH = DIFF_HEADS * DIFF_VD
N_BUCKETS = 32
MAX_DISTANCE = 128
PLE_DIM = 256
PAGE_SIZE = 128
ALPHA = (2 * DEPTH) ** 0.25
EPS = 1e-5

LANES = 128
GLA_CHUNK = 64
VMEM_LIMIT = 56 * 1024 * 1024
MASK_VALUE = -1e30
LOG2_E = math.log2(math.e)
SUM_ROWS = 16
PAGE_RING = 3


def _dot(a, b):
    return jnp.dot(a, b, preferred_element_type=F32)


def _dot_tb(a, b):
    return lax.dot_general(a, b, (((1,), (1,)), ((), ())), preferred_element_type=F32)


def _dot_ta(a, b):
    return lax.dot_general(a, b, (((0,), (0,)), ((), ())), preferred_element_type=F32)


def _split_bf16(x):
    hi = x.astype(BF16)
    mid = (x - hi.astype(F32)).astype(BF16)
    return hi, mid


def _log_sigmoid(z):
    return jnp.minimum(z, 0.0) - jnp.log(1.0 + jnp.exp(-jnp.abs(z)))


def _sigmoid(z):
    return 1.0 / (1.0 + jnp.exp(-z))


def _head_rmsnorm_gate(o, gate, norm_g, n_heads, head_dim, scale):
    parts = []
    for h in range(n_heads):
        oh = o[:, h * head_dim:(h + 1) * head_dim]
        ms = jnp.mean(oh * oh, axis=-1, keepdims=True)
        parts.append(oh * lax.rsqrt(ms + EPS))
    on = jnp.concatenate(parts, axis=-1) * norm_g
    if scale != 1.0:
        on = on * scale
    return on * (gate * _sigmoid(gate))


def _residual_update(x, f, p, ln_g, ln_b, w_gate, w_proj):
    hp = ALPHA * x + f
    mu = jnp.mean(hp, axis=-1, keepdims=True)
    hc = hp - mu
    var = jnp.mean(hc * hc, axis=-1, keepdims=True)
    h = hc * lax.rsqrt(var + EPS) * ln_g + ln_b
    gate = _sigmoid(_dot(h.astype(BF16), w_gate))
    return h + gate * _dot(p.astype(BF16), w_proj)


def _gla_project(xb, wq, wk, wv, wg, wa, wa2, ba):
    q = _dot(xb, wq) * (GLA_DK ** -0.5)
    k = _dot(xb, wk)
    v = _dot(xb, wv)
    g = _dot(xb, wg)
    a_lr = _dot(xb, wa)
    z = _dot(a_lr.astype(BF16), wa2) + ba
    log_a = _log_sigmoid(z) * (1.0 / GLA_TAU)
    return q, k, v, g, log_a


def _chunk_decay_matrices(rows, chunk):
    half = chunk // 2
    i = lax.broadcasted_iota(jnp.int32, (rows, rows), 0)
    j = lax.broadcasted_iota(jnp.int32, (rows, rows), 1)
    same = (i // chunk) == (j // chunk)
    jl = j % chunk
    il = i % chunk
    pos = same & (jl >= half) & (jl <= il)
    neg = same & (jl < half) & (jl > il)
    rel = jnp.where(pos, 1.0, jnp.where(neg, -1.0, 0.0)).astype(BF16)
    n_sel = max(8, 2 * rows // chunk)
    s = lax.broadcasted_iota(jnp.int32, (n_sel, rows), 0)
    t = lax.broadcasted_iota(jnp.int32, (n_sel, rows), 1)
    halves = jnp.where((t // half) == s, 1.0, 0.0).astype(BF16)
    return rel, halves


def _gla_prompt_kernel(x_ref, p_ref, wq_ref, wk_ref, wv_ref, wg_ref, wa_ref, wa2_ref, ba_ref, ng_ref,
                       wout_ref, lng_ref, lnb_ref, wgate_ref, wproj_ref,
                       y_ref, st_ref, s_scr, o_scr, *, chunk):
    blk = pl.program_id(1)
    rows = o_scr.shape[1]

    @pl.when(blk == 0)
    def _():
        s_scr[...] = jnp.zeros_like(s_scr)

    ci = lax.broadcasted_iota(jnp.int32, (chunk, chunk), 0)
    cj = lax.broadcasted_iota(jnp.int32, (chunk, chunk), 1)
    causal = ci >= cj
    rel, halves = _chunk_decay_matrices(rows, chunk)
    for sb in range(x_ref.shape[1] // rows):
        tok = slice(sb * rows, (sb + 1) * rows)
        x = x_ref[0, tok, :]
        xb = x.astype(BF16)
        q, k, v, g, log_a = _gla_project(xb, wq_ref[...], wk_ref[...], wv_ref[...], wg_ref[...],
                                         wa_ref[...], wa2_ref[...], ba_ref[...])
        la_hi, la_mid = _split_bf16(log_a)
        d = _dot(rel, la_hi) + _dot(rel, la_mid)
        hs = _dot(halves, la_hi) + _dot(halves, la_mid)
        ehs = jnp.exp(hs)
        q_dec = (q * jnp.exp(d)).astype(BF16)
        k_inv = (k * jnp.exp(-d)).astype(BF16)
        vb = v.astype(BF16)

        for c in range(rows // chunk):
            r0 = c * chunk
            for h in range(GLA_HEADS):
                ks = slice(h * GLA_DK, (h + 1) * GLA_DK)
                vs = slice(h * GLA_DV, (h + 1) * GLA_DV)
                qd = q_dec[r0:r0 + chunk, ks]
                ki = k_inv[r0:r0 + chunk, ks]
                vh = vb[r0:r0 + chunk, vs]
                e_first = ehs[2 * c:2 * c + 1, ks]
                e_second = ehs[2 * c + 1:2 * c + 2, ks]
                s_mid = s_scr[h] * e_first
                att = jnp.where(causal, _dot_tb(qd, ki), 0.0).astype(BF16)
                o_scr[sb, r0:r0 + chunk, vs] = _dot(att, vh) + _dot_tb(qd, s_mid.astype(BF16))
                s_scr[h] = (s_mid + _dot_ta(vh, ki)) * e_second

        on = _head_rmsnorm_gate(o_scr[sb], g, ng_ref[...], GLA_HEADS, GLA_DV, 1.0)
        f = _dot(on.astype(BF16), wout_ref[...])
        y_ref[0, tok, :] = _residual_update(x, f, p_ref[0, tok, :], lng_ref[...], lnb_ref[...],
                                            wgate_ref[...], wproj_ref[...])

    @pl.when(blk == pl.num_programs(1) - 1)
    def _():
        for h in range(GLA_HEADS):
            st_ref[0, h] = s_scr[h].T


def _const_spec(shape):
    nd = len(shape)
    return pl.BlockSpec(shape, lambda *_: (0,) * nd)


def _gla_weights(w_in, w_a2, b_a, norm_g, w_out, ln_g, ln_b, w_gate, w_proj):
    hk, hv = GLA_HK, GLA_HV
    wq = w_in[:, :hk].astype(BF16)
    wk = w_in[:, hk:2 * hk].astype(BF16)
    wv = w_in[:, 2 * hk:2 * hk + hv].astype(BF16)
    wg = w_in[:, 2 * hk + hv:2 * hk + 2 * hv].astype(BF16)
    wa = jnp.pad(w_in[:, 2 * hk + 2 * hv:], ((0, 0), (0, LANES - GLA_LOWRANK))).astype(BF16)
    wa2 = jnp.pad(w_a2, ((0, LANES - GLA_LOWRANK), (0, 0))).astype(BF16)
    return (wq, wk, wv, wg, wa, wa2, b_a.reshape(1, hk), norm_g.reshape(1, hv),
            w_out.astype(BF16), ln_g.reshape(1, D_MODEL), ln_b.reshape(1, D_MODEL),
            w_gate.astype(BF16), w_proj.astype(BF16))


def gla_layer_prompt(x, p_all, layer, weights, *, block=512, sub_block=256, chunk=GLA_CHUNK):
    bsz, t, _ = x.shape
    assert t % block == 0 and block % sub_block == 0 and sub_block % chunk == 0
    in_specs = [pl.BlockSpec((1, block, D_MODEL), lambda b, i: (b, i, 0)),
                pl.BlockSpec((None, 1, block, PLE_DIM), lambda b, i: (layer, b, i, 0))]
    in_specs += [_const_spec(w.shape) for w in weights]
    return pl.pallas_call(
        functools.partial(_gla_prompt_kernel, chunk=chunk),
        grid=(bsz, t // block),
        in_specs=in_specs,
        out_specs=[pl.BlockSpec((1, block, D_MODEL), lambda b, i: (b, i, 0)),
                   pl.BlockSpec((1, GLA_HEADS, GLA_DK, GLA_DV), lambda b, i: (b, 0, 0, 0))],
        out_shape=[jax.ShapeDtypeStruct((bsz, t, D_MODEL), F32),
                   jax.ShapeDtypeStruct((bsz, GLA_HEADS, GLA_DK, GLA_DV), F32)],
        scratch_shapes=[pltpu.VMEM((GLA_HEADS, GLA_DV, GLA_DK), F32),
                        pltpu.VMEM((block // sub_block, sub_block, GLA_HV), F32)],
        compiler_params=pltpu.CompilerParams(dimension_semantics=("arbitrary", "arbitrary"),
                                             vmem_limit_bytes=VMEM_LIMIT),
        name="gla_prompt",
    )(x, p_all, *weights)


def _gla_sample_kernel(x_ref, p_ref, s0_ref, wq_ref, wk_ref, wv_ref, wg_ref, wa_ref, wa2_ref, ba_ref, ng_ref,
                       wout_ref, lng_ref, lnb_ref, wgate_ref, wproj_ref,
                       y_ref, st_ref, o_scr, *, seq):
    rows = x_ref.shape[0]
    n_seq = rows // seq
    half = seq // 2
    x = x_ref[...]
    xb = x.astype(BF16)
    q, k, v, g, log_a = _gla_project(xb, wq_ref[...], wk_ref[...], wv_ref[...], wg_ref[...],
                                     wa_ref[...], wa2_ref[...], ba_ref[...])
    rel, _ = _chunk_decay_matrices(rows, seq)
    la_hi, la_mid = _split_bf16(log_a)
    d = _dot(rel, la_hi) + _dot(rel, la_mid)
    s_i = lax.broadcasted_iota(jnp.int32, (n_seq, rows), 0)
    t_i = lax.broadcasted_iota(jnp.int32, (n_seq, rows), 1)
    in_seq = (t_i // seq) == s_i
    sel_first = jnp.where(in_seq & ((t_i % seq) < half), 1.0, 0.0).astype(BF16)
    sel_second = jnp.where(in_seq & ((t_i % seq) >= half), 1.0, 0.0).astype(BF16)
    t_c = lax.broadcasted_iota(jnp.int32, (rows, n_seq), 0)
    s_c = lax.broadcasted_iota(jnp.int32, (rows, n_seq), 1)
    sel_tot_t = jnp.where((t_c // seq) == s_c, 1.0, 0.0).astype(BF16)
    e_first = jnp.exp(_dot(sel_first, la_hi) + _dot(sel_first, la_mid))
    e_second = jnp.exp(_dot(sel_second, la_hi) + _dot(sel_second, la_mid))
    e_tot_col = jnp.exp(_dot_ta(la_hi, sel_tot_t) + _dot_ta(la_mid, sel_tot_t))
    q_dec = q * jnp.exp(d)
    k_inv = k * jnp.exp(-d)

    ci = lax.broadcasted_iota(jnp.int32, (seq, seq), 0)
    cj = lax.broadcasted_iota(jnp.int32, (seq, seq), 1)
    causal = ci >= cj
    for c in range(n_seq):
        r0 = c * seq
        for h in range(GLA_HEADS):
            ks = slice(h * GLA_DK, (h + 1) * GLA_DK)
            vs = slice(h * GLA_DV, (h + 1) * GLA_DV)
            qd = q_dec[r0:r0 + seq, ks]
            ki = k_inv[r0:r0 + seq, ks]
            vh = v[r0:r0 + seq, vs].astype(BF16)
            s_old = s0_ref[c, h]
            att = jnp.where(causal, _dot_tb(qd.astype(BF16), ki.astype(BF16)), 0.0).astype(BF16)
            q_mid = (qd * e_first[c:c + 1, ks]).astype(BF16)
            o_scr[r0:r0 + seq, vs] = _dot(att, vh) + _dot(q_mid, s_old.astype(BF16))
            k_end = (ki * e_second[c:c + 1, ks]).astype(BF16)
            st_ref[c, h] = s_old * e_tot_col[h * GLA_DK:(h + 1) * GLA_DK, c:c + 1] + _dot_ta(k_end, vh)

    on = _head_rmsnorm_gate(o_scr[...], g, ng_ref[...], GLA_HEADS, GLA_DV, 1.0)
    f = _dot(on.astype(BF16), wout_ref[...])
    y_ref[...] = _residual_update(x, f, p_ref[...], lng_ref[...], lnb_ref[...], wgate_ref[...], wproj_ref[...])


def gla_layer_sample(x, p_all, layer, s0, weights, *, group=16):
    n, seq, _ = x.shape
    assert n % group == 0 and seq % 2 == 0
    rows = group * seq
    in_specs = [pl.BlockSpec((rows, D_MODEL), lambda i: (i, 0)),
                pl.BlockSpec((None, rows, PLE_DIM), lambda i: (layer, i, 0)),
                pl.BlockSpec((group, GLA_HEADS, GLA_DK, GLA_DV), lambda i: (i, 0, 0, 0))]
    in_specs += [_resident_spec(w.shape) for w in weights]
    y, st = pl.pallas_call(
        functools.partial(_gla_sample_kernel, seq=seq),
        grid=(n // group,),
        in_specs=in_specs,
        out_specs=[pl.BlockSpec((rows, D_MODEL), lambda i: (i, 0)),
                   pl.BlockSpec((group, GLA_HEADS, GLA_DK, GLA_DV), lambda i: (i, 0, 0, 0))],
        out_shape=[jax.ShapeDtypeStruct((n * seq, D_MODEL), F32),
                   jax.ShapeDtypeStruct((n, GLA_HEADS, GLA_DK, GLA_DV), F32)],
        scratch_shapes=[pltpu.VMEM((rows, GLA_HV), F32)],
        compiler_params=pltpu.CompilerParams(dimension_semantics=("arbitrary",),
                                             vmem_limit_bytes=VMEM_LIMIT),
        name="gla_sample",
    )(x.reshape(n * seq, D_MODEL), p_all.reshape(-1, n * seq, PLE_DIM), s0, *weights)
    return y.reshape(n, seq, D_MODEL), st


def _resident_spec(shape):
    nd = len(shape)
    return pl.BlockSpec(shape, lambda *_: (0,) * nd, pipeline_mode=pl.Buffered(1))


def _t5_bias(rel_bias, dist):
    max_exact = N_BUCKETS // 2
    n = jnp.maximum(dist, 0)[None]
    nf = jnp.maximum(n, 1).astype(F32)
    steps = jnp.log(nf / max_exact) / math.log(MAX_DISTANCE / max_exact) * (N_BUCKETS - max_exact)
    table = rel_bias.astype(F32)
    per_head = (DIFF_HEADS,) + (1,) * dist.ndim
    out = jnp.broadcast_to(table[N_BUCKETS - 1].reshape(per_head), (DIFF_HEADS,) + dist.shape)
    for b in range(N_BUCKETS - 2, max_exact - 1, -1):
        out = jnp.where(steps < (b + 1 - max_exact), table[b].reshape(per_head), out)
    for b in range(max_exact):
        out = jnp.where(n == b, table[b].reshape(per_head), out)
    return out


def _store_token_head_rows(ref, lead, x):
    tokens = x.shape[0]
    for h in range(DIFF_HEADS):
        ref[(*lead, pl.ds(h, tokens, stride=DIFF_HEADS), slice(None))] = x[:, h * DIFF_VD:(h + 1) * DIFF_VD]


def _load_token_head_rows(ref, lead, tokens):
    return jnp.concatenate(
        [ref[(*lead, pl.ds(h, tokens, stride=DIFF_HEADS), slice(None))] for h in range(DIFF_HEADS)], axis=1)


def _diff_lambda(lam_ref, lam_init):
    lv = lam_ref[...]
    a = jnp.sum(lv[0:1] * lv[1:2], axis=-1, keepdims=True)
    b = jnp.sum(lv[2:3] * lv[3:4], axis=-1, keepdims=True)
    return jnp.exp(a) - jnp.exp(b) + lam_init


def _diff_prompt_kernel(x_ref, p_ref, wq_ref, wk_ref, wv_ref, wg_ref, lam_ref, bias_ref, ng_ref,
                        wout_ref, lng_ref, lnb_ref, wgate_ref, wproj_ref,
                        y_ref, kout_ref, vout_ref,
                        k_scr, vt_scr, q_scr, o_scr, acc_scr, m_scr, alpha_scr, *bufs,
                        lam_init, heads_per_iter):
    i = pl.program_id(1)
    tq = x_ref.shape[1]
    x = x_ref[0]
    xb = x.astype(BF16)
    q = _dot(xb, wq_ref[...]) * (DIFF_DH ** -0.5 * LOG2_E)
    k = _dot(xb, wk_ref[...])
    v = _dot(xb, wv_ref[...])
    _store_token_head_rows(kout_ref, (0,), k)
    _store_token_head_rows(vout_ref, (0,), v)
    first_half = lax.broadcasted_iota(jnp.int32, (DIFF_VD, tq), 0) < DIFF_DH
    row0 = pl.multiple_of(i * tq, tq)
    for h in range(DIFF_HEADS):
        hs = slice(h * DIFF_VD, (h + 1) * DIFF_VD)
        qh_t = q[:, hs].T
        q_scr[h, :, 0:tq] = jnp.where(first_half, qh_t, 0.0).astype(BF16)
        q_scr[h, :, tq:2 * tq] = jnp.where(first_half, 0.0, qh_t).astype(BF16)
        k_scr[h, pl.ds(row0, tq), :] = k[:, hs].astype(BF16)
        vt_scr[h, i, 0:DIFF_VD, :] = v[:, hs].T.astype(BF16)
        vt_scr[h, i, DIFF_VD:, :] = jnp.ones((SUM_ROWS, tq), BF16)
    lam = _diff_lambda(lam_ref, lam_init)

    s_buf = [bufs[2 * u:2 * u + 2] for u in range(heads_per_iter)]
    p_buf = [bufs[2 * heads_per_iter + 2 * u:2 * heads_per_iter + 2 * u + 2] for u in range(heads_per_iter)]

    def block_of(t):
        near = jnp.maximum(i - t, 0)
        if isinstance(t, int):
            return near if t < 2 else jnp.minimum(t - 2, i)
        return jnp.where(t < 2, near, jnp.minimum(t - 2, i))

    def group_body(grp, carry):
        heads = [grp * heads_per_iter + u for u in range(heads_per_iter)]
        m_scr[...] = jnp.full(m_scr.shape, MASK_VALUE, F32)
        acc_scr[...] = jnp.zeros(acc_scr.shape, F32)

        def scores(t, slot):
            rows = pl.ds(pl.multiple_of(block_of(t) * tq, tq), tq)
            for u, h in enumerate(heads):
                s_buf[u][slot][...] = _dot(k_scr[h, rows, :], q_scr[h])

        def softmax(t, slot, bias_slot):
            valid = t <= i
            for u, h in enumerate(heads):
                s = s_buf[u][slot][...]
                if bias_slot is not None:
                    tile = bias_ref[h, bias_slot]
                    s = s + jnp.concatenate([tile, tile], axis=1)
                m_old = m_scr[u]
                m_new = jnp.where(valid, jnp.maximum(m_old, jnp.max(s, axis=0, keepdims=True)), m_old)
                alpha = jnp.exp2(m_old - m_new)
                pr = jnp.exp2(s - jnp.where(valid, m_new, -MASK_VALUE))
                m_scr[u] = m_new
                alpha_scr[u] = alpha
                p_buf[u][slot][...] = pr.astype(BF16)

        def weighted_values(t, slot):
            j = block_of(t)
            for u, h in enumerate(heads):
                acc_scr[u] = acc_scr[u] * alpha_scr[u] + _dot(vt_scr[h, j], p_buf[u][slot][...])

        def step(t, slot, bias_slot):
            weighted_values(t - 1, 1 - slot)
            scores(t + 1, 1 - slot)
            softmax(t, slot, bias_slot)

        scores(0, 0)
        softmax(0, 0, 0)
        scores(1, 1)
        step(1, 1, 1)

        def two_far_steps(r, c):
            step(2 * r, 0, None)
            step(2 * r + 1, 1, None)
            return c

        n_pairs = (i + 2) // 2
        lax.fori_loop(1, n_pairs, two_far_steps, 0)
        weighted_values(2 * n_pairs - 1, 1)

        for u, h in enumerate(heads):
            inv_l = 1.0 / acc_scr[u, DIFF_VD:DIFF_VD + 1, :]
            acc = acc_scr[u, 0:DIFF_VD, :]
            o_t = acc[:, :tq] * inv_l[:, :tq] - lam * (acc[:, tq:] * inv_l[:, tq:])
            o_scr[h] = o_t.T
        return carry

    lax.fori_loop(0, DIFF_HEADS // heads_per_iter, group_body, 0)
    o = jnp.concatenate([o_scr[h] for h in range(DIFF_HEADS)], axis=-1)
    gate = _dot(xb, wg_ref[...])
    on = _head_rmsnorm_gate(o, gate, ng_ref[...], DIFF_HEADS, DIFF_VD, 1.0 - lam_init)
    f = _dot(on.astype(BF16), wout_ref[...])
    y_ref[0] = _residual_update(x, f, p_ref[0], lng_ref[...], lnb_ref[...], wgate_ref[...], wproj_ref[...])


def _diff_weights(w_in, lam_q1, lam_k1, lam_q2, lam_k2, norm_g, w_out, ln_g, ln_b, w_gate, w_proj):
    w = DIFF_WIDTH
    return dict(
        wq=w_in[:, :w].astype(BF16), wk=w_in[:, w:2 * w].astype(BF16),
        wv=w_in[:, 2 * w:3 * w].astype(BF16), wg=w_in[:, 3 * w:].astype(BF16),
        lam=jnp.stack([lam_q1, lam_k1, lam_q2, lam_k2]).astype(F32),
        ng=norm_g.reshape(1, w), wout=w_out.astype(BF16),
        lng=ln_g.reshape(1, D_MODEL), lnb=ln_b.reshape(1, D_MODEL),
        wgate=w_gate.astype(BF16), wproj=w_proj.astype(BF16))


def diff_layer_prompt(x, p_all, layer, dw, rel_bias, lam_init, *, block=256, heads_per_iter=2):
    bsz, t, _ = x.shape
    assert t % block == 0 and block >= MAX_DISTANCE and DIFF_HEADS % heads_per_iter == 0
    kk = jnp.arange(block, dtype=jnp.int32)[:, None]
    qq = jnp.arange(block, dtype=jnp.int32)[None, :]
    dist = jnp.stack([qq - kk, block + qq - kk])
    far = rel_bias.astype(F32)[N_BUCKETS - 1].reshape(DIFF_HEADS, 1, 1, 1)
    bias = jnp.where(dist >= 0, (_t5_bias(rel_bias, dist) - far) * LOG2_E, MASK_VALUE)
    tok = lambda width: pl.BlockSpec((1, block, width), lambda b, i: (b, i, 0))
    kv_spec = pl.BlockSpec((1, block * DIFF_HEADS, DIFF_VD), lambda b, i: (b, i, 0))
    consts = [dw["wq"], dw["wk"], dw["wv"], dw["wg"], dw["lam"], bias, dw["ng"], dw["wout"],
              dw["lng"], dw["lnb"], dw["wgate"], dw["wproj"]]
    return pl.pallas_call(
        functools.partial(_diff_prompt_kernel, lam_init=lam_init, heads_per_iter=heads_per_iter),
        grid=(bsz, t // block),
        in_specs=[tok(D_MODEL), pl.BlockSpec((None, 1, block, PLE_DIM), lambda b, i: (layer, b, i, 0))]
                 + [_resident_spec(c.shape) for c in consts],
        out_specs=[tok(D_MODEL), kv_spec, kv_spec],
        out_shape=[jax.ShapeDtypeStruct((bsz, t, D_MODEL), F32),
                   jax.ShapeDtypeStruct((bsz, t * DIFF_HEADS, DIFF_VD), F32),
                   jax.ShapeDtypeStruct((bsz, t * DIFF_HEADS, DIFF_VD), F32)],
        scratch_shapes=[pltpu.VMEM((DIFF_HEADS, t, DIFF_VD), BF16),
                        pltpu.VMEM((DIFF_HEADS, t // block, DIFF_VD + SUM_ROWS, block), BF16),
                        pltpu.VMEM((DIFF_HEADS, DIFF_VD, 2 * block), BF16),
                        pltpu.VMEM((DIFF_HEADS, block, DIFF_VD), F32),
                        pltpu.VMEM((heads_per_iter, DIFF_VD + SUM_ROWS, 2 * block), F32),
                        pltpu.VMEM((heads_per_iter, 1, 2 * block), F32),
                        pltpu.VMEM((heads_per_iter, 1, 2 * block), F32),
                        *[pltpu.VMEM((block, 2 * block), F32)] * (2 * heads_per_iter),
                        *[pltpu.VMEM((block, 2 * block), BF16)] * (2 * heads_per_iter)],
        compiler_params=pltpu.CompilerParams(dimension_semantics=("arbitrary", "arbitrary"),
                                             vmem_limit_bytes=VMEM_LIMIT),
        name="diff_prompt",
    )(x, p_all, *consts)


def _diff_sample_project_kernel(x_ref, wq_ref, wk_ref, wv_ref, qt_ref, k_ref, v_ref):
    xb = x_ref[...].astype(BF16)
    qt_ref[...] = (_dot(xb, wq_ref[...]) * (DIFF_DH ** -0.5)).T.astype(BF16)
    _store_token_head_rows(k_ref, (), _dot(xb, wk_ref[...]))
    _store_token_head_rows(v_ref, (), _dot(xb, wv_ref[...]))


def _row_to_col(row, n):
    eye = lax.broadcasted_iota(jnp.int32, (n, n), 0) == lax.broadcasted_iota(jnp.int32, (n, n), 1)
    return jnp.sum(jnp.where(eye, row, 0.0), axis=1, keepdims=True)


def _paged_attn_kernel(pt_ref, qt_ref, kn_ref, vn_ref, lam_ref, bias_ref, bnew_ref, ck_hbm, cv_hbm, o_ref,
                       k_ring, v_ring, sems, wq_scr, acc_scr, m_scr, l_scr, *, pages, seq, lam_init):
    b = pl.program_id(0)
    g = pl.program_id(1)
    n_steps = pl.num_programs(0) * pl.num_programs(1)
    step = b * pl.num_programs(1) + g
    ncol = 2 * DIFF_HEADS * seq

    def page_copies(s, slot):
        copies = []
        for i in range(pages):
            page = pt_ref[s * pages + i]
            copies.append(pltpu.make_async_copy(ck_hbm.at[page], k_ring.at[slot, i], sems.at[0, slot]))
            copies.append(pltpu.make_async_copy(cv_hbm.at[page], v_ring.at[slot, i], sems.at[1, slot]))
        return copies

    @pl.when(step == 0)
    def _():
        for ahead in range(PAGE_RING - 1):
            @pl.when(ahead < n_steps)
            def _():
                for c in page_copies(ahead, ahead):
                    c.start()

    nxt = step + PAGE_RING - 1

    @pl.when(nxt < n_steps)
    def _():
        for c in page_copies(nxt, lax.rem(nxt, PAGE_RING)):
            c.start()

    slot = lax.rem(step, PAGE_RING)
    for c in page_copies(step, slot):
        c.wait()
    k_pages = [k_ring.at[slot, i] for i in range(pages)]
    v_pages = [v_ring.at[slot, i] for i in range(pages)]

    @pl.when(g == 0)
    def _():
        local = (b % (LANES // seq)) * seq
        src = lax.broadcasted_iota(jnp.int32, (LANES, ncol), 0)
        col = lax.broadcasted_iota(jnp.int32, (LANES, ncol), 1)
        pick = jnp.where(src == local + col % seq, 1.0, 0.0).astype(BF16)
        rep = _dot(qt_ref[...], pick)
        r = lax.broadcasted_iota(jnp.int32, (DIFF_WIDTH, ncol), 0)
        c = lax.broadcasted_iota(jnp.int32, (DIFF_WIDTH, ncol), 1)
        own = ((r // DIFF_VD) == ((c % (DIFF_HEADS * seq)) // seq)) & \
              (((r % DIFF_VD) // DIFF_DH) == (c // (DIFF_HEADS * seq)))
        wq_scr[...] = jnp.where(own, rep, 0.0).astype(BF16)
        m_scr[...] = jnp.full(m_scr.shape, MASK_VALUE, F32)
        l_scr[...] = jnp.zeros(l_scr.shape, F32)
        acc_scr[...] = jnp.zeros(acc_scr.shape, F32)

    def flash_step(scores, values):
        m_old = m_scr[...]
        m_new = m_old
        for s in scores:
            m_new = jnp.maximum(m_new, jnp.max(s, axis=0, keepdims=True))
        alpha = jnp.exp(m_old - m_new)
        l_new = alpha * l_scr[...]
        probs = []
        for s in scores:
            pr = jnp.exp(s - m_new)
            l_new = l_new + jnp.sum(pr, axis=0, keepdims=True)
            probs.append(pr.astype(BF16))
        pv = None
        for a in range(0, len(probs), 2):
            t = _dot_ta(jnp.concatenate(probs[a:a + 2], axis=0),
                        jnp.concatenate(values[a:a + 2], axis=0))
            pv = t if pv is None else pv + t
        acc_scr[...] = acc_scr[...] * _row_to_col(alpha, ncol) + pv
        l_scr[...] = l_new
        m_scr[...] = m_new

    def load_page(ref):
        return _load_token_head_rows(ref, (), PAGE_SIZE).astype(BF16)

    wq = wq_scr[...]
    scores, values = [], []
    for i in range(pages):
        s = _dot(load_page(k_pages[i]), wq)
        scores.append(s + bias_ref[0] if i == pages - 1 else s)
        values.append(load_page(v_pages[i]))
    flash_step(scores, values)

    @pl.when(g == pl.num_programs(1) - 1)
    def _():
        pad = jnp.zeros((16 - seq, DIFF_WIDTH), F32)
        kn = jnp.concatenate([_load_token_head_rows(kn_ref, (0,), seq), pad], axis=0).astype(BF16)
        vn = jnp.concatenate([_load_token_head_rows(vn_ref, (0,), seq), pad], axis=0).astype(BF16)
        flash_step([_dot(kn, wq) + bnew_ref[...]], [vn])
        lam = _diff_lambda(lam_ref, lam_init)
        inv_l = _row_to_col(1.0 / l_scr[...], ncol)
        half_rows = DIFF_HEADS * seq
        for h in range(DIFF_HEADS):
            cs = slice(h * DIFF_VD, (h + 1) * DIFF_VD)
            r1 = slice(h * seq, (h + 1) * seq)
            r2 = slice(half_rows + h * seq, half_rows + (h + 1) * seq)
            o_ref[0, :, cs] = acc_scr[r1, cs] * inv_l[r1] - lam * (acc_scr[r2, cs] * inv_l[r2])


def _diff_sample_out_kernel(x_ref, o_ref, p_ref, wg_ref, ng_ref, wout_ref, lng_ref, lnb_ref, wgate_ref, wproj_ref,
                            y_ref, *, lam_init):
    x = x_ref[...]
    gate = _dot(x.astype(BF16), wg_ref[...])
    on = _head_rmsnorm_gate(o_ref[...], gate, ng_ref[...], DIFF_HEADS, DIFF_VD, 1.0 - lam_init)
    f = _dot(on.astype(BF16), wout_ref[...])
    y_ref[...] = _residual_update(x, f, p_ref[...], lng_ref[...], lnb_ref[...], wgate_ref[...], wproj_ref[...])


def diff_layer_sample(x, p_all, layer, cache_k, cache_v, page_table, dw, rel_bias, lam_init, *, pages=8, block=256):
    n, seq, _ = x.shape
    n_pages = page_table.shape[1]
    rows = n * seq
    assert n_pages % pages == 0 and rows % block == 0 and LANES % seq == 0 and seq <= 16
    assert PAGE_SIZE >= MAX_DISTANCE
    x2 = x.reshape(rows, D_MODEL)
    row_spec = lambda width: pl.BlockSpec((block, width), lambda i: (i, 0))
    qt, kn, vn = pl.pallas_call(
        _diff_sample_project_kernel,
        grid=(rows // block,),
        in_specs=[row_spec(D_MODEL)] + [_const_spec((D_MODEL, DIFF_WIDTH))] * 3,
        out_specs=[pl.BlockSpec((DIFF_WIDTH, block), lambda i: (0, i)),
                   pl.BlockSpec((block * DIFF_HEADS, DIFF_VD), lambda i: (i, 0)),
                   pl.BlockSpec((block * DIFF_HEADS, DIFF_VD), lambda i: (i, 0))],
        out_shape=[jax.ShapeDtypeStruct((DIFF_WIDTH, rows), BF16),
                   jax.ShapeDtypeStruct((rows * DIFF_HEADS, DIFF_VD), F32),
                   jax.ShapeDtypeStruct((rows * DIFF_HEADS, DIFF_VD), F32)],
        compiler_params=pltpu.CompilerParams(dimension_semantics=("arbitrary",), vmem_limit_bytes=VMEM_LIMIT),
        name="diff_sample_project",
    )(x2, dw["wq"], dw["wk"], dw["wv"])

    ncol = 2 * DIFF_HEADS * seq
    col = jnp.arange(ncol, dtype=jnp.int32)[None, :]
    col_h = (col % (DIFF_HEADS * seq)) // seq
    col_t = col % seq
    kk = jnp.arange(PAGE_SIZE, dtype=jnp.int32)[:, None]
    tk = jnp.arange(16, dtype=jnp.int32)[:, None]
    dist_last = PAGE_SIZE + col_t - kk
    dist_new = col_t - tk
    far = rel_bias.astype(F32)[N_BUCKETS - 1].reshape(DIFF_HEADS, 1, 1)

    def own_head(per_head):
        return sum(jnp.where(col_h == h, per_head[h], 0.0) for h in range(DIFF_HEADS))

    bias_last = own_head(_t5_bias(rel_bias, dist_last) - far)
    bias_pages = jnp.stack([jnp.zeros_like(bias_last), bias_last])
    bias_new = jnp.where((tk < seq) & (dist_new >= 0), own_head(_t5_bias(rel_bias, dist_new) - far), MASK_VALUE)

    n_groups = n_pages // pages
    seq_per_blk = LANES // seq
    page_shape = (PAGE_RING, pages, PAGE_SIZE * DIFF_HEADS, DIFF_VD)

    tok_spec = pl.BlockSpec((1, seq, DIFF_WIDTH), lambda b, g, pt: (b, 0, 0))
    new_spec = pl.BlockSpec((1, seq * DIFF_HEADS, DIFF_VD), lambda b, g, pt: (b, 0, 0))
    grid_spec = pltpu.PrefetchScalarGridSpec(
        num_scalar_prefetch=1,
        grid=(n, n_groups),
        in_specs=[pl.BlockSpec((DIFF_WIDTH, LANES), lambda b, g, pt: (0, b // seq_per_blk)),
                  new_spec, new_spec,
                  pl.BlockSpec((4, DIFF_DH), lambda b, g, pt: (0, 0)),
                  pl.BlockSpec((1, PAGE_SIZE, ncol), lambda b, g, pt: ((g + 1) // n_groups, 0, 0)),
                  pl.BlockSpec((16, ncol), lambda b, g, pt: (0, 0)),
                  pl.BlockSpec(memory_space=pl.ANY), pl.BlockSpec(memory_space=pl.ANY)],
        out_specs=tok_spec,
        scratch_shapes=[pltpu.VMEM(page_shape, F32), pltpu.VMEM(page_shape, F32),
                        pltpu.SemaphoreType.DMA((2, PAGE_RING)),
                        pltpu.VMEM((DIFF_WIDTH, ncol), BF16),
                        pltpu.VMEM((ncol, DIFF_WIDTH), F32),
                        pltpu.VMEM((1, ncol), F32),
                        pltpu.VMEM((1, ncol), F32)])
    o = pl.pallas_call(
        functools.partial(_paged_attn_kernel, pages=pages, seq=seq, lam_init=lam_init),
        grid_spec=grid_spec,
        out_shape=jax.ShapeDtypeStruct((n, seq, DIFF_WIDTH), F32),
        compiler_params=pltpu.CompilerParams(dimension_semantics=("arbitrary", "arbitrary"),
                                             vmem_limit_bytes=VMEM_LIMIT),
        name="diff_sample_attn",
    )(page_table.reshape(-1), qt, kn.reshape(n, seq * DIFF_HEADS, DIFF_VD), vn.reshape(n, seq * DIFF_HEADS, DIFF_VD),
      dw["lam"], bias_pages, bias_new, cache_k, cache_v)

    consts = [dw["wg"], dw["ng"], dw["wout"], dw["lng"], dw["lnb"], dw["wgate"], dw["wproj"]]
    y = pl.pallas_call(
        functools.partial(_diff_sample_out_kernel, lam_init=lam_init),
        grid=(rows // block,),
        in_specs=[row_spec(D_MODEL), row_spec(DIFF_WIDTH),
                  pl.BlockSpec((None, block, PLE_DIM), lambda i: (layer, i, 0))] + [_const_spec(c.shape) for c in consts],
        out_specs=row_spec(D_MODEL),
        out_shape=jax.ShapeDtypeStruct((rows, D_MODEL), F32),
        compiler_params=pltpu.CompilerParams(dimension_semantics=("arbitrary",), vmem_limit_bytes=VMEM_LIMIT),
        name="diff_sample_out",
    )(x2, o.reshape(rows, DIFF_WIDTH), p_all.reshape(-1, rows, PLE_DIM), *consts)
    return y.reshape(n, seq, D_MODEL), kn, vn


def kernel(x_prompt, x_sample, state_gla, cache_k, cache_v, page_table, p_prompt, p_sample, rel_bias,
           gla_w_in, gla_w_a2, gla_b_a, gla_norm_g, gla_w_out,
           diff_w_in, diff_lam_q1, diff_lam_k1, diff_lam_q2, diff_lam_k2, diff_norm_g, diff_w_out,
           ln_g, ln_b, ple_w_proj, ple_w_gate):
    w0 = _gla_weights(gla_w_in[0], gla_w_a2[0], gla_b_a[0], gla_norm_g[0], gla_w_out[0],
                      ln_g[0], ln_b[0], ple_w_gate[0], ple_w_proj[0])
    xp1, sp = gla_layer_prompt(x_prompt, p_prompt, 0, w0)
    xs1, ss = gla_layer_sample(x_sample, p_sample, 0, state_gla[0], w0)
    lam_init = 0.8 - 0.6 * math.exp(-0.3 * 1)
    dw = _diff_weights(diff_w_in[0], diff_lam_q1[0], diff_lam_k1[0], diff_lam_q2[0], diff_lam_k2[0],
                       diff_norm_g[0], diff_w_out[0], ln_g[1], ln_b[1], ple_w_gate[1], ple_w_proj[1])
    yp, kp, vp = diff_layer_prompt(xp1, p_prompt, 1, dw, rel_bias, lam_init)
    pool = cache_k.shape[1]
    ys, ks, vs = diff_layer_sample(xs1, p_sample, 1,
                                   cache_k[0].reshape(pool, PAGE_SIZE * DIFF_HEADS, DIFF_VD),
                                   cache_v[0].reshape(pool, PAGE_SIZE * DIFF_HEADS, DIFF_VD),
                                   page_table, dw, rel_bias, lam_init)
    bsz, t, _ = x_prompt.shape
    n, seq, _ = x_sample.shape
    heads = (DIFF_HEADS, DIFF_VD)
    return (yp, ys, sp[None], ss[None],
            kp.reshape(1, bsz, t, *heads), vp.reshape(1, bsz, t, *heads),
            ks.reshape(1, n, seq, *heads), vs.reshape(1, n, seq, *heads))
```

```python
import functools
import math

import jax
import jax.numpy as jnp
from jax import lax
from jax.experimental import pallas as pl
from jax.experimental.pallas import tpu as pltpu

F32 = jnp.float32
BF16 = jnp.bfloat16

D_MODEL = 1024
DEPTH = 2
GLA_HEADS = 4
GLA_DK = 128
GLA_DV = 256
GLA_HK = GLA_HEADS * GLA_DK
GLA_HV = GLA_HEADS * GLA_DV
GLA_LOWRANK = 16
GLA_TAU = 16.0
DIFF_HEADS = 8
DIFF_DH = 64
DIFF_VD = 128
DIFF_WIDTH = DIFF_HEADS * DIFF_VD
N_BUCKETS = 32
MAX_DISTANCE = 128
PLE_DIM = 256
PAGE_SIZE = 128
ALPHA = (2 * DEPTH) ** 0.25
EPS = 1e-5

LANES = 128
GLA_CHUNK = 64
VMEM_LIMIT = 56 * 1024 * 1024
MASK_VALUE = -1e30
LOG2_E = math.log2(math.e)
SUM_ROWS = 16
PAGE_RING = 4


def _dot(a, b):
    return jnp.dot(a, b, preferred_element_type=F32)


def _dot_tb(a, b):
    return lax.dot_general(a, b, (((1,), (1,)), ((), ())), preferred_element_type=F32)


def _dot_ta(a, b):
    return lax.dot_general(a, b, (((0,), (0,)), ((), ())), preferred_element_type=F32)


def _split_bf16(x):
    hi = x.astype(BF16)
    mid = (x - hi.astype(F32)).astype(BF16)
    return hi, mid


def _log_sigmoid(z):
    return jnp.minimum(z, 0.0) - jnp.log(1.0 + jnp.exp(-jnp.abs(z)))


def _sigmoid(z):
    return 1.0 / (1.0 + jnp.exp(-z))


def _head_rmsnorm_gate(o, gate, norm_g, n_heads, head_dim, scale):
    parts = []
    for h in range(n_heads):
        oh = o[:, h * head_dim:(h + 1) * head_dim]
        ms = jnp.mean(oh * oh, axis=-1, keepdims=True)
        parts.append(oh * lax.rsqrt(ms + EPS))
    on = jnp.concatenate(parts, axis=-1) * norm_g
    if scale != 1.0:
        on = on * scale
    return on * (gate * _sigmoid(gate))


def _residual_update(x, f, p, ln_g, ln_b, w_gate, w_proj):
    hp = ALPHA * x + f
    mu = jnp.mean(hp, axis=-1, keepdims=True)
    hc = hp - mu
    var = jnp.mean(hc * hc, axis=-1, keepdims=True)
    h = hc * lax.rsqrt(var + EPS) * ln_g + ln_b
    gate = _sigmoid(_dot(h.astype(BF16), w_gate))
    return h + gate * _dot(p.astype(BF16), w_proj)


def _gla_project(xb, wq, wk, wv, wg, wa, wa2, ba):
    q = _dot(xb, wq) * (GLA_DK ** -0.5)
    k = _dot(xb, wk)
    v = _dot(xb, wv)
    g = _dot(xb, wg)
    a_lr = _dot(xb, wa)
    z = _dot(a_lr.astype(BF16), wa2) + ba
    log_a = _log_sigmoid(z) * (1.0 / GLA_TAU)
    return q, k, v, g, log_a


def _chunk_decay_matrices(rows, chunk):
    half = chunk // 2
    i = lax.broadcasted_iota(jnp.int32, (rows, rows), 0)
    j = lax.broadcasted_iota(jnp.int32, (rows, rows), 1)
    same = (i // chunk) == (j // chunk)
    jl = j % chunk
    il = i % chunk
    pos = same & (jl >= half) & (jl <= il)
    neg = same & (jl < half) & (jl > il)
    rel = jnp.where(pos, 1.0, jnp.where(neg, -1.0, 0.0)).astype(BF16)
    n_sel = max(8, 2 * rows // chunk)
    s = lax.broadcasted_iota(jnp.int32, (n_sel, rows), 0)
    t = lax.broadcasted_iota(jnp.int32, (n_sel, rows), 1)
    halves = jnp.where((t // half) == s, 1.0, 0.0).astype(BF16)
    return rel, halves


def _gla_prompt_kernel(x_ref, p_ref, wq_ref, wk_ref, wv_ref, wg_ref, wa_ref, wa2_ref, ba_ref, ng_ref,
                       wout_ref, lng_ref, lnb_ref, wgate_ref, wproj_ref,
                       y_ref, st_ref, s_scr, o_scr, *, chunk):
    blk = pl.program_id(1)
    rows = o_scr.shape[1]

    @pl.when(blk == 0)
    def _():
        s_scr[...] = jnp.zeros_like(s_scr)

    ci = lax.broadcasted_iota(jnp.int32, (chunk, chunk), 0)
    cj = lax.broadcasted_iota(jnp.int32, (chunk, chunk), 1)
    causal = ci >= cj
    rel, halves = _chunk_decay_matrices(rows, chunk)
    for sb in range(x_ref.shape[1] // rows):
        tok = slice(sb * rows, (sb + 1) * rows)
        x = x_ref[0, tok, :]
        xb = x.astype(BF16)
        q, k, v, g, log_a = _gla_project(xb, wq_ref[...], wk_ref[...], wv_ref[...], wg_ref[...],
                                         wa_ref[...], wa2_ref[...], ba_ref[...])
        la_hi, la_mid = _split_bf16(log_a)
        d = _dot(rel, la_hi) + _dot(rel, la_mid)
        hs = _dot(halves, la_hi) + _dot(halves, la_mid)
        ehs = jnp.exp(hs)
        q_dec = (q * jnp.exp(d)).astype(BF16)
        k_inv = (k * jnp.exp(-d)).astype(BF16)
        vb = v.astype(BF16)

        for c in range(rows // chunk):
            r0 = c * chunk
            for h in range(GLA_HEADS):
                ks = slice(h * GLA_DK, (h + 1) * GLA_DK)
                vs = slice(h * GLA_DV, (h + 1) * GLA_DV)
                qd = q_dec[r0:r0 + chunk, ks]
                ki = k_inv[r0:r0 + chunk, ks]
                vh = vb[r0:r0 + chunk, vs]
                e_first = ehs[2 * c:2 * c + 1, ks]
                e_second = ehs[2 * c + 1:2 * c + 2, ks]
                s_mid = s_scr[h] * e_first
                att = jnp.where(causal, _dot_tb(qd, ki), 0.0).astype(BF16)
                o_scr[sb, r0:r0 + chunk, vs] = _dot(att, vh) + _dot_tb(qd, s_mid.astype(BF16))
                s_scr[h] = (s_mid + _dot_ta(vh, ki)) * e_second

        on = _head_rmsnorm_gate(o_scr[sb], g, ng_ref[...], GLA_HEADS, GLA_DV, 1.0)
        f = _dot(on.astype(BF16), wout_ref[...])
        y_ref[0, tok, :] = _residual_update(x, f, p_ref[0, tok, :], lng_ref[...], lnb_ref[...],
                                            wgate_ref[...], wproj_ref[...])

    @pl.when(blk == pl.num_programs(1) - 1)
    def _():
        for h in range(GLA_HEADS):
            st_ref[0, h] = s_scr[h].T


def _const_spec(shape):
    nd = len(shape)
    return pl.BlockSpec(shape, lambda *_: (0,) * nd)


def _gla_weights(w_in, w_a2, b_a, norm_g, w_out, ln_g, ln_b, w_gate, w_proj):
    hk, hv = GLA_HK, GLA_HV
    wq = w_in[:, :hk].astype(BF16)
    wk = w_in[:, hk:2 * hk].astype(BF16)
    wv = w_in[:, 2 * hk:2 * hk + hv].astype(BF16)
    wg = w_in[:, 2 * hk + hv:2 * hk + 2 * hv].astype(BF16)
    wa = jnp.pad(w_in[:, 2 * hk + 2 * hv:], ((0, 0), (0, LANES - GLA_LOWRANK))).astype(BF16)
    wa2 = jnp.pad(w_a2, ((0, LANES - GLA_LOWRANK), (0, 0))).astype(BF16)
    return (wq, wk, wv, wg, wa, wa2, b_a.reshape(1, hk), norm_g.reshape(1, hv),
            w_out.astype(BF16), ln_g.reshape(1, D_MODEL), ln_b.reshape(1, D_MODEL),
            w_gate.astype(BF16), w_proj.astype(BF16))


def gla_layer_prompt(x, p_all, layer, weights, *, block=512, sub_block=256, chunk=GLA_CHUNK):
    bsz, t, _ = x.shape
    assert t % block == 0 and block % sub_block == 0 and sub_block % chunk == 0
    in_specs = [pl.BlockSpec((1, block, D_MODEL), lambda b, i: (b, i, 0)),
                pl.BlockSpec((None, 1, block, PLE_DIM), lambda b, i: (layer, b, i, 0))]
    in_specs += [_const_spec(w.shape) for w in weights]
    return pl.pallas_call(
        functools.partial(_gla_prompt_kernel, chunk=chunk),
        grid=(bsz, t // block),
        in_specs=in_specs,
        out_specs=[pl.BlockSpec((1, block, D_MODEL), lambda b, i: (b, i, 0)),
                   pl.BlockSpec((1, GLA_HEADS, GLA_DK, GLA_DV), lambda b, i: (b, 0, 0, 0))],
        out_shape=[jax.ShapeDtypeStruct((bsz, t, D_MODEL), F32),
                   jax.ShapeDtypeStruct((bsz, GLA_HEADS, GLA_DK, GLA_DV), F32)],
        scratch_shapes=[pltpu.VMEM((GLA_HEADS, GLA_DV, GLA_DK), F32),
                        pltpu.VMEM((block // sub_block, sub_block, GLA_HV), F32)],
        compiler_params=pltpu.CompilerParams(dimension_semantics=("arbitrary", "arbitrary"),
                                             vmem_limit_bytes=VMEM_LIMIT),
        name="gla_prompt",
    )(x, p_all, *weights)


def _gla_sample_kernel(x_ref, p_ref, s0_ref, wq_ref, wk_ref, wv_ref, wg_ref, wa_ref, wa2_ref, ba_ref, ng_ref,
                       wout_ref, lng_ref, lnb_ref, wgate_ref, wproj_ref,
                       y_ref, st_ref, o_scr, *, seq):
    rows = x_ref.shape[0]
    n_seq = rows // seq
    half = seq // 2
    x = x_ref[...]
    xb = x.astype(BF16)
    q, k, v, g, log_a = _gla_project(xb, wq_ref[...], wk_ref[...], wv_ref[...], wg_ref[...],
                                     wa_ref[...], wa2_ref[...], ba_ref[...])
    rel, _ = _chunk_decay_matrices(rows, seq)
    la_hi, la_mid = _split_bf16(log_a)
    d = _dot(rel, la_hi) + _dot(rel, la_mid)
    s_i = lax.broadcasted_iota(jnp.int32, (n_seq, rows), 0)
    t_i = lax.broadcasted_iota(jnp.int32, (n_seq, rows), 1)
    in_seq = (t_i // seq) == s_i
    sel_first = jnp.where(in_seq & ((t_i % seq) < half), 1.0, 0.0).astype(BF16)
    sel_second = jnp.where(in_seq & ((t_i % seq) >= half), 1.0, 0.0).astype(BF16)
    t_c = lax.broadcasted_iota(jnp.int32, (rows, n_seq), 0)
    s_c = lax.broadcasted_iota(jnp.int32, (rows, n_seq), 1)
    sel_tot_t = jnp.where((t_c // seq) == s_c, 1.0, 0.0).astype(BF16)
    e_first = jnp.exp(_dot(sel_first, la_hi) + _dot(sel_first, la_mid))
    e_second = jnp.exp(_dot(sel_second, la_hi) + _dot(sel_second, la_mid))
    e_tot_col = jnp.exp(_dot_ta(la_hi, sel_tot_t) + _dot_ta(la_mid, sel_tot_t))
    q_dec = q * jnp.exp(d)
    k_inv = k * jnp.exp(-d)

    ci = lax.broadcasted_iota(jnp.int32, (seq, seq), 0)
    cj = lax.broadcasted_iota(jnp.int32, (seq, seq), 1)
    causal = ci >= cj
    for c in range(n_seq):
        r0 = c * seq
        for h in range(GLA_HEADS):
            ks = slice(h * GLA_DK, (h + 1) * GLA_DK)
            vs = slice(h * GLA_DV, (h + 1) * GLA_DV)
            qd = q_dec[r0:r0 + seq, ks]
            ki = k_inv[r0:r0 + seq, ks]
            vh = v[r0:r0 + seq, vs].astype(BF16)
            s_old = s0_ref[c, h]
            att = jnp.where(causal, _dot_tb(qd.astype(BF16), ki.astype(BF16)), 0.0).astype(BF16)
            q_mid = (qd * e_first[c:c + 1, ks]).astype(BF16)
            o_scr[r0:r0 + seq, vs] = _dot(att, vh) + _dot(q_mid, s_old.astype(BF16))
            k_end = (ki * e_second[c:c + 1, ks]).astype(BF16)
            st_ref[c, h] = s_old * e_tot_col[h * GLA_DK:(h + 1) * GLA_DK, c:c + 1] + _dot_ta(k_end, vh)

    on = _head_rmsnorm_gate(o_scr[...], g, ng_ref[...], GLA_HEADS, GLA_DV, 1.0)
    f = _dot(on.astype(BF16), wout_ref[...])
    y_ref[...] = _residual_update(x, f, p_ref[...], lng_ref[...], lnb_ref[...], wgate_ref[...], wproj_ref[...])


def gla_layer_sample(x, p_all, layer, s0, weights, *, group=16):
    n, seq, _ = x.shape
    assert n % group == 0 and seq % 2 == 0
    rows = group * seq
    in_specs = [pl.BlockSpec((rows, D_MODEL), lambda i: (i, 0)),
                pl.BlockSpec((None, rows, PLE_DIM), lambda i: (layer, i, 0)),
                pl.BlockSpec((group, GLA_HEADS, GLA_DK, GLA_DV), lambda i: (i, 0, 0, 0))]
    in_specs += [_resident_spec(w.shape) for w in weights]
    y, st = pl.pallas_call(
        functools.partial(_gla_sample_kernel, seq=seq),
        grid=(n // group,),
        in_specs=in_specs,
        out_specs=[pl.BlockSpec((rows, D_MODEL), lambda i: (i, 0)),
                   pl.BlockSpec((group, GLA_HEADS, GLA_DK, GLA_DV), lambda i: (i, 0, 0, 0))],
        out_shape=[jax.ShapeDtypeStruct((n * seq, D_MODEL), F32),
                   jax.ShapeDtypeStruct((n, GLA_HEADS, GLA_DK, GLA_DV), F32)],
        scratch_shapes=[pltpu.VMEM((rows, GLA_HV), F32)],
        compiler_params=pltpu.CompilerParams(dimension_semantics=("arbitrary",),
                                             vmem_limit_bytes=VMEM_LIMIT),
        name="gla_sample",
    )(x.reshape(n * seq, D_MODEL), p_all.reshape(-1, n * seq, PLE_DIM), s0, *weights)
    return y.reshape(n, seq, D_MODEL), st


def _resident_spec(shape):
    nd = len(shape)
    return pl.BlockSpec(shape, lambda *_: (0,) * nd, pipeline_mode=pl.Buffered(1))


def _t5_bias(rel_bias, dist):
    max_exact = N_BUCKETS // 2
    n = jnp.maximum(dist, 0)[None]
    nf = jnp.maximum(n, 1).astype(F32)
    steps = jnp.log(nf / max_exact) / math.log(MAX_DISTANCE / max_exact) * (N_BUCKETS - max_exact)
    table = rel_bias.astype(F32)
    per_head = (DIFF_HEADS,) + (1,) * dist.ndim
    out = jnp.broadcast_to(table[N_BUCKETS - 1].reshape(per_head), (DIFF_HEADS,) + dist.shape)
    for b in range(N_BUCKETS - 2, max_exact - 1, -1):
        out = jnp.where(steps < (b + 1 - max_exact), table[b].reshape(per_head), out)
    for b in range(max_exact):
        out = jnp.where(n == b, table[b].reshape(per_head), out)
    return out


def _store_token_head_rows(ref, lead, x):
    tokens = x.shape[0]
    for h in range(DIFF_HEADS):
        ref[(*lead, pl.ds(h, tokens, stride=DIFF_HEADS), slice(None))] = x[:, h * DIFF_VD:(h + 1) * DIFF_VD]


def _load_token_head_rows(ref, lead, tokens):
    return jnp.concatenate(
        [ref[(*lead, pl.ds(h, tokens, stride=DIFF_HEADS), slice(None))] for h in range(DIFF_HEADS)], axis=1)


def _diff_lambda(lam_ref, lam_init):
    lv = lam_ref[...]
    a = jnp.sum(lv[0:1] * lv[1:2], axis=-1, keepdims=True)
    b = jnp.sum(lv[2:3] * lv[3:4], axis=-1, keepdims=True)
    return jnp.exp(a) - jnp.exp(b) + lam_init


def _diff_prompt_kernel(x_ref, p_ref, wq_ref, wk_ref, wv_ref, wg_ref, lam_ref, bias_ref, ng_ref,
                        wout_ref, lng_ref, lnb_ref, wgate_ref, wproj_ref,
                        y_ref, kout_ref, vout_ref,
                        k_scr, vt_scr, q_scr, o_scr, acc_scr, m_scr, alpha_scr, *bufs,
                        lam_init, heads_per_iter):
    i = pl.program_id(1)
    tq = x_ref.shape[1]
    x = x_ref[0]
    xb = x.astype(BF16)
    q = _dot(xb, wq_ref[...]) * (DIFF_DH ** -0.5 * LOG2_E)
    k = _dot(xb, wk_ref[...])
    v = _dot(xb, wv_ref[...])
    _store_token_head_rows(kout_ref, (0,), k)
    _store_token_head_rows(vout_ref, (0,), v)
    first_half = lax.broadcasted_iota(jnp.int32, (DIFF_VD, tq), 0) < DIFF_DH
    row0 = pl.multiple_of(i * tq, tq)
    for h in range(DIFF_HEADS):
        hs = slice(h * DIFF_VD, (h + 1) * DIFF_VD)
        qh_t = q[:, hs].T
        q_scr[h, :, 0:tq] = jnp.where(first_half, qh_t, 0.0).astype(BF16)
        q_scr[h, :, tq:2 * tq] = jnp.where(first_half, 0.0, qh_t).astype(BF16)
        k_scr[h, pl.ds(row0, tq), :] = k[:, hs].astype(BF16)
        vt_scr[h, i, 0:DIFF_VD, :] = v[:, hs].T.astype(BF16)
        vt_scr[h, i, DIFF_VD:, :] = jnp.ones((SUM_ROWS, tq), BF16)
    lam = _diff_lambda(lam_ref, lam_init)

    s_buf = [bufs[2 * u:2 * u + 2] for u in range(heads_per_iter)]
    p_buf = [bufs[2 * heads_per_iter + 2 * u:2 * heads_per_iter + 2 * u + 2] for u in range(heads_per_iter)]

    def block_of(t):
        near = jnp.maximum(i - t, 0)
        if isinstance(t, int):
            return near if t < 2 else jnp.minimum(t - 2, i)
        return jnp.where(t < 2, near, jnp.minimum(t - 2, i))

    def group_body(grp, carry):
        heads = [grp * heads_per_iter + u for u in range(heads_per_iter)]
        m_scr[...] = jnp.full(m_scr.shape, MASK_VALUE, F32)
        acc_scr[...] = jnp.zeros(acc_scr.shape, F32)

        def scores(t, slot):
            rows = pl.ds(pl.multiple_of(block_of(t) * tq, tq), tq)
            for u, h in enumerate(heads):
                s_buf[u][slot][...] = _dot(k_scr[h, rows, :], q_scr[h])

        def softmax(t, slot, bias_slot):
            valid = t <= i
            for u, h in enumerate(heads):
                s = s_buf[u][slot][...]
                if bias_slot is not None:
                    tile = bias_ref[h, bias_slot]
                    s = s + jnp.concatenate([tile, tile], axis=1)
                m_old = m_scr[u]
                m_new = jnp.where(valid, jnp.maximum(m_old, jnp.max(s, axis=0, keepdims=True)), m_old)
                alpha = jnp.exp2(m_old - m_new)
                pr = jnp.exp2(s - jnp.where(valid, m_new, -MASK_VALUE))
                m_scr[u] = m_new
                alpha_scr[u] = alpha
                p_buf[u][slot][...] = pr.astype(BF16)

        def weighted_values(t, slot):
            j = block_of(t)
            for u, h in enumerate(heads):
                acc_scr[u] = acc_scr[u] * alpha_scr[u] + _dot(vt_scr[h, j], p_buf[u][slot][...])

        def step(t, slot, bias_slot):
            weighted_values(t - 1, 1 - slot)
            scores(t + 1, 1 - slot)
            softmax(t, slot, bias_slot)

        scores(0, 0)
        softmax(0, 0, 0)
        scores(1, 1)
        step(1, 1, 1)

        def two_far_steps(r, c):
            step(2 * r, 0, None)
            step(2 * r + 1, 1, None)
            return c

        n_pairs = (i + 2) // 2
        lax.fori_loop(1, n_pairs, two_far_steps, 0)
        weighted_values(2 * n_pairs - 1, 1)

        for u, h in enumerate(heads):
            inv_l = 1.0 / acc_scr[u, DIFF_VD:DIFF_VD + 1, :]
            acc = acc_scr[u, 0:DIFF_VD, :]
            o_t = acc[:, :tq] * inv_l[:, :tq] - lam * (acc[:, tq:] * inv_l[:, tq:])
            o_scr[h] = o_t.T
        return carry

    lax.fori_loop(0, DIFF_HEADS // heads_per_iter, group_body, 0)
    o = jnp.concatenate([o_scr[h] for h in range(DIFF_HEADS)], axis=-1)
    gate = _dot(xb, wg_ref[...])
    on = _head_rmsnorm_gate(o, gate, ng_ref[...], DIFF_HEADS, DIFF_VD, 1.0 - lam_init)
    f = _dot(on.astype(BF16), wout_ref[...])
    y_ref[0] = _residual_update(x, f, p_ref[0], lng_ref[...], lnb_ref[...], wgate_ref[...], wproj_ref[...])


def _diff_weights(w_in, lam_q1, lam_k1, lam_q2, lam_k2, norm_g, w_out, ln_g, ln_b, w_gate, w_proj):
    w = DIFF_WIDTH
    return dict(
        wq=w_in[:, :w].astype(BF16), wk=w_in[:, w:2 * w].astype(BF16),
        wv=w_in[:, 2 * w:3 * w].astype(BF16), wg=w_in[:, 3 * w:].astype(BF16),
        lam=jnp.stack([lam_q1, lam_k1, lam_q2, lam_k2]).astype(F32),
        ng=norm_g.reshape(1, w), wout=w_out.astype(BF16),
        lng=ln_g.reshape(1, D_MODEL), lnb=ln_b.reshape(1, D_MODEL),
        wgate=w_gate.astype(BF16), wproj=w_proj.astype(BF16))


def diff_layer_prompt(x, p_all, layer, dw, rel_bias, lam_init, *, block=256, heads_per_iter=2):
    bsz, t, _ = x.shape
    assert t % block == 0 and block >= MAX_DISTANCE and DIFF_HEADS % heads_per_iter == 0
    kk = jnp.arange(block, dtype=jnp.int32)[:, None]
    qq = jnp.arange(block, dtype=jnp.int32)[None, :]
    dist = jnp.stack([qq - kk, block + qq - kk])
    far = rel_bias.astype(F32)[N_BUCKETS - 1].reshape(DIFF_HEADS, 1, 1, 1)
    bias = jnp.where(dist >= 0, (_t5_bias(rel_bias, dist) - far) * LOG2_E, MASK_VALUE)
    tok = lambda width: pl.BlockSpec((1, block, width), lambda b, i: (b, i, 0))
    kv_spec = pl.BlockSpec((1, block * DIFF_HEADS, DIFF_VD), lambda b, i: (b, i, 0))
    consts = [dw["wq"], dw["wk"], dw["wv"], dw["wg"], dw["lam"], bias, dw["ng"], dw["wout"],
              dw["lng"], dw["lnb"], dw["wgate"], dw["wproj"]]
    return pl.pallas_call(
        functools.partial(_diff_prompt_kernel, lam_init=lam_init, heads_per_iter=heads_per_iter),
        grid=(bsz, t // block),
        in_specs=[tok(D_MODEL), pl.BlockSpec((None, 1, block, PLE_DIM), lambda b, i: (layer, b, i, 0))]
                 + [_resident_spec(c.shape) for c in consts],
        out_specs=[tok(D_MODEL), kv_spec, kv_spec],
        out_shape=[jax.ShapeDtypeStruct((bsz, t, D_MODEL), F32),
                   jax.ShapeDtypeStruct((bsz, t * DIFF_HEADS, DIFF_VD), F32),
                   jax.ShapeDtypeStruct((bsz, t * DIFF_HEADS, DIFF_VD), F32)],
        scratch_shapes=[pltpu.VMEM((DIFF_HEADS, t, DIFF_VD), BF16),
                        pltpu.VMEM((DIFF_HEADS, t // block, DIFF_VD + SUM_ROWS, block), BF16),
                        pltpu.VMEM((DIFF_HEADS, DIFF_VD, 2 * block), BF16),
                        pltpu.VMEM((DIFF_HEADS, block, DIFF_VD), F32),
                        pltpu.VMEM((heads_per_iter, DIFF_VD + SUM_ROWS, 2 * block), F32),
                        pltpu.VMEM((heads_per_iter, 1, 2 * block), F32),
                        pltpu.VMEM((heads_per_iter, 1, 2 * block), F32),
                        *[pltpu.VMEM((block, 2 * block), F32)] * (2 * heads_per_iter),
                        *[pltpu.VMEM((block, 2 * block), BF16)] * (2 * heads_per_iter)],
        compiler_params=pltpu.CompilerParams(dimension_semantics=("arbitrary", "arbitrary"),
                                             vmem_limit_bytes=VMEM_LIMIT),
        name="diff_prompt",
    )(x, p_all, *consts)


def _diff_sample_project_kernel(x_ref, wq_ref, wk_ref, wv_ref, qt_ref, k_ref, v_ref):
    xb = x_ref[...].astype(BF16)
    qt_ref[...] = (_dot(xb, wq_ref[...]) * (DIFF_DH ** -0.5)).T.astype(BF16)
    _store_token_head_rows(k_ref, (), _dot(xb, wk_ref[...]))
    _store_token_head_rows(v_ref, (), _dot(xb, wv_ref[...]))


def _row_to_col(row, n):
    eye = lax.broadcasted_iota(jnp.int32, (n, n), 0) == lax.broadcasted_iota(jnp.int32, (n, n), 1)
    return jnp.sum(jnp.where(eye, row, 0.0), axis=1, keepdims=True)


def _paged_attn_kernel(pt_ref, qt_ref, kn_ref, vn_ref, lam_ref, bias_ref, bnew_ref, own_ref, ck_hbm, cv_hbm, o_ref,
                       k_ring, v_ring, sems, wq_scr, acc_scr, m_scr, l_scr, *, pages, seq, lam_init):
    b = pl.program_id(0)
    g = pl.program_id(1)
    n_steps = pl.num_programs(0) * pl.num_programs(1)
    step = b * pl.num_programs(1) + g
    ncol = 2 * DIFF_HEADS * seq

    def page_copies(s, slot):
        copies = []
        for i in range(pages):
            page = pt_ref[s * pages + i]
            copies.append(pltpu.make_async_copy(ck_hbm.at[page], k_ring.at[slot, i], sems.at[0, slot]))
            copies.append(pltpu.make_async_copy(cv_hbm.at[page], v_ring.at[slot, i], sems.at[1, slot]))
        return copies

    @pl.when(step == 0)
    def _():
        for ahead in range(PAGE_RING - 1):
            @pl.when(ahead < n_steps)
            def _():
                for c in page_copies(ahead, ahead):
                    c.start()

    nxt = step + PAGE_RING - 1

    @pl.when(nxt < n_steps)
    def _():
        for c in page_copies(nxt, lax.rem(nxt, PAGE_RING)):
            c.start()

    slot = lax.rem(step, PAGE_RING)
    for c in page_copies(step, slot):
        c.wait()
    k_pages = [k_ring.at[slot, i] for i in range(pages)]
    v_pages = [v_ring.at[slot, i] for i in range(pages)]

    @pl.when(g == 0)
    def _():
        local = (b % (LANES // seq)) * seq
        src = lax.broadcasted_iota(jnp.int32, (LANES, ncol), 0)
        col = lax.broadcasted_iota(jnp.int32, (LANES, ncol), 1)
        pick = jnp.where(src == local + col % seq, 1.0, 0.0).astype(BF16)
        rep = _dot(qt_ref[...], pick)
        wq_scr[...] = (rep * own_ref[...]).astype(BF16)
        m_scr[...] = jnp.full(m_scr.shape, MASK_VALUE, F32)
        l_scr[...] = jnp.zeros(l_scr.shape, F32)
        acc_scr[...] = jnp.zeros(acc_scr.shape, F32)

    def flash_step(scores, values):
        m_old = m_scr[...]
        m_new = m_old
        for s in scores:
            m_new = jnp.maximum(m_new, jnp.max(s, axis=0, keepdims=True))
        alpha = jnp.exp(m_old - m_new)
        l_new = alpha * l_scr[...]
        probs = []
        for s in scores:
            pr = jnp.exp(s - m_new)
            l_new = l_new + jnp.sum(pr, axis=0, keepdims=True)
            probs.append(pr.astype(BF16))
        pv = None
        for a in range(0, len(probs), 2):
            t = _dot_ta(jnp.concatenate(probs[a:a + 2], axis=0),
                        jnp.concatenate(values[a:a + 2], axis=0))
            pv = t if pv is None else pv + t
        acc_scr[...] = acc_scr[...] * _row_to_col(alpha, ncol) + pv
        l_scr[...] = l_new
        m_scr[...] = m_new

    def load_page(ref):
        return _load_token_head_rows(ref, (), PAGE_SIZE).astype(BF16)

    wq = wq_scr[...]
    scores, values = [], []
    for i in range(pages):
        s = _dot(load_page(k_pages[i]), wq)
        scores.append(s + bias_ref[0] if i == pages - 1 else s)
        values.append(load_page(v_pages[i]))
    flash_step(scores, values)

    @pl.when(g == pl.num_programs(1) - 1)
    def _():
        pad = jnp.zeros((16 - seq, DIFF_WIDTH), F32)
        kn = jnp.concatenate([_load_token_head_rows(kn_ref, (0,), seq), pad], axis=0).astype(BF16)
        vn = jnp.concatenate([_load_token_head_rows(vn_ref, (0,), seq), pad], axis=0).astype(BF16)
        flash_step([_dot(kn, wq) + bnew_ref[...]], [vn])
        lam = _diff_lambda(lam_ref, lam_init)
        inv_l = _row_to_col(1.0 / l_scr[...], ncol)
        half_rows = DIFF_HEADS * seq
        for h in range(DIFF_HEADS):
            cs = slice(h * DIFF_VD, (h + 1) * DIFF_VD)
            r1 = slice(h * seq, (h + 1) * seq)
            r2 = slice(half_rows + h * seq, half_rows + (h + 1) * seq)
            o_ref[0, :, cs] = acc_scr[r1, cs] * inv_l[r1] - lam * (acc_scr[r2, cs] * inv_l[r2])


def _diff_sample_out_kernel(x_ref, o_ref, p_ref, wg_ref, ng_ref, wout_ref, lng_ref, lnb_ref, wgate_ref, wproj_ref,
                            y_ref, *, lam_init):
    x = x_ref[...]
    gate = _dot(x.astype(BF16), wg_ref[...])
    on = _head_rmsnorm_gate(o_ref[...], gate, ng_ref[...], DIFF_HEADS, DIFF_VD, 1.0 - lam_init)
    f = _dot(on.astype(BF16), wout_ref[...])
    y_ref[...] = _residual_update(x, f, p_ref[...], lng_ref[...], lnb_ref[...], wgate_ref[...], wproj_ref[...])


def diff_layer_sample(x, p_all, layer, cache_k, cache_v, page_table, dw, rel_bias, lam_init, *, pages=8, block=256):
    n, seq, _ = x.shape
    n_pages = page_table.shape[1]
    rows = n * seq
    assert n_pages % pages == 0 and rows % block == 0 and LANES % seq == 0 and seq <= 16
    assert PAGE_SIZE >= MAX_DISTANCE
    x2 = x.reshape(rows, D_MODEL)
    row_spec = lambda width: pl.BlockSpec((block, width), lambda i: (i, 0))
    qt, kn, vn = pl.pallas_call(
        _diff_sample_project_kernel,
        grid=(rows // block,),
        in_specs=[row_spec(D_MODEL)] + [_const_spec((D_MODEL, DIFF_WIDTH))] * 3,
        out_specs=[pl.BlockSpec((DIFF_WIDTH, block), lambda i: (0, i)),
                   pl.BlockSpec((block * DIFF_HEADS, DIFF_VD), lambda i: (i, 0)),
                   pl.BlockSpec((block * DIFF_HEADS, DIFF_VD), lambda i: (i, 0))],
        out_shape=[jax.ShapeDtypeStruct((DIFF_WIDTH, rows), BF16),
                   jax.ShapeDtypeStruct((rows * DIFF_HEADS, DIFF_VD), F32),
                   jax.ShapeDtypeStruct((rows * DIFF_HEADS, DIFF_VD), F32)],
        compiler_params=pltpu.CompilerParams(dimension_semantics=("arbitrary",), vmem_limit_bytes=VMEM_LIMIT),
        name="diff_sample_project",
    )(x2, dw["wq"], dw["wk"], dw["wv"])

    ncol = 2 * DIFF_HEADS * seq
    col = jnp.arange(ncol, dtype=jnp.int32)[None, :]
    col_h = (col % (DIFF_HEADS * seq)) // seq
    col_t = col % seq
    kk = jnp.arange(PAGE_SIZE, dtype=jnp.int32)[:, None]
    tk = jnp.arange(16, dtype=jnp.int32)[:, None]
    dist_last = PAGE_SIZE + col_t - kk
    dist_new = col_t - tk
    far = rel_bias.astype(F32)[N_BUCKETS - 1].reshape(DIFF_HEADS, 1, 1)

    def own_head(per_head):
        return sum(jnp.where(col_h == h, per_head[h], 0.0) for h in range(DIFF_HEADS))

    bias_last = own_head(_t5_bias(rel_bias, dist_last) - far)
    bias_pages = jnp.stack([jnp.zeros_like(bias_last), bias_last])
    bias_new = jnp.where((tk < seq) & (dist_new >= 0), own_head(_t5_bias(rel_bias, dist_new) - far), MASK_VALUE)
    feat = jnp.arange(DIFF_WIDTH, dtype=jnp.int32)[:, None]
    own = ((feat // DIFF_VD == col_h) & ((feat % DIFF_VD) // DIFF_DH == col // (DIFF_HEADS * seq))).astype(F32)

    n_groups = n_pages // pages
    seq_per_blk = LANES // seq
    page_shape = (PAGE_RING, pages, PAGE_SIZE * DIFF_HEADS, DIFF_VD)

    tok_spec = pl.BlockSpec((1, seq, DIFF_WIDTH), lambda b, g, pt: (b, 0, 0))
    new_spec = pl.BlockSpec((1, seq * DIFF_HEADS, DIFF_VD), lambda b, g, pt: (b, 0, 0))
    grid_spec = pltpu.PrefetchScalarGridSpec(
        num_scalar_prefetch=1,
        grid=(n, n_groups),
        in_specs=[pl.BlockSpec((DIFF_WIDTH, LANES), lambda b, g, pt: (0, b // seq_per_blk)),
                  new_spec, new_spec,
                  pl.BlockSpec((4, DIFF_DH), lambda b, g, pt: (0, 0)),
                  pl.BlockSpec((1, PAGE_SIZE, ncol), lambda b, g, pt: ((g + 1) // n_groups, 0, 0)),
                  pl.BlockSpec((16, ncol), lambda b, g, pt: (0, 0)),
                  pl.BlockSpec((DIFF_WIDTH, ncol), lambda b, g, pt: (0, 0)),
                  pl.BlockSpec(memory_space=pl.ANY), pl.BlockSpec(memory_space=pl.ANY)],
        out_specs=tok_spec,
        scratch_shapes=[pltpu.VMEM(page_shape, F32), pltpu.VMEM(page_shape, F32),
                        pltpu.SemaphoreType.DMA((2, PAGE_RING)),
                        pltpu.VMEM((DIFF_WIDTH, ncol), BF16),
                        pltpu.VMEM((ncol, DIFF_WIDTH), F32),
                        pltpu.VMEM((1, ncol), F32),
                        pltpu.VMEM((1, ncol), F32)])
    o = pl.pallas_call(
        functools.partial(_paged_attn_kernel, pages=pages, seq=seq, lam_init=lam_init),
        grid_spec=grid_spec,
        out_shape=jax.ShapeDtypeStruct((n, seq, DIFF_WIDTH), F32),
        compiler_params=pltpu.CompilerParams(dimension_semantics=("arbitrary", "arbitrary"),
                                             vmem_limit_bytes=VMEM_LIMIT),
        name="diff_sample_attn",
    )(page_table.reshape(-1), qt, kn.reshape(n, seq * DIFF_HEADS, DIFF_VD), vn.reshape(n, seq * DIFF_HEADS, DIFF_VD),
      dw["lam"], bias_pages, bias_new, own, cache_k, cache_v)

    consts = [dw["wg"], dw["ng"], dw["wout"], dw["lng"], dw["lnb"], dw["wgate"], dw["wproj"]]
    y = pl.pallas_call(
        functools.partial(_diff_sample_out_kernel, lam_init=lam_init),
        grid=(rows // block,),
        in_specs=[row_spec(D_MODEL), row_spec(DIFF_WIDTH),
                  pl.BlockSpec((None, block, PLE_DIM), lambda i: (layer, i, 0))] + [_const_spec(c.shape) for c in consts],
        out_specs=row_spec(D_MODEL),
        out_shape=jax.ShapeDtypeStruct((rows, D_MODEL), F32),
        compiler_params=pltpu.CompilerParams(dimension_semantics=("arbitrary",), vmem_limit_bytes=VMEM_LIMIT),
        name="diff_sample_out",
    )(x2, o.reshape(rows, DIFF_WIDTH), p_all.reshape(-1, rows, PLE_DIM), *consts)
    return y.reshape(n, seq, D_MODEL), kn, vn


def kernel(x_prompt, x_sample, state_gla, cache_k, cache_v, page_table, p_prompt, p_sample, rel_bias,
           gla_w_in, gla_w_a2, gla_b_a, gla_norm_g, gla_w_out,
           diff_w_in, diff_lam_q1, diff_lam_k1, diff_lam_q2, diff_lam_k2, diff_norm_g, diff_w_out,
           ln_g, ln_b, ple_w_proj, ple_w_gate):
    w0 = _gla_weights(gla_w_in[0], gla_w_a2[0], gla_b_a[0], gla_norm_g[0], gla_w_out[0],
                      ln_g[0], ln_b[0], ple_w_gate[0], ple_w_proj[0])
    xp1, sp = gla_layer_prompt(x_prompt, p_prompt, 0, w0)
    xs1, ss = gla_layer_sample(x_sample, p_sample, 0, state_gla[0], w0)
    lam_init = 0.8 - 0.6 * math.exp(-0.3 * 1)
    dw = _diff_weights(diff_w_in[0], diff_lam_q1[0], diff_lam_k1[0], diff_lam_q2[0], diff_lam_k2[0],
                       diff_norm_g[0], diff_w_out[0], ln_g[1], ln_b[1], ple_w_gate[1], ple_w_proj[1])
    yp, kp, vp = diff_layer_prompt(xp1, p_prompt, 1, dw, rel_bias, lam_init)
    pool = cache_k.shape[1]
    ys, ks, vs = diff_layer_sample(xs1, p_sample, 1,
                                   cache_k[0].reshape(pool, PAGE_SIZE * DIFF_HEADS, DIFF_VD),
                                   cache_v[0].reshape(pool, PAGE_SIZE * DIFF_HEADS, DIFF_VD),
                                   page_table, dw, rel_bias, lam_init)
    bsz, t, _ = x_prompt.shape
    n, seq, _ = x_sample.shape
    heads = (DIFF_HEADS, DIFF_VD)
    return (yp, ys, sp[None], ss[None],
            kp.reshape(1, bsz, t, *heads), vp.reshape(1, bsz, t, *heads),
            ks.reshape(1, n, seq, *heads), vs.reshape(1, n, seq, *heads))
```

```python
import functools
import math

import jax
import jax.numpy as jnp
from jax import lax
from jax.experimental import pallas as pl
from jax.experimental.pallas import tpu as pltpu

F32 = jnp.float32
BF16 = jnp.bfloat16

D_MODEL = 1024
DEPTH = 2
GLA_HEADS = 4
GLA_DK = 128
GLA_DV = 256
GLA_HK = GLA_HEADS * GLA_DK
GLA_HV = GLA_HEADS * GLA_DV
GLA_LOWRANK = 16
GLA_TAU = 16.0
DIFF_HEADS = 8
DIFF_DH = 64
DIFF_VD = 128
DIFF_WIDTH = DIFF_HEADS * DIFF_VD
N_BUCKETS = 32
MAX_DISTANCE = 128
PLE_DIM = 256
PAGE_SIZE = 128
ALPHA = (2 * DEPTH) ** 0.25
EPS = 1e-5

LANES = 128
GLA_CHUNK = 64
VMEM_LIMIT = 56 * 1024 * 1024
MASK_VALUE = -1e30
LOG2_E = math.log2(math.e)
SUM_ROWS = 16
PAGE_RING = 4


def _dot(a, b):
    return jnp.dot(a, b, preferred_element_type=F32)


def _dot_tb(a, b):
    return lax.dot_general(a, b, (((1,), (1,)), ((), ())), preferred_element_type=F32)


def _dot_ta(a, b):
    return lax.dot_general(a, b, (((0,), (0,)), ((), ())), preferred_element_type=F32)


def _split_bf16(x):
    hi = x.astype(BF16)
    mid = (x - hi.astype(F32)).astype(BF16)
    return hi, mid


def _log_sigmoid(z):
    return jnp.minimum(z, 0.0) - jnp.log(1.0 + jnp.exp(-jnp.abs(z)))


def _sigmoid(z):
    return 1.0 / (1.0 + jnp.exp(-z))


def _head_rmsnorm_gate(o, gate, norm_g, n_heads, head_dim, scale):
    parts = []
    for h in range(n_heads):
        oh = o[:, h * head_dim:(h + 1) * head_dim]
        ms = jnp.mean(oh * oh, axis=-1, keepdims=True)
        parts.append(oh * lax.rsqrt(ms + EPS))
    on = jnp.concatenate(parts, axis=-1) * norm_g
    if scale != 1.0:
        on = on * scale
    return on * (gate * _sigmoid(gate))


def _residual_update(x, f, p, ln_g, ln_b, w_gate, w_proj):
    hp = ALPHA * x + f
    mu = jnp.mean(hp, axis=-1, keepdims=True)
    hc = hp - mu
    var = jnp.mean(hc * hc, axis=-1, keepdims=True)
    h = hc * lax.rsqrt(var + EPS) * ln_g + ln_b
    gate = _sigmoid(_dot(h.astype(BF16), w_gate))
    return h + gate * _dot(p.astype(BF16), w_proj)


def _gla_project(xb, wq, wk, wv, wg, wa, wa2, ba):
    q = _dot(xb, wq) * (GLA_DK ** -0.5)
    k = _dot(xb, wk)
    v = _dot(xb, wv)
    g = _dot(xb, wg)
    a_lr = _dot(xb, wa)
    z = _dot(a_lr.astype(BF16), wa2) + ba
    log_a = _log_sigmoid(z) * (1.0 / GLA_TAU)
    return q, k, v, g, log_a


def _chunk_decay_matrices(rows, chunk):
    half = chunk // 2
    i = lax.broadcasted_iota(jnp.int32, (rows, rows), 0)
    j = lax.broadcasted_iota(jnp.int32, (rows, rows), 1)
    same = (i // chunk) == (j // chunk)
    jl = j % chunk
    il = i % chunk
    pos = same & (jl >= half) & (jl <= il)
    neg = same & (jl < half) & (jl > il)
    rel = jnp.where(pos, 1.0, jnp.where(neg, -1.0, 0.0)).astype(BF16)
    n_sel = max(8, 2 * rows // chunk)
    s = lax.broadcasted_iota(jnp.int32, (n_sel, rows), 0)
    t = lax.broadcasted_iota(jnp.int32, (n_sel, rows), 1)
    halves = jnp.where((t // half) == s, 1.0, 0.0).astype(BF16)
    return rel, halves


def _gla_prompt_kernel(x_ref, p_ref, wq_ref, wk_ref, wv_ref, wg_ref, wa_ref, wa2_ref, ba_ref, ng_ref,
                       wout_ref, lng_ref, lnb_ref, wgate_ref, wproj_ref,
                       y_ref, st_ref, s_scr, o_scr, *, chunk):
    blk = pl.program_id(1)
    rows = o_scr.shape[1]

    @pl.when(blk == 0)
    def _():
        s_scr[...] = jnp.zeros_like(s_scr)

    ci = lax.broadcasted_iota(jnp.int32, (chunk, chunk), 0)
    cj = lax.broadcasted_iota(jnp.int32, (chunk, chunk), 1)
    causal = ci >= cj
    rel, halves = _chunk_decay_matrices(rows, chunk)
    for sb in range(x_ref.shape[1] // rows):
        tok = slice(sb * rows, (sb + 1) * rows)
        x = x_ref[0, tok, :]
        xb = x.astype(BF16)
        q, k, v, g, log_a = _gla_project(xb, wq_ref[...], wk_ref[...], wv_ref[...], wg_ref[...],
                                         wa_ref[...], wa2_ref[...], ba_ref[...])
        la_hi, la_mid = _split_bf16(log_a)
        d = _dot(rel, la_hi) + _dot(rel, la_mid)
        hs = _dot(halves, la_hi) + _dot(halves, la_mid)
        ehs = jnp.exp(hs)
        q_dec = (q * jnp.exp(d)).astype(BF16)
        k_inv = (k * jnp.exp(-d)).astype(BF16)
        vb = v.astype(BF16)

        for c in range(rows // chunk):
            r0 = c * chunk
            for h in range(GLA_HEADS):
                ks = slice(h * GLA_DK, (h + 1) * GLA_DK)
                vs = slice(h * GLA_DV, (h + 1) * GLA_DV)
                qd = q_dec[r0:r0 + chunk, ks]
                ki = k_inv[r0:r0 + chunk, ks]
                vh = vb[r0:r0 + chunk, vs]
                e_first = ehs[2 * c:2 * c + 1, ks]
                e_second = ehs[2 * c + 1:2 * c + 2, ks]
                s_mid = s_scr[h] * e_first
                att = jnp.where(causal, _dot_tb(qd, ki), 0.0).astype(BF16)
                o_scr[sb, r0:r0 + chunk, vs] = _dot(att, vh) + _dot_tb(qd, s_mid.astype(BF16))
                s_scr[h] = (s_mid + _dot_ta(vh, ki)) * e_second

        on = _head_rmsnorm_gate(o_scr[sb], g, ng_ref[...], GLA_HEADS, GLA_DV, 1.0)
        f = _dot(on.astype(BF16), wout_ref[...])
        y_ref[0, tok, :] = _residual_update(x, f, p_ref[0, tok, :], lng_ref[...], lnb_ref[...],
                                            wgate_ref[...], wproj_ref[...])

    @pl.when(blk == pl.num_programs(1) - 1)
    def _():
        for h in range(GLA_HEADS):
            st_ref[0, h] = s_scr[h].T


def _const_spec(shape):
    nd = len(shape)
    return pl.BlockSpec(shape, lambda *_: (0,) * nd)


def _gla_weights(w_in, w_a2, b_a, norm_g, w_out, ln_g, ln_b, w_gate, w_proj):
    hk, hv = GLA_HK, GLA_HV
    wq = w_in[:, :hk].astype(BF16)
    wk = w_in[:, hk:2 * hk].astype(BF16)
    wv = w_in[:, 2 * hk:2 * hk + hv].astype(BF16)
    wg = w_in[:, 2 * hk + hv:2 * hk + 2 * hv].astype(BF16)
    wa = jnp.pad(w_in[:, 2 * hk + 2 * hv:], ((0, 0), (0, LANES - GLA_LOWRANK))).astype(BF16)
    wa2 = jnp.pad(w_a2, ((0, LANES - GLA_LOWRANK), (0, 0))).astype(BF16)
    return (wq, wk, wv, wg, wa, wa2, b_a.reshape(1, hk), norm_g.reshape(1, hv),
            w_out.astype(BF16), ln_g.reshape(1, D_MODEL), ln_b.reshape(1, D_MODEL),
            w_gate.astype(BF16), w_proj.astype(BF16))


def gla_layer_prompt(x, p_all, layer, weights, *, block=512, sub_block=256, chunk=GLA_CHUNK):
    bsz, t, _ = x.shape
    assert t % block == 0 and block % sub_block == 0 and sub_block % chunk == 0
    in_specs = [pl.BlockSpec((1, block, D_MODEL), lambda b, i: (b, i, 0)),
                pl.BlockSpec((None, 1, block, PLE_DIM), lambda b, i: (layer, b, i, 0))]
    in_specs += [_const_spec(w.shape) for w in weights]
    return pl.pallas_call(
        functools.partial(_gla_prompt_kernel, chunk=chunk),
        grid=(bsz, t // block),
        in_specs=in_specs,
        out_specs=[pl.BlockSpec((1, block, D_MODEL), lambda b, i: (b, i, 0)),
                   pl.BlockSpec((1, GLA_HEADS, GLA_DK, GLA_DV), lambda b, i: (b, 0, 0, 0))],
        out_shape=[jax.ShapeDtypeStruct((bsz, t, D_MODEL), F32),
                   jax.ShapeDtypeStruct((bsz, GLA_HEADS, GLA_DK, GLA_DV), F32)],
        scratch_shapes=[pltpu.VMEM((GLA_HEADS, GLA_DV, GLA_DK), F32),
                        pltpu.VMEM((block // sub_block, sub_block, GLA_HV), F32)],
        compiler_params=pltpu.CompilerParams(dimension_semantics=("arbitrary", "arbitrary"),
                                             vmem_limit_bytes=VMEM_LIMIT),
        name="gla_prompt",
    )(x, p_all, *weights)


def _gla_sample_kernel(x_ref, p_ref, s0_ref, wq_ref, wk_ref, wv_ref, wg_ref, wa_ref, wa2_ref, ba_ref, ng_ref,
                       wout_ref, lng_ref, lnb_ref, wgate_ref, wproj_ref,
                       y_ref, st_ref, o_scr, *, seq):
    rows = x_ref.shape[0]
    n_seq = rows // seq
    half = seq // 2
    x = x_ref[...]
    xb = x.astype(BF16)
    q, k, v, g, log_a = _gla_project(xb, wq_ref[...], wk_ref[...], wv_ref[...], wg_ref[...],
                                     wa_ref[...], wa2_ref[...], ba_ref[...])
    rel, _ = _chunk_decay_matrices(rows, seq)
    la_hi, la_mid = _split_bf16(log_a)
    d = _dot(rel, la_hi) + _dot(rel, la_mid)
    s_i = lax.broadcasted_iota(jnp.int32, (n_seq, rows), 0)
    t_i = lax.broadcasted_iota(jnp.int32, (n_seq, rows), 1)
    in_seq = (t_i // seq) == s_i
    sel_first = jnp.where(in_seq & ((t_i % seq) < half), 1.0, 0.0).astype(BF16)
    sel_second = jnp.where(in_seq & ((t_i % seq) >= half), 1.0, 0.0).astype(BF16)
    t_c = lax.broadcasted_iota(jnp.int32, (rows, n_seq), 0)
    s_c = lax.broadcasted_iota(jnp.int32, (rows, n_seq), 1)
    sel_tot_t = jnp.where((t_c // seq) == s_c, 1.0, 0.0).astype(BF16)
    e_first = jnp.exp(_dot(sel_first, la_hi) + _dot(sel_first, la_mid))
    e_second = jnp.exp(_dot(sel_second, la_hi) + _dot(sel_second, la_mid))
    e_tot_col = jnp.exp(_dot_ta(la_hi, sel_tot_t) + _dot_ta(la_mid, sel_tot_t))
    q_dec = q * jnp.exp(d)
    k_inv = k * jnp.exp(-d)

    ci = lax.broadcasted_iota(jnp.int32, (seq, seq), 0)
    cj = lax.broadcasted_iota(jnp.int32, (seq, seq), 1)
    causal = ci >= cj
    for c in range(n_seq):
        r0 = c * seq
        for h in range(GLA_HEADS):
            ks = slice(h * GLA_DK, (h + 1) * GLA_DK)
            vs = slice(h * GLA_DV, (h + 1) * GLA_DV)
            qd = q_dec[r0:r0 + seq, ks]
            ki = k_inv[r0:r0 + seq, ks]
            vh = v[r0:r0 + seq, vs].astype(BF16)
            s_old = s0_ref[c, h]
            att = jnp.where(causal, _dot_tb(qd.astype(BF16), ki.astype(BF16)), 0.0).astype(BF16)
            q_mid = (qd * e_first[c:c + 1, ks]).astype(BF16)
            o_scr[r0:r0 + seq, vs] = _dot(att, vh) + _dot(q_mid, s_old.astype(BF16))
            k_end = (ki * e_second[c:c + 1, ks]).astype(BF16)
            st_ref[c, h] = s_old * e_tot_col[h * GLA_DK:(h + 1) * GLA_DK, c:c + 1] + _dot_ta(k_end, vh)

    on = _head_rmsnorm_gate(o_scr[...], g, ng_ref[...], GLA_HEADS, GLA_DV, 1.0)
    f = _dot(on.astype(BF16), wout_ref[...])
    y_ref[...] = _residual_update(x, f, p_ref[...], lng_ref[...], lnb_ref[...], wgate_ref[...], wproj_ref[...])


def gla_layer_sample(x, p_all, layer, s0, weights, *, group=16):
    n, seq, _ = x.shape
    assert n % group == 0 and seq % 2 == 0
    rows = group * seq
    in_specs = [pl.BlockSpec((rows, D_MODEL), lambda i: (i, 0)),
                pl.BlockSpec((None, rows, PLE_DIM), lambda i: (layer, i, 0)),
                pl.BlockSpec((group, GLA_HEADS, GLA_DK, GLA_DV), lambda i: (i, 0, 0, 0))]
    in_specs += [_resident_spec(w.shape) for w in weights]
    y, st = pl.pallas_call(
        functools.partial(_gla_sample_kernel, seq=seq),
        grid=(n // group,),
        in_specs=in_specs,
        out_specs=[pl.BlockSpec((rows, D_MODEL), lambda i: (i, 0)),
                   pl.BlockSpec((group, GLA_HEADS, GLA_DK, GLA_DV), lambda i: (i, 0, 0, 0))],
        out_shape=[jax.ShapeDtypeStruct((n * seq, D_MODEL), F32),
                   jax.ShapeDtypeStruct((n, GLA_HEADS, GLA_DK, GLA_DV), F32)],
        scratch_shapes=[pltpu.VMEM((rows, GLA_HV), F32)],
        compiler_params=pltpu.CompilerParams(dimension_semantics=("arbitrary",),
                                             vmem_limit_bytes=VMEM_LIMIT),
        name="gla_sample",
    )(x.reshape(n * seq, D_MODEL), p_all.reshape(-1, n * seq, PLE_DIM), s0, *weights)
    return y.reshape(n, seq, D_MODEL), st


def _resident_spec(shape):
    nd = len(shape)
    return pl.BlockSpec(shape, lambda *_: (0,) * nd, pipeline_mode=pl.Buffered(1))


def _t5_bias(rel_bias, dist):
    max_exact = N_BUCKETS // 2
    n = jnp.maximum(dist, 0)[None]
    nf = jnp.maximum(n, 1).astype(F32)
    steps = jnp.log(nf / max_exact) / math.log(MAX_DISTANCE / max_exact) * (N_BUCKETS - max_exact)
    table = rel_bias.astype(F32)
    per_head = (DIFF_HEADS,) + (1,) * dist.ndim
    out = jnp.broadcast_to(table[N_BUCKETS - 1].reshape(per_head), (DIFF_HEADS,) + dist.shape)
    for b in range(N_BUCKETS - 2, max_exact - 1, -1):
        out = jnp.where(steps < (b + 1 - max_exact), table[b].reshape(per_head), out)
    for b in range(max_exact):
        out = jnp.where(n == b, table[b].reshape(per_head), out)
    return out


def _store_token_head_rows(ref, lead, x):
    tokens = x.shape[0]
    for h in range(DIFF_HEADS):
        ref[(*lead, pl.ds(h, tokens, stride=DIFF_HEADS), slice(None))] = x[:, h * DIFF_VD:(h + 1) * DIFF_VD]


def _load_token_head_rows(ref, lead, tokens):
    return jnp.concatenate(
        [ref[(*lead, pl.ds(h, tokens, stride=DIFF_HEADS), slice(None))] for h in range(DIFF_HEADS)], axis=1)


def _diff_lambda(lam_ref, lam_init):
    lv = lam_ref[...]
    a = jnp.sum(lv[0:1] * lv[1:2], axis=-1, keepdims=True)
    b = jnp.sum(lv[2:3] * lv[3:4], axis=-1, keepdims=True)
    return jnp.exp(a) - jnp.exp(b) + lam_init


def _diff_prompt_kernel(x_ref, p_ref, wq_ref, wk_ref, wv_ref, wg_ref, lam_ref, bias_ref, ng_ref,
                        wout_ref, lng_ref, lnb_ref, wgate_ref, wproj_ref,
                        y_ref, kout_ref, vout_ref,
                        k_scr, vt_scr, q_scr, o_scr, alpha_scr, *bufs,
                        lam_init, heads_per_iter):
    i = pl.program_id(1)
    tq = x_ref.shape[1]
    x = x_ref[0]
    xb = x.astype(BF16)
    q = _dot(xb, wq_ref[...]) * (DIFF_DH ** -0.5 * LOG2_E)
    k = _dot(xb, wk_ref[...])
    v = _dot(xb, wv_ref[...])
    _store_token_head_rows(kout_ref, (0,), k)
    _store_token_head_rows(vout_ref, (0,), v)
    first_half = lax.broadcasted_iota(jnp.int32, (DIFF_VD, tq), 0) < DIFF_DH
    row0 = pl.multiple_of(i * tq, tq)
    for h in range(DIFF_HEADS):
        hs = slice(h * DIFF_VD, (h + 1) * DIFF_VD)
        qh_t = q[:, hs].T
        q_scr[h, :, 0:tq] = jnp.where(first_half, qh_t, 0.0).astype(BF16)
        q_scr[h, :, tq:2 * tq] = jnp.where(first_half, 0.0, qh_t).astype(BF16)
        k_scr[h, pl.ds(row0, tq), :] = k[:, hs].astype(BF16)
        vt_scr[h, i, 0:DIFF_VD, :] = v[:, hs].T.astype(BF16)
        vt_scr[h, i, DIFF_VD:, :] = jnp.ones((SUM_ROWS, tq), BF16)
    lam = _diff_lambda(lam_ref, lam_init)

    hpi = heads_per_iter
    n_groups = DIFF_HEADS // hpi
    acc_scr = bufs[0:hpi]
    m_scr = bufs[hpi:2 * hpi]
    s_buf = [bufs[2 * hpi + 2 * u:2 * hpi + 2 * u + 2] for u in range(hpi)]
    p_buf = [bufs[4 * hpi + 2 * u:4 * hpi + 2 * u + 2] for u in range(hpi)]

    n_pairs = (i + 2) // 2
    steps_per_group = 2 * n_pairs

    def block_of(t):
        near = jnp.maximum(i - t, 0)
        if isinstance(t, int):
            return near if t < 2 else jnp.minimum(t - 2, i)
        return jnp.where(t < 2, near, jnp.minimum(t - 2, i))

    def scores(g, t, slot):
        rows = pl.ds(pl.multiple_of(block_of(t) * tq, tq), tq)
        for u in range(hpi):
            h = g * hpi + u
            s_buf[u][slot][...] = _dot(k_scr[h, rows, :], q_scr[h])

    def softmax(g, t, slot, bias_slot):
        valid = t <= i
        for u in range(hpi):
            s = s_buf[u][slot][...]
            if bias_slot is not None:
                tile = bias_ref[g * hpi + u, bias_slot]
                s = s + jnp.concatenate([tile, tile], axis=1)
            m_old = m_scr[u][g]
            m_new = jnp.where(valid, jnp.maximum(m_old, jnp.max(s, axis=0, keepdims=True)), m_old)
            alpha = jnp.exp2(m_old - m_new)
            pr = jnp.exp2(s - jnp.where(valid, m_new, -MASK_VALUE))
            m_scr[u][g] = m_new
            alpha_scr[u] = alpha
            p_buf[u][slot][...] = pr.astype(BF16)

    def weighted_values(g, t, slot):
        j = block_of(t)
        for u in range(hpi):
            acc_scr[u][g] = (acc_scr[u][g] * alpha_scr[u]
                             + _dot(vt_scr[g * hpi + u, j], p_buf[u][slot][...]))

    def pair(g, r, biased):
        first, last = r == 0, r == n_pairs - 1
        weighted_values(jnp.where(first, jnp.maximum(g - 1, 0), g),
                        jnp.where(first, steps_per_group - 1, 2 * r - 1), 1)
        scores(g, 2 * r + 1, 1)
        softmax(g, 2 * r, 0, 0 if biased else None)
        weighted_values(g, 2 * r, 0)
        scores(jnp.where(last, jnp.minimum(g + 1, n_groups - 1), g), jnp.where(last, 0, 2 * r + 2), 0)
        softmax(g, 2 * r + 1, 1, 1 if biased else None)

    for u in range(hpi):
        m_scr[u][...] = jnp.full(m_scr[u].shape, MASK_VALUE, F32)
        acc_scr[u][...] = jnp.zeros(acc_scr[u].shape, F32)
        p_buf[u][1][...] = jnp.zeros(p_buf[u][1].shape, BF16)
    alpha_scr[...] = jnp.ones(alpha_scr.shape, F32)
    scores(0, 0, 0)

    def pair_step(n, carry):
        g = n // n_pairs
        r = n - g * n_pairs

        @pl.when(r == 0)
        def _():
            pair(g, r, True)

        @pl.when(r > 0)
        def _():
            pair(g, r, False)

        return carry

    lax.fori_loop(0, n_groups * n_pairs, pair_step, 0)
    weighted_values(n_groups - 1, steps_per_group - 1, 1)

    for h in range(DIFF_HEADS):
        g, u = divmod(h, hpi)
        inv_l = 1.0 / acc_scr[u][g, DIFF_VD:DIFF_VD + 1, :]
        acc = acc_scr[u][g, 0:DIFF_VD, :]
        o_t = acc[:, :tq] * inv_l[:, :tq] - lam * (acc[:, tq:] * inv_l[:, tq:])
        o_scr[h] = o_t.T
    o = jnp.concatenate([o_scr[h] for h in range(DIFF_HEADS)], axis=-1)
    gate = _dot(xb, wg_ref[...])
    on = _head_rmsnorm_gate(o, gate, ng_ref[...], DIFF_HEADS, DIFF_VD, 1.0 - lam_init)
    f = _dot(on.astype(BF16), wout_ref[...])
    y_ref[0] = _residual_update(x, f, p_ref[0], lng_ref[...], lnb_ref[...], wgate_ref[...], wproj_ref[...])


def _diff_weights(w_in, lam_q1, lam_k1, lam_q2, lam_k2, norm_g, w_out, ln_g, ln_b, w_gate, w_proj):
    w = DIFF_WIDTH
    return dict(
        wq=w_in[:, :w].astype(BF16), wk=w_in[:, w:2 * w].astype(BF16),
        wv=w_in[:, 2 * w:3 * w].astype(BF16), wg=w_in[:, 3 * w:].astype(BF16),
        lam=jnp.stack([lam_q1, lam_k1, lam_q2, lam_k2]).astype(F32),
        ng=norm_g.reshape(1, w), wout=w_out.astype(BF16),
        lng=ln_g.reshape(1, D_MODEL), lnb=ln_b.reshape(1, D_MODEL),
        wgate=w_gate.astype(BF16), wproj=w_proj.astype(BF16))


def diff_layer_prompt(x, p_all, layer, dw, rel_bias, lam_init, *, block=256, heads_per_iter=2):
    bsz, t, _ = x.shape
    assert t % block == 0 and block >= MAX_DISTANCE and DIFF_HEADS % heads_per_iter == 0
    n_groups = DIFF_HEADS // heads_per_iter
    kk = jnp.arange(block, dtype=jnp.int32)[:, None]
    qq = jnp.arange(block, dtype=jnp.int32)[None, :]
    dist = jnp.stack([qq - kk, block + qq - kk])
    far = rel_bias.astype(F32)[N_BUCKETS - 1].reshape(DIFF_HEADS, 1, 1, 1)
    bias = jnp.where(dist >= 0, (_t5_bias(rel_bias, dist) - far) * LOG2_E, MASK_VALUE)
    tok = lambda width: pl.BlockSpec((1, block, width), lambda b, i: (b, i, 0))
    kv_spec = pl.BlockSpec((1, block * DIFF_HEADS, DIFF_VD), lambda b, i: (b, i, 0))
    consts = [dw["wq"], dw["wk"], dw["wv"], dw["wg"], dw["lam"], bias, dw["ng"], dw["wout"],
              dw["lng"], dw["lnb"], dw["wgate"], dw["wproj"]]
    return pl.pallas_call(
        functools.partial(_diff_prompt_kernel, lam_init=lam_init, heads_per_iter=heads_per_iter),
        grid=(bsz, t // block),
        in_specs=[tok(D_MODEL), pl.BlockSpec((None, 1, block, PLE_DIM), lambda b, i: (layer, b, i, 0))]
                 + [_resident_spec(c.shape) for c in consts],
        out_specs=[tok(D_MODEL), kv_spec, kv_spec],
        out_shape=[jax.ShapeDtypeStruct((bsz, t, D_MODEL), F32),
                   jax.ShapeDtypeStruct((bsz, t * DIFF_HEADS, DIFF_VD), F32),
                   jax.ShapeDtypeStruct((bsz, t * DIFF_HEADS, DIFF_VD), F32)],
        scratch_shapes=[pltpu.VMEM((DIFF_HEADS, t, DIFF_VD), BF16),
                        pltpu.VMEM((DIFF_HEADS, t // block, DIFF_VD + SUM_ROWS, block), BF16),
                        pltpu.VMEM((DIFF_HEADS, DIFF_VD, 2 * block), BF16),
                        pltpu.VMEM((DIFF_HEADS, block, DIFF_VD), F32),
                        pltpu.VMEM((heads_per_iter, 1, 2 * block), F32),
                        *[pltpu.VMEM((n_groups, DIFF_VD + SUM_ROWS, 2 * block), F32)] * heads_per_iter,
                        *[pltpu.VMEM((n_groups, 1, 2 * block), F32)] * heads_per_iter,
                        *[pltpu.VMEM((block, 2 * block), F32)] * (2 * heads_per_iter),
                        *[pltpu.VMEM((block, 2 * block), BF16)] * (2 * heads_per_iter)],
        compiler_params=pltpu.CompilerParams(dimension_semantics=("arbitrary", "arbitrary"),
                                             vmem_limit_bytes=VMEM_LIMIT),
        name="diff_prompt",
    )(x, p_all, *consts)


def _diff_sample_project_kernel(x_ref, wq_ref, wk_ref, wv_ref, qt_ref, k_ref, v_ref):
    xb = x_ref[...].astype(BF16)
    qt_ref[...] = (_dot(xb, wq_ref[...]) * (DIFF_DH ** -0.5)).T.astype(BF16)
    _store_token_head_rows(k_ref, (), _dot(xb, wk_ref[...]))
    _store_token_head_rows(v_ref, (), _dot(xb, wv_ref[...]))


def _row_to_col(row, n):
    eye = lax.broadcasted_iota(jnp.int32, (n, n), 0) == lax.broadcasted_iota(jnp.int32, (n, n), 1)
    return jnp.sum(jnp.where(eye, row, 0.0), axis=1, keepdims=True)


def _paged_attn_kernel(pt_ref, qt_ref, kn_ref, vn_ref, lam_ref, bias_ref, bnew_ref, own_ref, ck_hbm, cv_hbm, o_ref,
                       k_ring, v_ring, sems, wq_scr, acc_scr, m_scr, l_scr, *, pages, seq, lam_init):
    b = pl.program_id(0)
    g = pl.program_id(1)
    n_steps = pl.num_programs(0) * pl.num_programs(1)
    step = b * pl.num_programs(1) + g
    ncol = 2 * DIFF_HEADS * seq

    def page_copies(s, slot):
        copies = []
        for i in range(pages):
            page = pt_ref[s * pages + i]
            copies.append(pltpu.make_async_copy(ck_hbm.at[page], k_ring.at[slot, i], sems.at[0, slot]))
            copies.append(pltpu.make_async_copy(cv_hbm.at[page], v_ring.at[slot, i], sems.at[1, slot]))
        return copies

    @pl.when(step == 0)
    def _():
        for ahead in range(PAGE_RING - 1):
            @pl.when(ahead < n_steps)
            def _():
                for c in page_copies(ahead, ahead):
                    c.start()

    nxt = step + PAGE_RING - 1

    @pl.when(nxt < n_steps)
    def _():
        for c in page_copies(nxt, lax.rem(nxt, PAGE_RING)):
            c.start()

    slot = lax.rem(step, PAGE_RING)
    for c in page_copies(step, slot):
        c.wait()
    k_pages = [k_ring.at[slot, i] for i in range(pages)]
    v_pages = [v_ring.at[slot, i] for i in range(pages)]

    @pl.when(g == 0)
    def _():
        local = (b % (LANES // seq)) * seq
        src = lax.broadcasted_iota(jnp.int32, (LANES, ncol), 0)
        col = lax.broadcasted_iota(jnp.int32, (LANES, ncol), 1)
        pick = jnp.where(src == local + col % seq, 1.0, 0.0).astype(BF16)
        rep = _dot(qt_ref[...], pick)
        wq_scr[...] = (rep * own_ref[...]).astype(BF16)
        m_scr[...] = jnp.full(m_scr.shape, MASK_VALUE, F32)
        l_scr[...] = jnp.zeros(l_scr.shape, F32)
        acc_scr[...] = jnp.zeros(acc_scr.shape, F32)

    def flash_step(scores, values):
        m_old = m_scr[...]
        m_new = m_old
        for s in scores:
            m_new = jnp.maximum(m_new, jnp.max(s, axis=0, keepdims=True))
        alpha = jnp.exp(m_old - m_new)
        l_new = alpha * l_scr[...]
        probs = []
        for s in scores:
            pr = jnp.exp(s - m_new)
            l_new = l_new + jnp.sum(pr, axis=0, keepdims=True)
            probs.append(pr.astype(BF16))
        pv = None
        for a in range(0, len(probs), 2):
            t = _dot_ta(jnp.concatenate(probs[a:a + 2], axis=0),
                        jnp.concatenate(values[a:a + 2], axis=0))
            pv = t if pv is None else pv + t
        acc_scr[...] = acc_scr[...] * _row_to_col(alpha, ncol) + pv
        l_scr[...] = l_new
        m_scr[...] = m_new

    def load_page(ref):
        return _load_token_head_rows(ref, (), PAGE_SIZE).astype(BF16)

    wq = wq_scr[...]
    scores, values = [], []
    for i in range(pages):
        s = _dot(load_page(k_pages[i]), wq)
        scores.append(s + bias_ref[0] if i == pages - 1 else s)
        values.append(load_page(v_pages[i]))
    flash_step(scores, values)

    @pl.when(g == pl.num_programs(1) - 1)
    def _():
        pad = jnp.zeros((16 - seq, DIFF_WIDTH), F32)
        kn = jnp.concatenate([_load_token_head_rows(kn_ref, (0,), seq), pad], axis=0).astype(BF16)
        vn = jnp.concatenate([_load_token_head_rows(vn_ref, (0,), seq), pad], axis=0).astype(BF16)
        flash_step([_dot(kn, wq) + bnew_ref[...]], [vn])
        lam = _diff_lambda(lam_ref, lam_init)
        inv_l = _row_to_col(1.0 / l_scr[...], ncol)
        half_rows = DIFF_HEADS * seq
        for h in range(DIFF_HEADS):
            cs = slice(h * DIFF_VD, (h + 1) * DIFF_VD)
            r1 = slice(h * seq, (h + 1) * seq)
            r2 = slice(half_rows + h * seq, half_rows + (h + 1) * seq)
            o_ref[0, :, cs] = acc_scr[r1, cs] * inv_l[r1] - lam * (acc_scr[r2, cs] * inv_l[r2])


def _diff_sample_out_kernel(x_ref, o_ref, p_ref, wg_ref, ng_ref, wout_ref, lng_ref, lnb_ref, wgate_ref, wproj_ref,
                            y_ref, *, lam_init):
    x = x_ref[...]
    gate = _dot(x.astype(BF16), wg_ref[...])
    on = _head_rmsnorm_gate(o_ref[...], gate, ng_ref[...], DIFF_HEADS, DIFF_VD, 1.0 - lam_init)
    f = _dot(on.astype(BF16), wout_ref[...])
    y_ref[...] = _residual_update(x, f, p_ref[...], lng_ref[...], lnb_ref[...], wgate_ref[...], wproj_ref[...])


def diff_layer_sample(x, p_all, layer, cache_k, cache_v, page_table, dw, rel_bias, lam_init, *, pages=8, block=256):
    n, seq, _ = x.shape
    n_pages = page_table.shape[1]
    rows = n * seq
    assert n_pages % pages == 0 and rows % block == 0 and LANES % seq == 0 and seq <= 16
    assert PAGE_SIZE >= MAX_DISTANCE
    x2 = x.reshape(rows, D_MODEL)
    row_spec = lambda width: pl.BlockSpec((block, width), lambda i: (i, 0))
    qt, kn, vn = pl.pallas_call(
        _diff_sample_project_kernel,
        grid=(rows // block,),
        in_specs=[row_spec(D_MODEL)] + [_const_spec((D_MODEL, DIFF_WIDTH))] * 3,
        out_specs=[pl.BlockSpec((DIFF_WIDTH, block), lambda i: (0, i)),
                   pl.BlockSpec((block * DIFF_HEADS, DIFF_VD), lambda i: (i, 0)),
                   pl.BlockSpec((block * DIFF_HEADS, DIFF_VD), lambda i: (i, 0))],
        out_shape=[jax.ShapeDtypeStruct((DIFF_WIDTH, rows), BF16),
                   jax.ShapeDtypeStruct((rows * DIFF_HEADS, DIFF_VD), F32),
                   jax.ShapeDtypeStruct((rows * DIFF_HEADS, DIFF_VD), F32)],
        compiler_params=pltpu.CompilerParams(dimension_semantics=("arbitrary",), vmem_limit_bytes=VMEM_LIMIT),
        name="diff_sample_project",
    )(x2, dw["wq"], dw["wk"], dw["wv"])

    ncol = 2 * DIFF_HEADS * seq
    col = jnp.arange(ncol, dtype=jnp.int32)[None, :]
    col_h = (col % (DIFF_HEADS * seq)) // seq
    col_t = col % seq
    kk = jnp.arange(PAGE_SIZE, dtype=jnp.int32)[:, None]
    tk = jnp.arange(16, dtype=jnp.int32)[:, None]
    dist_last = PAGE_SIZE + col_t - kk
    dist_new = col_t - tk
    far = rel_bias.astype(F32)[N_BUCKETS - 1].reshape(DIFF_HEADS, 1, 1)

    def own_head(per_head):
        return sum(jnp.where(col_h == h, per_head[h], 0.0) for h in range(DIFF_HEADS))

    bias_last = own_head(_t5_bias(rel_bias, dist_last) - far)
    bias_pages = jnp.stack([jnp.zeros_like(bias_last), bias_last])
    bias_new = jnp.where((tk < seq) & (dist_new >= 0), own_head(_t5_bias(rel_bias, dist_new) - far), MASK_VALUE)
    feat = jnp.arange(DIFF_WIDTH, dtype=jnp.int32)[:, None]
    own = ((feat // DIFF_VD == col_h) & ((feat % DIFF_VD) // DIFF_DH == col // (DIFF_HEADS * seq))).astype(F32)

    n_groups = n_pages // pages
    seq_per_blk = LANES // seq
    page_shape = (PAGE_RING, pages, PAGE_SIZE * DIFF_HEADS, DIFF_VD)

    tok_spec = pl.BlockSpec((1, seq, DIFF_WIDTH), lambda b, g, pt: (b, 0, 0))
    new_spec = pl.BlockSpec((1, seq * DIFF_HEADS, DIFF_VD), lambda b, g, pt: (b, 0, 0))
    grid_spec = pltpu.PrefetchScalarGridSpec(
        num_scalar_prefetch=1,
        grid=(n, n_groups),
        in_specs=[pl.BlockSpec((DIFF_WIDTH, LANES), lambda b, g, pt: (0, b // seq_per_blk)),
                  new_spec, new_spec,
                  pl.BlockSpec((4, DIFF_DH), lambda b, g, pt: (0, 0)),
                  pl.BlockSpec((1, PAGE_SIZE, ncol), lambda b, g, pt: ((g + 1) // n_groups, 0, 0)),
                  pl.BlockSpec((16, ncol), lambda b, g, pt: (0, 0)),
                  pl.BlockSpec((DIFF_WIDTH, ncol), lambda b, g, pt: (0, 0)),
                  pl.BlockSpec(memory_space=pl.ANY), pl.BlockSpec(memory_space=pl.ANY)],
        out_specs=tok_spec,
        scratch_shapes=[pltpu.VMEM(page_shape, F32), pltpu.VMEM(page_shape, F32),
                        pltpu.SemaphoreType.DMA((2, PAGE_RING)),
                        pltpu.VMEM((DIFF_WIDTH, ncol), BF16),
                        pltpu.VMEM((ncol, DIFF_WIDTH), F32),
                        pltpu.VMEM((1, ncol), F32),
                        pltpu.VMEM((1, ncol), F32)])
    o = pl.pallas_call(
        functools.partial(_paged_attn_kernel, pages=pages, seq=seq, lam_init=lam_init),
        grid_spec=grid_spec,
        out_shape=jax.ShapeDtypeStruct((n, seq, DIFF_WIDTH), F32),
        compiler_params=pltpu.CompilerParams(dimension_semantics=("arbitrary", "arbitrary"),
                                             vmem_limit_bytes=VMEM_LIMIT),
        name="diff_sample_attn",
    )(page_table.reshape(-1), qt, kn.reshape(n, seq * DIFF_HEADS, DIFF_VD), vn.reshape(n, seq * DIFF_HEADS, DIFF_VD),
      dw["lam"], bias_pages, bias_new, own, cache_k, cache_v)

    consts = [dw["wg"], dw["ng"], dw["wout"], dw["lng"], dw["lnb"], dw["wgate"], dw["wproj"]]
    y = pl.pallas_call(
        functools.partial(_diff_sample_out_kernel, lam_init=lam_init),
        grid=(rows // block,),
        in_specs=[row_spec(D_MODEL), row_spec(DIFF_WIDTH),
                  pl.BlockSpec((None, block, PLE_DIM), lambda i: (layer, i, 0))] + [_const_spec(c.shape) for c in consts],
        out_specs=row_spec(D_MODEL),
        out_shape=jax.ShapeDtypeStruct((rows, D_MODEL), F32),
        compiler_params=pltpu.CompilerParams(dimension_semantics=("arbitrary",), vmem_limit_bytes=VMEM_LIMIT),
        name="diff_sample_out",
    )(x2, o.reshape(rows, DIFF_WIDTH), p_all.reshape(-1, rows, PLE_DIM), *consts)
    return y.reshape(n, seq, D_MODEL), kn, vn


def kernel(x_prompt, x_sample, state_gla, cache_k, cache_v, page_table, p_prompt, p_sample, rel_bias,
           gla_w_in, gla_w_a2, gla_b_a, gla_norm_g, gla_w_out,
           diff_w_in, diff_lam_q1, diff_lam_k1, diff_lam_q2, diff_lam_k2, diff_norm_g, diff_w_out,
           ln_g, ln_b, ple_w_proj, ple_w_gate):
    w0 = _gla_weights(gla_w_in[0], gla_w_a2[0], gla_b_a[0], gla_norm_g[0], gla_w_out[0],
                      ln_g[0], ln_b[0], ple_w_gate[0], ple_w_proj[0])
    xp1, sp = gla_layer_prompt(x_prompt, p_prompt, 0, w0)
    xs1, ss = gla_layer_sample(x_sample, p_sample, 0, state_gla[0], w0)
    lam_init = 0.8 - 0.6 * math.exp(-0.3 * 1)
    dw = _diff_weights(diff_w_in[0], diff_lam_q1[0], diff_lam_k1[0], diff_lam_q2[0], diff_lam_k2[0],
                       diff_norm_g[0], diff_w_out[0], ln_g[1], ln_b[1], ple_w_gate[1], ple_w_proj[1])
    yp, kp, vp = diff_layer_prompt(xp1, p_prompt, 1, dw, rel_bias, lam_init)
    pool = cache_k.shape[1]
    ys, ks, vs = diff_layer_sample(xs1, p_sample, 1,
                                   cache_k[0].reshape(pool, PAGE_SIZE * DIFF_HEADS, DIFF_VD),
                                   cache_v[0].reshape(pool, PAGE_SIZE * DIFF_HEADS, DIFF_VD),
                                   page_table, dw, rel_bias, lam_init)
    bsz, t, _ = x_prompt.shape
    n, seq, _ = x_sample.shape
    heads = (DIFF_HEADS, DIFF_VD)
    return (yp, ys, sp[None], ss[None],
            kp.reshape(1, bsz, t, *heads), vp.reshape(1, bsz, t, *heads),
            ks.reshape(1, n, seq, *heads), vs.reshape(1, n, seq, *heads))
```

```python
import functools
import math

import jax
import jax.numpy as jnp
from jax import lax
from jax.experimental import pallas as pl
from jax.experimental.pallas import tpu as pltpu

F32 = jnp.float32
BF16 = jnp.bfloat16

D_MODEL = 1024
DEPTH = 2
GLA_HEADS = 4
GLA_DK = 128
GLA_DV = 256
GLA_HK = GLA_HEADS * GLA_DK
GLA_HV = GLA_HEADS * GLA_DV
GLA_LOWRANK = 16
GLA_TAU = 16.0
DIFF_HEADS = 8
DIFF_DH = 64
DIFF_VD = 128
DIFF_WIDTH = DIFF_HEADS * DIFF_VD
N_BUCKETS = 32
MAX_DISTANCE = 128
PLE_DIM = 256
PAGE_SIZE = 128
ALPHA = (2 * DEPTH) ** 0.25
EPS = 1e-5

LANES = 128
GLA_CHUNK = 64
VMEM_LIMIT = 56 * 1024 * 1024
MASK_VALUE = -1e30
LOG2_E = math.log2(math.e)
SUM_ROWS = 16
PAGE_RING = 4


def _dot(a, b):
    return jnp.dot(a, b, preferred_element_type=F32)


def _dot_tb(a, b):
    return lax.dot_general(a, b, (((1,), (1,)), ((), ())), preferred_element_type=F32)


def _dot_ta(a, b):
    return lax.dot_general(a, b, (((0,), (0,)), ((), ())), preferred_element_type=F32)


def _split_bf16(x):
    hi = x.astype(BF16)
    mid = (x - hi.astype(F32)).astype(BF16)
    return hi, mid


def _log_sigmoid(z):
    return jnp.minimum(z, 0.0) - jnp.log(1.0 + jnp.exp(-jnp.abs(z)))


def _sigmoid(z):
    return 1.0 / (1.0 + jnp.exp(-z))


def _head_rmsnorm_gate(o, gate, norm_g, n_heads, head_dim, scale):
    parts = []
    for h in range(n_heads):
        oh = o[:, h * head_dim:(h + 1) * head_dim]
        ms = jnp.mean(oh * oh, axis=-1, keepdims=True)
        parts.append(oh * lax.rsqrt(ms + EPS))
    on = jnp.concatenate(parts, axis=-1) * norm_g
    if scale != 1.0:
        on = on * scale
    return on * (gate * _sigmoid(gate))


def _residual_update(x, f, p, ln_g, ln_b, w_gate, w_proj):
    hp = ALPHA * x + f
    mu = jnp.mean(hp, axis=-1, keepdims=True)
    hc = hp - mu
    var = jnp.mean(hc * hc, axis=-1, keepdims=True)
    h = hc * lax.rsqrt(var + EPS) * ln_g + ln_b
    gate = _sigmoid(_dot(h.astype(BF16), w_gate))
    return h + gate * _dot(p.astype(BF16), w_proj)


def _gla_project(xb, wq, wk, wv, wg, wa, wa2, ba):
    q = _dot(xb, wq) * (GLA_DK ** -0.5)
    k = _dot(xb, wk)
    v = _dot(xb, wv)
    g = _dot(xb, wg)
    a_lr = _dot(xb, wa)
    z = _dot(a_lr.astype(BF16), wa2) + ba
    log_a = _log_sigmoid(z) * (1.0 / GLA_TAU)
    return q, k, v, g, log_a


def _chunk_decay_matrices(rows, chunk):
    half = chunk // 2
    i = lax.broadcasted_iota(jnp.int32, (rows, rows), 0)
    j = lax.broadcasted_iota(jnp.int32, (rows, rows), 1)
    same = (i // chunk) == (j // chunk)
    jl = j % chunk
    il = i % chunk
    pos = same & (jl >= half) & (jl <= il)
    neg = same & (jl < half) & (jl > il)
    rel = jnp.where(pos, 1.0, jnp.where(neg, -1.0, 0.0)).astype(BF16)
    n_sel = max(8, 2 * rows // chunk)
    s = lax.broadcasted_iota(jnp.int32, (n_sel, rows), 0)
    t = lax.broadcasted_iota(jnp.int32, (n_sel, rows), 1)
    halves = jnp.where((t // half) == s, 1.0, 0.0).astype(BF16)
    return rel, halves


def _gla_prompt_kernel(x_ref, p_ref, wq_ref, wk_ref, wv_ref, wg_ref, wa_ref, wa2_ref, ba_ref, ng_ref,
                       wout_ref, lng_ref, lnb_ref, wgate_ref, wproj_ref,
                       y_ref, st_ref, s_scr, o_scr, *, chunk):
    blk = pl.program_id(1)
    rows = o_scr.shape[1]

    @pl.when(blk == 0)
    def _():
        s_scr[...] = jnp.zeros_like(s_scr)

    ci = lax.broadcasted_iota(jnp.int32, (chunk, chunk), 0)
    cj = lax.broadcasted_iota(jnp.int32, (chunk, chunk), 1)
    causal = ci >= cj
    rel, halves = _chunk_decay_matrices(rows, chunk)
    for sb in range(x_ref.shape[1] // rows):
        tok = slice(sb * rows, (sb + 1) * rows)
        x = x_ref[0, tok, :]
        xb = x.astype(BF16)
        q, k, v, g, log_a = _gla_project(xb, wq_ref[...], wk_ref[...], wv_ref[...], wg_ref[...],
                                         wa_ref[...], wa2_ref[...], ba_ref[...])
        la_hi, la_mid = _split_bf16(log_a)
        d = _dot(rel, la_hi) + _dot(rel, la_mid)
        hs = _dot(halves, la_hi) + _dot(halves, la_mid)
        ehs = jnp.exp(hs)
        q_dec = (q * jnp.exp(d)).astype(BF16)
        k_inv = (k * jnp.exp(-d)).astype(BF16)
        vb = v.astype(BF16)

        for c in range(rows // chunk):
            r0 = c * chunk
            for h in range(GLA_HEADS):
                ks = slice(h * GLA_DK, (h + 1) * GLA_DK)
                vs = slice(h * GLA_DV, (h + 1) * GLA_DV)
                qd = q_dec[r0:r0 + chunk, ks]
                ki = k_inv[r0:r0 + chunk, ks]
                vh = vb[r0:r0 + chunk, vs]
                e_first = ehs[2 * c:2 * c + 1, ks]
                e_second = ehs[2 * c + 1:2 * c + 2, ks]
                s_mid = s_scr[h] * e_first
                att = jnp.where(causal, _dot_tb(qd, ki), 0.0).astype(BF16)
                o_scr[sb, r0:r0 + chunk, vs] = _dot(att, vh) + _dot_tb(qd, s_mid.astype(BF16))
                s_scr[h] = (s_mid + _dot_ta(vh, ki)) * e_second

        on = _head_rmsnorm_gate(o_scr[sb], g, ng_ref[...], GLA_HEADS, GLA_DV, 1.0)
        f = _dot(on.astype(BF16), wout_ref[...])
        y_ref[0, tok, :] = _residual_update(x, f, p_ref[0, tok, :], lng_ref[...], lnb_ref[...],
                                            wgate_ref[...], wproj_ref[...])

    @pl.when(blk == pl.num_programs(1) - 1)
    def _():
        for h in range(GLA_HEADS):
            st_ref[0, h] = s_scr[h].T


def _const_spec(shape):
    nd = len(shape)
    return pl.BlockSpec(shape, lambda *_: (0,) * nd)


def _gla_weights(w_in, w_a2, b_a, norm_g, w_out, ln_g, ln_b, w_gate, w_proj):
    hk, hv = GLA_HK, GLA_HV
    wq = w_in[:, :hk].astype(BF16)
    wk = w_in[:, hk:2 * hk].astype(BF16)
    wv = w_in[:, 2 * hk:2 * hk + hv].astype(BF16)
    wg = w_in[:, 2 * hk + hv:2 * hk + 2 * hv].astype(BF16)
    wa = jnp.pad(w_in[:, 2 * hk + 2 * hv:], ((0, 0), (0, LANES - GLA_LOWRANK))).astype(BF16)
    wa2 = jnp.pad(w_a2, ((0, LANES - GLA_LOWRANK), (0, 0))).astype(BF16)
    return (wq, wk, wv, wg, wa, wa2, b_a.reshape(1, hk), norm_g.reshape(1, hv),
            w_out.astype(BF16), ln_g.reshape(1, D_MODEL), ln_b.reshape(1, D_MODEL),
            w_gate.astype(BF16), w_proj.astype(BF16))


def gla_layer_prompt(x, p_all, layer, weights, *, block=512, sub_block=256, chunk=GLA_CHUNK):
    bsz, t, _ = x.shape
    assert t % block == 0 and block % sub_block == 0 and sub_block % chunk == 0
    in_specs = [pl.BlockSpec((1, block, D_MODEL), lambda b, i: (b, i, 0)),
                pl.BlockSpec((None, 1, block, PLE_DIM), lambda b, i: (layer, b, i, 0))]
    in_specs += [_const_spec(w.shape) for w in weights]
    return pl.pallas_call(
        functools.partial(_gla_prompt_kernel, chunk=chunk),
        grid=(bsz, t // block),
        in_specs=in_specs,
        out_specs=[pl.BlockSpec((1, block, D_MODEL), lambda b, i: (b, i, 0)),
                   pl.BlockSpec((1, GLA_HEADS, GLA_DK, GLA_DV), lambda b, i: (b, 0, 0, 0))],
        out_shape=[jax.ShapeDtypeStruct((bsz, t, D_MODEL), F32),
                   jax.ShapeDtypeStruct((bsz, GLA_HEADS, GLA_DK, GLA_DV), F32)],
        scratch_shapes=[pltpu.VMEM((GLA_HEADS, GLA_DV, GLA_DK), F32),
                        pltpu.VMEM((block // sub_block, sub_block, GLA_HV), F32)],
        compiler_params=pltpu.CompilerParams(dimension_semantics=("arbitrary", "arbitrary"),
                                             vmem_limit_bytes=VMEM_LIMIT),
        name="gla_prompt",
    )(x, p_all, *weights)


def _gla_sample_kernel(x_ref, p_ref, s0_ref, wq_ref, wk_ref, wv_ref, wg_ref, wa_ref, wa2_ref, ba_ref, ng_ref,
                       wout_ref, lng_ref, lnb_ref, wgate_ref, wproj_ref,
                       y_ref, st_ref, o_scr, *, seq):
    rows = x_ref.shape[0]
    n_seq = rows // seq
    half = seq // 2
    x = x_ref[...]
    xb = x.astype(BF16)
    q, k, v, g, log_a = _gla_project(xb, wq_ref[...], wk_ref[...], wv_ref[...], wg_ref[...],
                                     wa_ref[...], wa2_ref[...], ba_ref[...])
    rel, _ = _chunk_decay_matrices(rows, seq)
    la_hi, la_mid = _split_bf16(log_a)
    d = _dot(rel, la_hi) + _dot(rel, la_mid)
    s_i = lax.broadcasted_iota(jnp.int32, (n_seq, rows), 0)
    t_i = lax.broadcasted_iota(jnp.int32, (n_seq, rows), 1)
    in_seq = (t_i // seq) == s_i
    sel_first = jnp.where(in_seq & ((t_i % seq) < half), 1.0, 0.0).astype(BF16)
    sel_second = jnp.where(in_seq & ((t_i % seq) >= half), 1.0, 0.0).astype(BF16)
    t_c = lax.broadcasted_iota(jnp.int32, (rows, n_seq), 0)
    s_c = lax.broadcasted_iota(jnp.int32, (rows, n_seq), 1)
    sel_tot_t = jnp.where((t_c // seq) == s_c, 1.0, 0.0).astype(BF16)
    e_first = jnp.exp(_dot(sel_first, la_hi) + _dot(sel_first, la_mid))
    e_second = jnp.exp(_dot(sel_second, la_hi) + _dot(sel_second, la_mid))
    e_tot_col = jnp.exp(_dot_ta(la_hi, sel_tot_t) + _dot_ta(la_mid, sel_tot_t))
    q_dec = q * jnp.exp(d)
    k_inv = k * jnp.exp(-d)

    ci = lax.broadcasted_iota(jnp.int32, (seq, seq), 0)
    cj = lax.broadcasted_iota(jnp.int32, (seq, seq), 1)
    causal = ci >= cj
    for c in range(n_seq):
        r0 = c * seq
        for h in range(GLA_HEADS):
            ks = slice(h * GLA_DK, (h + 1) * GLA_DK)
            vs = slice(h * GLA_DV, (h + 1) * GLA_DV)
            qd = q_dec[r0:r0 + seq, ks]
            ki = k_inv[r0:r0 + seq, ks]
            vh = v[r0:r0 + seq, vs].astype(BF16)
            s_old = s0_ref[c, h]
            att = jnp.where(causal, _dot_tb(qd.astype(BF16), ki.astype(BF16)), 0.0).astype(BF16)
            q_mid = (qd * e_first[c:c + 1, ks]).astype(BF16)
            o_scr[r0:r0 + seq, vs] = _dot(att, vh) + _dot(q_mid, s_old.astype(BF16))
            k_end = (ki * e_second[c:c + 1, ks]).astype(BF16)
            st_ref[c, h] = s_old * e_tot_col[h * GLA_DK:(h + 1) * GLA_DK, c:c + 1] + _dot_ta(k_end, vh)

    on = _head_rmsnorm_gate(o_scr[...], g, ng_ref[...], GLA_HEADS, GLA_DV, 1.0)
    f = _dot(on.astype(BF16), wout_ref[...])
    y_ref[...] = _residual_update(x, f, p_ref[...], lng_ref[...], lnb_ref[...], wgate_ref[...], wproj_ref[...])


def gla_layer_sample(x, p_all, layer, s0, weights, *, group=16):
    n, seq, _ = x.shape
    assert n % group == 0 and seq % 2 == 0
    rows = group * seq
    in_specs = [pl.BlockSpec((rows, D_MODEL), lambda i: (i, 0)),
                pl.BlockSpec((None, rows, PLE_DIM), lambda i: (layer, i, 0)),
                pl.BlockSpec((group, GLA_HEADS, GLA_DK, GLA_DV), lambda i: (i, 0, 0, 0))]
    in_specs += [_resident_spec(w.shape) for w in weights]
    y, st = pl.pallas_call(
        functools.partial(_gla_sample_kernel, seq=seq),
        grid=(n // group,),
        in_specs=in_specs,
        out_specs=[pl.BlockSpec((rows, D_MODEL), lambda i: (i, 0)),
                   pl.BlockSpec((group, GLA_HEADS, GLA_DK, GLA_DV), lambda i: (i, 0, 0, 0))],
        out_shape=[jax.ShapeDtypeStruct((n * seq, D_MODEL), F32),
                   jax.ShapeDtypeStruct((n, GLA_HEADS, GLA_DK, GLA_DV), F32)],
        scratch_shapes=[pltpu.VMEM((rows, GLA_HV), F32)],
        compiler_params=pltpu.CompilerParams(dimension_semantics=("arbitrary",),
                                             vmem_limit_bytes=VMEM_LIMIT),
        name="gla_sample",
    )(x.reshape(n * seq, D_MODEL), p_all.reshape(-1, n * seq, PLE_DIM), s0, *weights)
    return y.reshape(n, seq, D_MODEL), st


def _resident_spec(shape):
    nd = len(shape)
    return pl.BlockSpec(shape, lambda *_: (0,) * nd, pipeline_mode=pl.Buffered(1))


def _t5_bias(rel_bias, dist):
    max_exact = N_BUCKETS // 2
    n = jnp.maximum(dist, 0)[None]
    nf = jnp.maximum(n, 1).astype(F32)
    steps = jnp.log(nf / max_exact) / math.log(MAX_DISTANCE / max_exact) * (N_BUCKETS - max_exact)
    table = rel_bias.astype(F32)
    per_head = (DIFF_HEADS,) + (1,) * dist.ndim
    out = jnp.broadcast_to(table[N_BUCKETS - 1].reshape(per_head), (DIFF_HEADS,) + dist.shape)
    for b in range(N_BUCKETS - 2, max_exact - 1, -1):
        out = jnp.where(steps < (b + 1 - max_exact), table[b].reshape(per_head), out)
    for b in range(max_exact):
        out = jnp.where(n == b, table[b].reshape(per_head), out)
    return out


def _store_token_head_rows(ref, lead, x):
    tokens = x.shape[0]
    for h in range(DIFF_HEADS):
        ref[(*lead, pl.ds(h, tokens, stride=DIFF_HEADS), slice(None))] = x[:, h * DIFF_VD:(h + 1) * DIFF_VD]


def _load_token_head_rows(ref, lead, tokens):
    return jnp.concatenate(
        [ref[(*lead, pl.ds(h, tokens, stride=DIFF_HEADS), slice(None))] for h in range(DIFF_HEADS)], axis=1)


def _diff_lambda(lam_ref, lam_init):
    lv = lam_ref[...]
    a = jnp.sum(lv[0:1] * lv[1:2], axis=-1, keepdims=True)
    b = jnp.sum(lv[2:3] * lv[3:4], axis=-1, keepdims=True)
    return jnp.exp(a) - jnp.exp(b) + lam_init


def _diff_prompt_kernel(x_ref, p_ref, wq_ref, wk_ref, wv_ref, wg_ref, lam_ref, bias_ref, ng_ref,
                        wout_ref, lng_ref, lnb_ref, wgate_ref, wproj_ref,
                        y_ref, kout_ref, vout_ref,
                        k_scr, vt_scr, q_scr, o_scr, alpha_scr, *bufs,
                        lam_init, heads_per_iter):
    i = pl.program_id(1)
    tq = x_ref.shape[1]
    x = x_ref[0]
    xb = x.astype(BF16)
    q = _dot(xb, wq_ref[...]) * (DIFF_DH ** -0.5 * LOG2_E)
    k = _dot(xb, wk_ref[...])
    v = _dot(xb, wv_ref[...])
    _store_token_head_rows(kout_ref, (0,), k)
    _store_token_head_rows(vout_ref, (0,), v)
    first_half = lax.broadcasted_iota(jnp.int32, (DIFF_VD, tq), 0) < DIFF_DH
    row0 = pl.multiple_of(i * tq, tq)
    for h in range(DIFF_HEADS):
        hs = slice(h * DIFF_VD, (h + 1) * DIFF_VD)
        qh_t = q[:, hs].T
        q_scr[h, :, 0:tq] = jnp.where(first_half, qh_t, 0.0).astype(BF16)
        q_scr[h, :, tq:2 * tq] = jnp.where(first_half, 0.0, qh_t).astype(BF16)
        k_scr[h, pl.ds(row0, tq), :] = k[:, hs].astype(BF16)
        vt_scr[h, i, 0:DIFF_VD, :] = v[:, hs].T.astype(BF16)
        vt_scr[h, i, DIFF_VD:, :] = jnp.ones((SUM_ROWS, tq), BF16)
    lam = _diff_lambda(lam_ref, lam_init)

    hpi = heads_per_iter
    n_groups = DIFF_HEADS // hpi
    acc_scr = bufs[0:hpi]
    m_scr = bufs[hpi:2 * hpi]
    s_buf = [bufs[2 * hpi + 2 * u:2 * hpi + 2 * u + 2] for u in range(hpi)]
    p_buf = [bufs[4 * hpi + 2 * u:4 * hpi + 2 * u + 2] for u in range(hpi)]

    n_pairs = (i + 2) // 2
    steps_per_group = 2 * n_pairs

    def block_of(t):
        near = jnp.maximum(i - t, 0)
        if isinstance(t, int):
            return near if t < 2 else jnp.minimum(t - 2, i)
        return jnp.where(t < 2, near, jnp.minimum(t - 2, i))

    def scores(g, t, slot):
        rows = pl.ds(pl.multiple_of(block_of(t) * tq, tq), tq)
        for u in range(hpi):
            h = g * hpi + u
            s_buf[u][slot][...] = _dot(k_scr[h, rows, :], q_scr[h])

    def softmax(g, t, slot, bias_slot):
        valid = t <= i
        for u in range(hpi):
            s = s_buf[u][slot][...]
            if bias_slot is not None:
                tile = bias_ref[g * hpi + u, bias_slot]
                s = s + jnp.concatenate([tile, tile], axis=1)
            m_old = m_scr[u][g]
            m_new = jnp.where(valid, jnp.maximum(m_old, jnp.max(s, axis=0, keepdims=True)), m_old)
            alpha = jnp.exp2(m_old - m_new)
            pr = jnp.exp2(s - jnp.where(valid, m_new, -MASK_VALUE))
            m_scr[u][g] = m_new
            alpha_scr[u] = alpha
            p_buf[u][slot][...] = pr.astype(BF16)

    def weighted_values(g, t, slot):
        j = block_of(t)
        for u in range(hpi):
            acc_scr[u][g] = (acc_scr[u][g] * alpha_scr[u]
                             + _dot(vt_scr[g * hpi + u, j], p_buf[u][slot][...]))

    def pair(g, r, biased):
        first, last = r == 0, r == n_pairs - 1
        weighted_values(jnp.where(first, jnp.maximum(g - 1, 0), g),
                        jnp.where(first, steps_per_group - 1, 2 * r - 1), 1)
        scores(g, 2 * r + 1, 1)
        softmax(g, 2 * r, 0, 0 if biased else None)
        weighted_values(g, 2 * r, 0)
        scores(jnp.where(last, jnp.minimum(g + 1, n_groups - 1), g), jnp.where(last, 0, 2 * r + 2), 0)
        softmax(g, 2 * r + 1, 1, 1 if biased else None)

    for u in range(hpi):
        m_scr[u][...] = jnp.full(m_scr[u].shape, MASK_VALUE, F32)
        acc_scr[u][...] = jnp.zeros(acc_scr[u].shape, F32)
        p_buf[u][1][...] = jnp.zeros(p_buf[u][1].shape, BF16)
    alpha_scr[...] = jnp.ones(alpha_scr.shape, F32)
    scores(0, 0, 0)

    def two_pairs(d, carry):
        located = []
        for n in (2 * d, 2 * d + 1):
            g = n // n_pairs
            located.append((g, n - g * n_pairs))
        (g0, r0), (g1, r1) = located
        for near0 in (True, False):
            for near1 in (True, False):
                @pl.when(((r0 == 0) == near0) & ((r1 == 0) == near1))
                def _(near0=near0, near1=near1):
                    pair(g0, r0, near0)
                    pair(g1, r1, near1)
        return carry

    assert n_groups % 2 == 0
    lax.fori_loop(0, n_groups * n_pairs // 2, two_pairs, 0)
    weighted_values(n_groups - 1, steps_per_group - 1, 1)

    for h in range(DIFF_HEADS):
        g, u = divmod(h, hpi)
        inv_l = 1.0 / acc_scr[u][g, DIFF_VD:DIFF_VD + 1, :]
        acc = acc_scr[u][g, 0:DIFF_VD, :]
        o_t = acc[:, :tq] * inv_l[:, :tq] - lam * (acc[:, tq:] * inv_l[:, tq:])
        o_scr[h] = o_t.T
    o = jnp.concatenate([o_scr[h] for h in range(DIFF_HEADS)], axis=-1)
    gate = _dot(xb, wg_ref[...])
    on = _head_rmsnorm_gate(o, gate, ng_ref[...], DIFF_HEADS, DIFF_VD, 1.0 - lam_init)
    f = _dot(on.astype(BF16), wout_ref[...])
    y_ref[0] = _residual_update(x, f, p_ref[0], lng_ref[...], lnb_ref[...], wgate_ref[...], wproj_ref[...])


def _diff_weights(w_in, lam_q1, lam_k1, lam_q2, lam_k2, norm_g, w_out, ln_g, ln_b, w_gate, w_proj):
    w = DIFF_WIDTH
    return dict(
        wq=w_in[:, :w].astype(BF16), wk=w_in[:, w:2 * w].astype(BF16),
        wv=w_in[:, 2 * w:3 * w].astype(BF16), wg=w_in[:, 3 * w:].astype(BF16),
        lam=jnp.stack([lam_q1, lam_k1, lam_q2, lam_k2]).astype(F32),
        ng=norm_g.reshape(1, w), wout=w_out.astype(BF16),
        lng=ln_g.reshape(1, D_MODEL), lnb=ln_b.reshape(1, D_MODEL),
        wgate=w_gate.astype(BF16), wproj=w_proj.astype(BF16))


def diff_layer_prompt(x, p_all, layer, dw, rel_bias, lam_init, *, block=256, heads_per_iter=2):
    bsz, t, _ = x.shape
    assert t % block == 0 and block >= MAX_DISTANCE and DIFF_HEADS % heads_per_iter == 0
    n_groups = DIFF_HEADS // heads_per_iter
    kk = jnp.arange(block, dtype=jnp.int32)[:, None]
    qq = jnp.arange(block, dtype=jnp.int32)[None, :]
    dist = jnp.stack([qq - kk, block + qq - kk])
    far = rel_bias.astype(F32)[N_BUCKETS - 1].reshape(DIFF_HEADS, 1, 1, 1)
    bias = jnp.where(dist >= 0, (_t5_bias(rel_bias, dist) - far) * LOG2_E, MASK_VALUE)
    tok = lambda width: pl.BlockSpec((1, block, width), lambda b, i: (b, i, 0))
    kv_spec = pl.BlockSpec((1, block * DIFF_HEADS, DIFF_VD), lambda b, i: (b, i, 0))
    consts = [dw["wq"], dw["wk"], dw["wv"], dw["wg"], dw["lam"], bias, dw["ng"], dw["wout"],
              dw["lng"], dw["lnb"], dw["wgate"], dw["wproj"]]
    return pl.pallas_call(
        functools.partial(_diff_prompt_kernel, lam_init=lam_init, heads_per_iter=heads_per_iter),
        grid=(bsz, t // block),
        in_specs=[tok(D_MODEL), pl.BlockSpec((None, 1, block, PLE_DIM), lambda b, i: (layer, b, i, 0))]
                 + [_resident_spec(c.shape) for c in consts],
        out_specs=[tok(D_MODEL), kv_spec, kv_spec],
        out_shape=[jax.ShapeDtypeStruct((bsz, t, D_MODEL), F32),
                   jax.ShapeDtypeStruct((bsz, t * DIFF_HEADS, DIFF_VD), F32),
                   jax.ShapeDtypeStruct((bsz, t * DIFF_HEADS, DIFF_VD), F32)],
        scratch_shapes=[pltpu.VMEM((DIFF_HEADS, t, DIFF_VD), BF16),
                        pltpu.VMEM((DIFF_HEADS, t // block, DIFF_VD + SUM_ROWS, block), BF16),
                        pltpu.VMEM((DIFF_HEADS, DIFF_VD, 2 * block), BF16),
                        pltpu.VMEM((DIFF_HEADS, block, DIFF_VD), F32),
                        pltpu.VMEM((heads_per_iter, 1, 2 * block), F32),
                        *[pltpu.VMEM((n_groups, DIFF_VD + SUM_ROWS, 2 * block), F32)] * heads_per_iter,
                        *[pltpu.VMEM((n_groups, 1, 2 * block), F32)] * heads_per_iter,
                        *[pltpu.VMEM((block, 2 * block), F32)] * (2 * heads_per_iter),
                        *[pltpu.VMEM((block, 2 * block), BF16)] * (2 * heads_per_iter)],
        compiler_params=pltpu.CompilerParams(dimension_semantics=("arbitrary", "arbitrary"),
                                             vmem_limit_bytes=VMEM_LIMIT),
        name="diff_prompt",
    )(x, p_all, *consts)


def _diff_sample_project_kernel(x_ref, wq_ref, wk_ref, wv_ref, qt_ref, k_ref, v_ref):
    xb = x_ref[...].astype(BF16)
    qt_ref[...] = (_dot(xb, wq_ref[...]) * (DIFF_DH ** -0.5)).T.astype(BF16)
    _store_token_head_rows(k_ref, (), _dot(xb, wk_ref[...]))
    _store_token_head_rows(v_ref, (), _dot(xb, wv_ref[...]))


def _row_to_col(row, n):
    eye = lax.broadcasted_iota(jnp.int32, (n, n), 0) == lax.broadcasted_iota(jnp.int32, (n, n), 1)
    return jnp.sum(jnp.where(eye, row, 0.0), axis=1, keepdims=True)


def _paged_attn_kernel(pt_ref, qt_ref, kn_ref, vn_ref, lam_ref, bias_ref, bnew_ref, own_ref, ck_hbm, cv_hbm, o_ref,
                       k_ring, v_ring, sems, wq_scr, acc_scr, m_scr, l_scr, *, pages, seq, lam_init):
    b = pl.program_id(0)
    g = pl.program_id(1)
    n_steps = pl.num_programs(0) * pl.num_programs(1)
    step = b * pl.num_programs(1) + g
    ncol = 2 * DIFF_HEADS * seq

    def page_copies(s, slot):
        copies = []
        for i in range(pages):
            page = pt_ref[s * pages + i]
            copies.append(pltpu.make_async_copy(ck_hbm.at[page], k_ring.at[slot, i], sems.at[0, slot]))
            copies.append(pltpu.make_async_copy(cv_hbm.at[page], v_ring.at[slot, i], sems.at[1, slot]))
        return copies

    @pl.when(step == 0)
    def _():
        for ahead in range(PAGE_RING - 1):
            @pl.when(ahead < n_steps)
            def _():
                for c in page_copies(ahead, ahead):
                    c.start()

    nxt = step + PAGE_RING - 1

    @pl.when(nxt < n_steps)
    def _():
        for c in page_copies(nxt, lax.rem(nxt, PAGE_RING)):
            c.start()

    slot = lax.rem(step, PAGE_RING)
    for c in page_copies(step, slot):
        c.wait()
    k_pages = [k_ring.at[slot, i] for i in range(pages)]
    v_pages = [v_ring.at[slot, i] for i in range(pages)]

    @pl.when(g == 0)
    def _():
        local = (b % (LANES // seq)) * seq
        src = lax.broadcasted_iota(jnp.int32, (LANES, ncol), 0)
        col = lax.broadcasted_iota(jnp.int32, (LANES, ncol), 1)
        pick = jnp.where(src == local + col % seq, 1.0, 0.0).astype(BF16)
        rep = _dot(qt_ref[...], pick)
        wq_scr[...] = (rep * own_ref[...]).astype(BF16)
        m_scr[...] = jnp.full(m_scr.shape, MASK_VALUE, F32)
        l_scr[...] = jnp.zeros(l_scr.shape, F32)
        acc_scr[...] = jnp.zeros(acc_scr.shape, F32)

    def flash_step(scores, values):
        m_old = m_scr[...]
        m_new = m_old
        for s in scores:
            m_new = jnp.maximum(m_new, jnp.max(s, axis=0, keepdims=True))
        alpha = jnp.exp(m_old - m_new)
        l_new = alpha * l_scr[...]
        probs = []
        for s in scores:
            pr = jnp.exp(s - m_new)
            l_new = l_new + jnp.sum(pr, axis=0, keepdims=True)
            probs.append(pr.astype(BF16))
        pv = None
        for a in range(0, len(probs), 2):
            t = _dot_ta(jnp.concatenate(probs[a:a + 2], axis=0),
                        jnp.concatenate(values[a:a + 2], axis=0))
            pv = t if pv is None else pv + t
        acc_scr[...] = acc_scr[...] * _row_to_col(alpha, ncol) + pv
        l_scr[...] = l_new
        m_scr[...] = m_new

    def load_page(ref):
        return _load_token_head_rows(ref, (), PAGE_SIZE).astype(BF16)

    wq = wq_scr[...]
    scores, values = [], []
    for i in range(pages):
        s = _dot(load_page(k_pages[i]), wq)
        scores.append(s + bias_ref[0] if i == pages - 1 else s)
        values.append(load_page(v_pages[i]))
    flash_step(scores, values)

    @pl.when(g == pl.num_programs(1) - 1)
    def _():
        pad = jnp.zeros((16 - seq, DIFF_WIDTH), F32)
        kn = jnp.concatenate([_load_token_head_rows(kn_ref, (0,), seq), pad], axis=0).astype(BF16)
        vn = jnp.concatenate([_load_token_head_rows(vn_ref, (0,), seq), pad], axis=0).astype(BF16)
        flash_step([_dot(kn, wq) + bnew_ref[...]], [vn])
        lam = _diff_lambda(lam_ref, lam_init)
        inv_l = _row_to_col(1.0 / l_scr[...], ncol)
        half_rows = DIFF_HEADS * seq
        for h in range(DIFF_HEADS):
            cs = slice(h * DIFF_VD, (h + 1) * DIFF_VD)
            r1 = slice(h * seq, (h + 1) * seq)
            r2 = slice(half_rows + h * seq, half_rows + (h + 1) * seq)
            o_ref[0, :, cs] = acc_scr[r1, cs] * inv_l[r1] - lam * (acc_scr[r2, cs] * inv_l[r2])


def _diff_sample_out_kernel(x_ref, o_ref, p_ref, wg_ref, ng_ref, wout_ref, lng_ref, lnb_ref, wgate_ref, wproj_ref,
                            y_ref, *, lam_init):
    x = x_ref[...]
    gate = _dot(x.astype(BF16), wg_ref[...])
    on = _head_rmsnorm_gate(o_ref[...], gate, ng_ref[...], DIFF_HEADS, DIFF_VD, 1.0 - lam_init)
    f = _dot(on.astype(BF16), wout_ref[...])
    y_ref[...] = _residual_update(x, f, p_ref[...], lng_ref[...], lnb_ref[...], wgate_ref[...], wproj_ref[...])


def diff_layer_sample(x, p_all, layer, cache_k, cache_v, page_table, dw, rel_bias, lam_init, *, pages=8, block=256):
    n, seq, _ = x.shape
    n_pages = page_table.shape[1]
    rows = n * seq
    assert n_pages % pages == 0 and rows % block == 0 and LANES % seq == 0 and seq <= 16
    assert PAGE_SIZE >= MAX_DISTANCE
    x2 = x.reshape(rows, D_MODEL)
    row_spec = lambda width: pl.BlockSpec((block, width), lambda i: (i, 0))
    qt, kn, vn = pl.pallas_call(
        _diff_sample_project_kernel,
        grid=(rows // block,),
        in_specs=[row_spec(D_MODEL)] + [_const_spec((D_MODEL, DIFF_WIDTH))] * 3,
        out_specs=[pl.BlockSpec((DIFF_WIDTH, block), lambda i: (0, i)),
                   pl.BlockSpec((block * DIFF_HEADS, DIFF_VD), lambda i: (i, 0)),
                   pl.BlockSpec((block * DIFF_HEADS, DIFF_VD), lambda i: (i, 0))],
        out_shape=[jax.ShapeDtypeStruct((DIFF_WIDTH, rows), BF16),
                   jax.ShapeDtypeStruct((rows * DIFF_HEADS, DIFF_VD), F32),
                   jax.ShapeDtypeStruct((rows * DIFF_HEADS, DIFF_VD), F32)],
        compiler_params=pltpu.CompilerParams(dimension_semantics=("arbitrary",), vmem_limit_bytes=VMEM_LIMIT),
        name="diff_sample_project",
    )(x2, dw["wq"], dw["wk"], dw["wv"])

    ncol = 2 * DIFF_HEADS * seq
    col = jnp.arange(ncol, dtype=jnp.int32)[None, :]
    col_h = (col % (DIFF_HEADS * seq)) // seq
    col_t = col % seq
    kk = jnp.arange(PAGE_SIZE, dtype=jnp.int32)[:, None]
    tk = jnp.arange(16, dtype=jnp.int32)[:, None]
    dist_last = PAGE_SIZE + col_t - kk
    dist_new = col_t - tk
    far = rel_bias.astype(F32)[N_BUCKETS - 1].reshape(DIFF_HEADS, 1, 1)

    def own_head(per_head):
        return sum(jnp.where(col_h == h, per_head[h], 0.0) for h in range(DIFF_HEADS))

    bias_last = own_head(_t5_bias(rel_bias, dist_last) - far)
    bias_pages = jnp.stack([jnp.zeros_like(bias_last), bias_last])
    bias_new = jnp.where((tk < seq) & (dist_new >= 0), own_head(_t5_bias(rel_bias, dist_new) - far), MASK_VALUE)
    feat = jnp.arange(DIFF_WIDTH, dtype=jnp.int32)[:, None]
    own = ((feat // DIFF_VD == col_h) & ((feat % DIFF_VD) // DIFF_DH == col // (DIFF_HEADS * seq))).astype(F32)

    n_groups = n_pages // pages
    seq_per_blk = LANES // seq
    page_shape = (PAGE_RING, pages, PAGE_SIZE * DIFF_HEADS, DIFF_VD)

    tok_spec = pl.BlockSpec((1, seq, DIFF_WIDTH), lambda b, g, pt: (b, 0, 0))
    new_spec = pl.BlockSpec((1, seq * DIFF_HEADS, DIFF_VD), lambda b, g, pt: (b, 0, 0))
    grid_spec = pltpu.PrefetchScalarGridSpec(
        num_scalar_prefetch=1,
        grid=(n, n_groups),
        in_specs=[pl.BlockSpec((DIFF_WIDTH, LANES), lambda b, g, pt: (0, b // seq_per_blk)),
                  new_spec, new_spec,
                  pl.BlockSpec((4, DIFF_DH), lambda b, g, pt: (0, 0)),
                  pl.BlockSpec((1, PAGE_SIZE, ncol), lambda b, g, pt: ((g + 1) // n_groups, 0, 0)),
                  pl.BlockSpec((16, ncol), lambda b, g, pt: (0, 0)),
                  pl.BlockSpec((DIFF_WIDTH, ncol), lambda b, g, pt: (0, 0)),
                  pl.BlockSpec(memory_space=pl.ANY), pl.BlockSpec(memory_space=pl.ANY)],
        out_specs=tok_spec,
        scratch_shapes=[pltpu.VMEM(page_shape, F32), pltpu.VMEM(page_shape, F32),
                        pltpu.SemaphoreType.DMA((2, PAGE_RING)),
                        pltpu.VMEM((DIFF_WIDTH, ncol), BF16),
                        pltpu.VMEM((ncol, DIFF_WIDTH), F32),
                        pltpu.VMEM((1, ncol), F32),
                        pltpu.VMEM((1, ncol), F32)])
    o = pl.pallas_call(
        functools.partial(_paged_attn_kernel, pages=pages, seq=seq, lam_init=lam_init),
        grid_spec=grid_spec,
        out_shape=jax.ShapeDtypeStruct((n, seq, DIFF_WIDTH), F32),
        compiler_params=pltpu.CompilerParams(dimension_semantics=("arbitrary", "arbitrary"),
                                             vmem_limit_bytes=VMEM_LIMIT),
        name="diff_sample_attn",
    )(page_table.reshape(-1), qt, kn.reshape(n, seq * DIFF_HEADS, DIFF_VD), vn.reshape(n, seq * DIFF_HEADS, DIFF_VD),
      dw["lam"], bias_pages, bias_new, own, cache_k, cache_v)

    consts = [dw["wg"], dw["ng"], dw["wout"], dw["lng"], dw["lnb"], dw["wgate"], dw["wproj"]]
    y = pl.pallas_call(
        functools.partial(_diff_sample_out_kernel, lam_init=lam_init),
        grid=(rows // block,),
        in_specs=[row_spec(D_MODEL), row_spec(DIFF_WIDTH),
                  pl.BlockSpec((None, block, PLE_DIM), lambda i: (layer, i, 0))] + [_const_spec(c.shape) for c in consts],
        out_specs=row_spec(D_MODEL),
        out_shape=jax.ShapeDtypeStruct((rows, D_MODEL), F32),
        compiler_params=pltpu.CompilerParams(dimension_semantics=("arbitrary",), vmem_limit_bytes=VMEM_LIMIT),
        name="diff_sample_out",
    )(x2, o.reshape(rows, DIFF_WIDTH), p_all.reshape(-1, rows, PLE_DIM), *consts)
    return y.reshape(n, seq, D_MODEL), kn, vn


def kernel(x_prompt, x_sample, state_gla, cache_k, cache_v, page_table, p_prompt, p_sample, rel_bias,
           gla_w_in, gla_w_a2, gla_b_a, gla_norm_g, gla_w_out,
           diff_w_in, diff_lam_q1, diff_lam_k1, diff_lam_q2, diff_lam_k2, diff_norm_g, diff_w_out,
           ln_g, ln_b, ple_w_proj, ple_w_gate):
    w0 = _gla_weights(gla_w_in[0], gla_w_a2[0], gla_b_a[0], gla_norm_g[0], gla_w_out[0],
                      ln_g[0], ln_b[0], ple_w_gate[0], ple_w_proj[0])
    xp1, sp = gla_layer_prompt(x_prompt, p_prompt, 0, w0)
    xs1, ss = gla_layer_sample(x_sample, p_sample, 0, state_gla[0], w0)
    lam_init = 0.8 - 0.6 * math.exp(-0.3 * 1)
    dw = _diff_weights(diff_w_in[0], diff_lam_q1[0], diff_lam_k1[0], diff_lam_q2[0], diff_lam_k2[0],
                       diff_norm_g[0], diff_w_out[0], ln_g[1], ln_b[1], ple_w_gate[1], ple_w_proj[1])
    yp, kp, vp = diff_layer_prompt(xp1, p_prompt, 1, dw, rel_bias, lam_init)
    pool = cache_k.shape[1]
    ys, ks, vs = diff_layer_sample(xs1, p_sample, 1,
                                   cache_k[0].reshape(pool, PAGE_SIZE * DIFF_HEADS, DIFF_VD),
                                   cache_v[0].reshape(pool, PAGE_SIZE * DIFF_HEADS, DIFF_VD),
                                   page_table, dw, rel_bias, lam_init)
    bsz, t, _ = x_prompt.shape
    n, seq, _ = x_sample.shape
    heads = (DIFF_HEADS, DIFF_VD)
    return (yp, ys, sp[None], ss[None],
            kp.reshape(1, bsz, t, *heads), vp.reshape(1, bsz, t, *heads),
            ks.reshape(1, n, seq, *heads), vs.reshape(1, n, seq, *heads))
```

```python
import functools
import math

import jax
import jax.numpy as jnp
from jax import lax
from jax.experimental import pallas as pl
from jax.experimental.pallas import tpu as pltpu

F32 = jnp.float32
BF16 = jnp.bfloat16

D_MODEL = 1024
DEPTH = 2
GLA_HEADS = 4
GLA_DK = 128
GLA_DV = 256
GLA_HK = GLA_HEADS * GLA_DK
GLA_HV = GLA_HEADS * GLA_DV
GLA_LOWRANK = 16
GLA_TAU = 16.0
DIFF_HEADS = 8
DIFF_DH = 64
DIFF_VD = 128
DIFF_WIDTH = DIFF_HEADS * DIFF_VD
N_BUCKETS = 32
MAX_DISTANCE = 128
PLE_DIM = 256
PAGE_SIZE = 128
ALPHA = (2 * DEPTH) ** 0.25
EPS = 1e-5

LANES = 128
GLA_CHUNK = 64
VMEM_LIMIT = 56 * 1024 * 1024
MASK_VALUE = -1e30
LOG2_E = math.log2(math.e)
SUM_ROWS = 16
PAGE_RING = 4


def _dot(a, b):
    return jnp.dot(a, b, preferred_element_type=F32)


def _dot_tb(a, b):
    return lax.dot_general(a, b, (((1,), (1,)), ((), ())), preferred_element_type=F32)


def _dot_ta(a, b):
    return lax.dot_general(a, b, (((0,), (0,)), ((), ())), preferred_element_type=F32)


def _split_bf16(x):
    hi = x.astype(BF16)
    mid = (x - hi.astype(F32)).astype(BF16)
    return hi, mid


def _log_sigmoid(z):
    return jnp.minimum(z, 0.0) - jnp.log(1.0 + jnp.exp(-jnp.abs(z)))


def _sigmoid(z):
    return 1.0 / (1.0 + jnp.exp(-z))


def _head_rmsnorm_gate(o, gate, norm_g, n_heads, head_dim, scale):
    parts = []
    for h in range(n_heads):
        oh = o[:, h * head_dim:(h + 1) * head_dim]
        ms = jnp.mean(oh * oh, axis=-1, keepdims=True)
        parts.append(oh * lax.rsqrt(ms + EPS))
    on = jnp.concatenate(parts, axis=-1) * norm_g
    if scale != 1.0:
        on = on * scale
    return on * (gate * _sigmoid(gate))


def _residual_update(x, f, p, ln_g, ln_b, w_gate, w_proj):
    hp = ALPHA * x + f
    mu = jnp.mean(hp, axis=-1, keepdims=True)
    hc = hp - mu
    var = jnp.mean(hc * hc, axis=-1, keepdims=True)
    h = hc * lax.rsqrt(var + EPS) * ln_g + ln_b
    gate = _sigmoid(_dot(h.astype(BF16), w_gate))
    return h + gate * _dot(p.astype(BF16), w_proj)


def _gla_project(xb, wq, wk, wv, wg, wa, wa2, ba):
    q = _dot(xb, wq) * (GLA_DK ** -0.5)
    k = _dot(xb, wk)
    v = _dot(xb, wv)
    g = _dot(xb, wg)
    a_lr = _dot(xb, wa)
    z = _dot(a_lr.astype(BF16), wa2) + ba
    log_a = _log_sigmoid(z) * (1.0 / GLA_TAU)
    return q, k, v, g, log_a


def _chunk_decay_matrices(rows, chunk):
    half = chunk // 2
    i = lax.broadcasted_iota(jnp.int32, (rows, rows), 0)
    j = lax.broadcasted_iota(jnp.int32, (rows, rows), 1)
    same = (i // chunk) == (j // chunk)
    jl = j % chunk
    il = i % chunk
    pos = same & (jl >= half) & (jl <= il)
    neg = same & (jl < half) & (jl > il)
    rel = jnp.where(pos, 1.0, jnp.where(neg, -1.0, 0.0)).astype(BF16)
    n_sel = max(8, 2 * rows // chunk)
    s = lax.broadcasted_iota(jnp.int32, (n_sel, rows), 0)
    t = lax.broadcasted_iota(jnp.int32, (n_sel, rows), 1)
    halves = jnp.where((t // half) == s, 1.0, 0.0).astype(BF16)
    return rel, halves


def _gla_prompt_kernel(x_ref, p_ref, wq_ref, wk_ref, wv_ref, wg_ref, wa_ref, wa2_ref, ba_ref, ng_ref,
                       wout_ref, lng_ref, lnb_ref, wgate_ref, wproj_ref,
                       y_ref, st_ref, s_scr, o_scr, *, chunk):
    blk = pl.program_id(1)
    rows = o_scr.shape[1]

    @pl.when(blk == 0)
    def _():
        s_scr[...] = jnp.zeros_like(s_scr)

    ci = lax.broadcasted_iota(jnp.int32, (chunk, chunk), 0)
    cj = lax.broadcasted_iota(jnp.int32, (chunk, chunk), 1)
    causal = ci >= cj
    rel, halves = _chunk_decay_matrices(rows, chunk)
    for sb in range(x_ref.shape[1] // rows):
        tok = slice(sb * rows, (sb + 1) * rows)
        x = x_ref[0, tok, :]
        xb = x.astype(BF16)
        q, k, v, g, log_a = _gla_project(xb, wq_ref[...], wk_ref[...], wv_ref[...], wg_ref[...],
                                         wa_ref[...], wa2_ref[...], ba_ref[...])
        la_hi, la_mid = _split_bf16(log_a)
        d = _dot(rel, la_hi) + _dot(rel, la_mid)
        hs = _dot(halves, la_hi) + _dot(halves, la_mid)
        ehs = jnp.exp(hs)
        q_dec = (q * jnp.exp(d)).astype(BF16)
        k_inv = (k * jnp.exp(-d)).astype(BF16)
        vb = v.astype(BF16)

        for c in range(rows // chunk):
            r0 = c * chunk
            for h in range(GLA_HEADS):
                ks = slice(h * GLA_DK, (h + 1) * GLA_DK)
                vs = slice(h * GLA_DV, (h + 1) * GLA_DV)
                qd = q_dec[r0:r0 + chunk, ks]
                ki = k_inv[r0:r0 + chunk, ks]
                vh = vb[r0:r0 + chunk, vs]
                e_first = ehs[2 * c:2 * c + 1, ks]
                e_second = ehs[2 * c + 1:2 * c + 2, ks]
                s_mid = s_scr[h] * e_first
                att = jnp.where(causal, _dot_tb(qd, ki), 0.0).astype(BF16)
                o_scr[sb, r0:r0 + chunk, vs] = _dot(att, vh) + _dot_tb(qd, s_mid.astype(BF16))
                s_scr[h] = (s_mid + _dot_ta(vh, ki)) * e_second

        on = _head_rmsnorm_gate(o_scr[sb], g, ng_ref[...], GLA_HEADS, GLA_DV, 1.0)
        f = _dot(on.astype(BF16), wout_ref[...])
        y_ref[0, tok, :] = _residual_update(x, f, p_ref[0, tok, :], lng_ref[...], lnb_ref[...],
                                            wgate_ref[...], wproj_ref[...])

    @pl.when(blk == pl.num_programs(1) - 1)
    def _():
        for h in range(GLA_HEADS):
            st_ref[0, h] = s_scr[h].T


def _const_spec(shape):
    nd = len(shape)
    return pl.BlockSpec(shape, lambda *_: (0,) * nd)


def _gla_weights(w_in, w_a2, b_a, norm_g, w_out, ln_g, ln_b, w_gate, w_proj):
    hk, hv = GLA_HK, GLA_HV
    wq = w_in[:, :hk].astype(BF16)
    wk = w_in[:, hk:2 * hk].astype(BF16)
    wv = w_in[:, 2 * hk:2 * hk + hv].astype(BF16)
    wg = w_in[:, 2 * hk + hv:2 * hk + 2 * hv].astype(BF16)
    wa = jnp.pad(w_in[:, 2 * hk + 2 * hv:], ((0, 0), (0, LANES - GLA_LOWRANK))).astype(BF16)
    wa2 = jnp.pad(w_a2, ((0, LANES - GLA_LOWRANK), (0, 0))).astype(BF16)
    return (wq, wk, wv, wg, wa, wa2, b_a.reshape(1, hk), norm_g.reshape(1, hv),
            w_out.astype(BF16), ln_g.reshape(1, D_MODEL), ln_b.reshape(1, D_MODEL),
            w_gate.astype(BF16), w_proj.astype(BF16))


def gla_layer_prompt(x, p_all, layer, weights, *, block=512, sub_block=256, chunk=GLA_CHUNK):
    bsz, t, _ = x.shape
    assert t % block == 0 and block % sub_block == 0 and sub_block % chunk == 0
    in_specs = [pl.BlockSpec((1, block, D_MODEL), lambda b, i: (b, i, 0)),
                pl.BlockSpec((None, 1, block, PLE_DIM), lambda b, i: (layer, b, i, 0))]
    in_specs += [_const_spec(w.shape) for w in weights]
    return pl.pallas_call(
        functools.partial(_gla_prompt_kernel, chunk=chunk),
        grid=(bsz, t // block),
        in_specs=in_specs,
        out_specs=[pl.BlockSpec((1, block, D_MODEL), lambda b, i: (b, i, 0)),
                   pl.BlockSpec((1, GLA_HEADS, GLA_DK, GLA_DV), lambda b, i: (b, 0, 0, 0))],
        out_shape=[jax.ShapeDtypeStruct((bsz, t, D_MODEL), F32),
                   jax.ShapeDtypeStruct((bsz, GLA_HEADS, GLA_DK, GLA_DV), F32)],
        scratch_shapes=[pltpu.VMEM((GLA_HEADS, GLA_DV, GLA_DK), F32),
                        pltpu.VMEM((block // sub_block, sub_block, GLA_HV), F32)],
        compiler_params=pltpu.CompilerParams(dimension_semantics=("arbitrary", "arbitrary"),
                                             vmem_limit_bytes=VMEM_LIMIT),
        name="gla_prompt",
    )(x, p_all, *weights)


def _gla_sample_kernel(x_ref, p_ref, s0_ref, wq_ref, wk_ref, wv_ref, wg_ref, wa_ref, wa2_ref, ba_ref, ng_ref,
                       wout_ref, lng_ref, lnb_ref, wgate_ref, wproj_ref,
                       y_ref, st_ref, o_scr, *, seq):
    rows = x_ref.shape[0]
    n_seq = rows // seq
    half = seq // 2
    x = x_ref[...]
    xb = x.astype(BF16)
    q, k, v, g, log_a = _gla_project(xb, wq_ref[...], wk_ref[...], wv_ref[...], wg_ref[...],
                                     wa_ref[...], wa2_ref[...], ba_ref[...])
    rel, _ = _chunk_decay_matrices(rows, seq)
    la_hi, la_mid = _split_bf16(log_a)
    d = _dot(rel, la_hi) + _dot(rel, la_mid)
    s_i = lax.broadcasted_iota(jnp.int32, (n_seq, rows), 0)
    t_i = lax.broadcasted_iota(jnp.int32, (n_seq, rows), 1)
    in_seq = (t_i // seq) == s_i
    sel_first = jnp.where(in_seq & ((t_i % seq) < half), 1.0, 0.0).astype(BF16)
    sel_second = jnp.where(in_seq & ((t_i % seq) >= half), 1.0, 0.0).astype(BF16)
    t_c = lax.broadcasted_iota(jnp.int32, (rows, n_seq), 0)
    s_c = lax.broadcasted_iota(jnp.int32, (rows, n_seq), 1)
    sel_tot_t = jnp.where((t_c // seq) == s_c, 1.0, 0.0).astype(BF16)
    e_first = jnp.exp(_dot(sel_first, la_hi) + _dot(sel_first, la_mid))
    e_second = jnp.exp(_dot(sel_second, la_hi) + _dot(sel_second, la_mid))
    e_tot_col = jnp.exp(_dot_ta(la_hi, sel_tot_t) + _dot_ta(la_mid, sel_tot_t))
    q_dec = q * jnp.exp(d)
    k_inv = k * jnp.exp(-d)

    ci = lax.broadcasted_iota(jnp.int32, (seq, seq), 0)
    cj = lax.broadcasted_iota(jnp.int32, (seq, seq), 1)
    causal = ci >= cj
    for c in range(n_seq):
        r0 = c * seq
        for h in range(GLA_HEADS):
            ks = slice(h * GLA_DK, (h + 1) * GLA_DK)
            vs = slice(h * GLA_DV, (h + 1) * GLA_DV)
            qd = q_dec[r0:r0 + seq, ks]
            ki = k_inv[r0:r0 + seq, ks]
            vh = v[r0:r0 + seq, vs].astype(BF16)
            s_old = s0_ref[c, h]
            att = jnp.where(causal, _dot_tb(qd.astype(BF16), ki.astype(BF16)), 0.0).astype(BF16)
            q_mid = (qd * e_first[c:c + 1, ks]).astype(BF16)
            o_scr[r0:r0 + seq, vs] = _dot(att, vh) + _dot(q_mid, s_old.astype(BF16))
            k_end = (ki * e_second[c:c + 1, ks]).astype(BF16)
            st_ref[c, h] = s_old * e_tot_col[h * GLA_DK:(h + 1) * GLA_DK, c:c + 1] + _dot_ta(k_end, vh)

    on = _head_rmsnorm_gate(o_scr[...], g, ng_ref[...], GLA_HEADS, GLA_DV, 1.0)
    f = _dot(on.astype(BF16), wout_ref[...])
    y_ref[...] = _residual_update(x, f, p_ref[...], lng_ref[...], lnb_ref[...], wgate_ref[...], wproj_ref[...])


def gla_layer_sample(x, p_all, layer, s0, weights, *, group=16):
    n, seq, _ = x.shape
    assert n % group == 0 and seq % 2 == 0
    rows = group * seq
    in_specs = [pl.BlockSpec((rows, D_MODEL), lambda i: (i, 0)),
                pl.BlockSpec((None, rows, PLE_DIM), lambda i: (layer, i, 0)),
                pl.BlockSpec((group, GLA_HEADS, GLA_DK, GLA_DV), lambda i: (i, 0, 0, 0))]
    in_specs += [_resident_spec(w.shape) for w in weights]
    y, st = pl.pallas_call(
        functools.partial(_gla_sample_kernel, seq=seq),
        grid=(n // group,),
        in_specs=in_specs,
        out_specs=[pl.BlockSpec((rows, D_MODEL), lambda i: (i, 0)),
                   pl.BlockSpec((group, GLA_HEADS, GLA_DK, GLA_DV), lambda i: (i, 0, 0, 0))],
        out_shape=[jax.ShapeDtypeStruct((n * seq, D_MODEL), F32),
                   jax.ShapeDtypeStruct((n, GLA_HEADS, GLA_DK, GLA_DV), F32)],
        scratch_shapes=[pltpu.VMEM((rows, GLA_HV), F32)],
        compiler_params=pltpu.CompilerParams(dimension_semantics=("arbitrary",),
                                             vmem_limit_bytes=VMEM_LIMIT),
        name="gla_sample",
    )(x.reshape(n * seq, D_MODEL), p_all.reshape(-1, n * seq, PLE_DIM), s0, *weights)
    return y.reshape(n, seq, D_MODEL), st


def _resident_spec(shape):
    nd = len(shape)
    return pl.BlockSpec(shape, lambda *_: (0,) * nd, pipeline_mode=pl.Buffered(1))


def _t5_bias(rel_bias, dist):
    max_exact = N_BUCKETS // 2
    n = jnp.maximum(dist, 0)[None]
    nf = jnp.maximum(n, 1).astype(F32)
    steps = jnp.log(nf / max_exact) / math.log(MAX_DISTANCE / max_exact) * (N_BUCKETS - max_exact)
    table = rel_bias.astype(F32)
    per_head = (DIFF_HEADS,) + (1,) * dist.ndim
    out = jnp.broadcast_to(table[N_BUCKETS - 1].reshape(per_head), (DIFF_HEADS,) + dist.shape)
    for b in range(N_BUCKETS - 2, max_exact - 1, -1):
        out = jnp.where(steps < (b + 1 - max_exact), table[b].reshape(per_head), out)
    for b in range(max_exact):
        out = jnp.where(n == b, table[b].reshape(per_head), out)
    return out


def _store_token_head_rows(ref, lead, x):
    tokens = x.shape[0]
    for h in range(DIFF_HEADS):
        ref[(*lead, pl.ds(h, tokens, stride=DIFF_HEADS), slice(None))] = x[:, h * DIFF_VD:(h + 1) * DIFF_VD]


def _load_token_head_rows(ref, lead, tokens):
    return jnp.concatenate(
        [ref[(*lead, pl.ds(h, tokens, stride=DIFF_HEADS), slice(None))] for h in range(DIFF_HEADS)], axis=1)


def _diff_lambda(lam_ref, lam_init):
    lv = lam_ref[...]
    a = jnp.sum(lv[0:1] * lv[1:2], axis=-1, keepdims=True)
    b = jnp.sum(lv[2:3] * lv[3:4], axis=-1, keepdims=True)
    return jnp.exp(a) - jnp.exp(b) + lam_init


def _diff_prompt_kernel(x_ref, p_ref, wq_ref, wk_ref, wv_ref, wg_ref, lam_ref, bias_ref, ng_ref,
                        wout_ref, lng_ref, lnb_ref, wgate_ref, wproj_ref,
                        y_ref, kout_ref, vout_ref,
                        k_scr, vt_scr, q_scr, o_scr, alpha_scr, *bufs,
                        lam_init, heads_per_iter):
    i = pl.program_id(1)
    tq = x_ref.shape[1]
    x = x_ref[0]
    xb = x.astype(BF16)
    q = _dot(xb, wq_ref[...]) * (DIFF_DH ** -0.5 * LOG2_E)
    k = _dot(xb, wk_ref[...])
    v = _dot(xb, wv_ref[...])
    _store_token_head_rows(kout_ref, (0,), k)
    _store_token_head_rows(vout_ref, (0,), v)
    first_half = lax.broadcasted_iota(jnp.int32, (DIFF_VD, tq), 0) < DIFF_DH
    row0 = pl.multiple_of(i * tq, tq)
    for h in range(DIFF_HEADS):
        hs = slice(h * DIFF_VD, (h + 1) * DIFF_VD)
        qh_t = q[:, hs].T
        q_scr[h, :, 0:tq] = jnp.where(first_half, qh_t, 0.0).astype(BF16)
        q_scr[h, :, tq:2 * tq] = jnp.where(first_half, 0.0, qh_t).astype(BF16)
        k_scr[h, pl.ds(row0, tq), :] = k[:, hs].astype(BF16)
        vt_scr[h, i, 0:DIFF_VD, :] = v[:, hs].T.astype(BF16)
        vt_scr[h, i, DIFF_VD:, :] = jnp.ones((SUM_ROWS, tq), BF16)
    lam = _diff_lambda(lam_ref, lam_init)

    hpi = heads_per_iter
    n_groups = DIFF_HEADS // hpi
    acc_scr = bufs[0:hpi]
    m_scr = bufs[hpi:2 * hpi]
    s_buf = [bufs[2 * hpi + 2 * u:2 * hpi + 2 * u + 2] for u in range(hpi)]
    p_buf = [bufs[4 * hpi + 2 * u:4 * hpi + 2 * u + 2] for u in range(hpi)]

    n_pairs = (i + 2) // 2
    steps_per_group = 2 * n_pairs

    def block_of(t):
        near = jnp.maximum(i - t, 0)
        if isinstance(t, int):
            return near if t < 2 else jnp.minimum(t - 2, i)
        return jnp.where(t < 2, near, jnp.minimum(t - 2, i))

    def scores(g, t, slot):
        rows = pl.ds(pl.multiple_of(block_of(t) * tq, tq), tq)
        for u in range(hpi):
            h = g * hpi + u
            s_buf[u][slot][:, 0:2 * tq] = _dot(k_scr[h, rows, :], q_scr[h])

    def softmax(g, t, slot, bias_slot):
        valid = t <= i
        for u in range(hpi):
            s = s_buf[u][slot][:, 0:2 * tq]
            if bias_slot is not None:
                tile = bias_ref[g * hpi + u, bias_slot]
                s = s + jnp.concatenate([tile, tile], axis=1)
            m_old = m_scr[u][g]
            m_new = jnp.where(valid, jnp.maximum(m_old, jnp.max(s, axis=0, keepdims=True)), m_old)
            alpha = jnp.exp2(m_old - m_new)
            pr = jnp.exp2(s - jnp.where(valid, m_new, -MASK_VALUE))
            m_scr[u][g] = m_new
            alpha_scr[u] = alpha
            p_buf[u][slot][:, 0:2 * tq] = pr.astype(BF16)

    def weighted_values(g, t, slot):
        j = block_of(t)
        for u in range(hpi):
            acc_scr[u][g] = (acc_scr[u][g] * alpha_scr[u]
                             + _dot(vt_scr[g * hpi + u, j], p_buf[u][slot][:, 0:2 * tq]))

    def pair(g, r, biased):
        first, last = r == 0, r == n_pairs - 1
        weighted_values(jnp.where(first, jnp.maximum(g - 1, 0), g),
                        jnp.where(first, steps_per_group - 1, 2 * r - 1), 1)
        scores(g, 2 * r + 1, 1)
        softmax(g, 2 * r, 0, 0 if biased else None)
        weighted_values(g, 2 * r, 0)
        scores(jnp.where(last, jnp.minimum(g + 1, n_groups - 1), g), jnp.where(last, 0, 2 * r + 2), 0)
        softmax(g, 2 * r + 1, 1, 1 if biased else None)

    for u in range(hpi):
        m_scr[u][...] = jnp.full(m_scr[u].shape, MASK_VALUE, F32)
        acc_scr[u][...] = jnp.zeros(acc_scr[u].shape, F32)
        p_buf[u][1][...] = jnp.zeros(p_buf[u][1].shape, BF16)
    alpha_scr[...] = jnp.ones(alpha_scr.shape, F32)
    scores(0, 0, 0)

    def two_pairs(d, carry):
        located = []
        for n in (2 * d, 2 * d + 1):
            g = n // n_pairs
            located.append((g, n - g * n_pairs))
        (g0, r0), (g1, r1) = located
        for near0 in (True, False):
            for near1 in (True, False):
                @pl.when(((r0 == 0) == near0) & ((r1 == 0) == near1))
                def _(near0=near0, near1=near1):
                    pair(g0, r0, near0)
                    pair(g1, r1, near1)
        return carry

    assert n_groups % 2 == 0
    lax.fori_loop(0, n_groups * n_pairs // 2, two_pairs, 0)
    weighted_values(n_groups - 1, steps_per_group - 1, 1)

    for h in range(DIFF_HEADS):
        g, u = divmod(h, hpi)
        inv_l = 1.0 / acc_scr[u][g, DIFF_VD:DIFF_VD + 1, :]
        acc = acc_scr[u][g, 0:DIFF_VD, :]
        o_t = acc[:, :tq] * inv_l[:, :tq] - lam * (acc[:, tq:] * inv_l[:, tq:])
        o_scr[h] = o_t.T
    o = jnp.concatenate([o_scr[h] for h in range(DIFF_HEADS)], axis=-1)
    gate = _dot(xb, wg_ref[...])
    on = _head_rmsnorm_gate(o, gate, ng_ref[...], DIFF_HEADS, DIFF_VD, 1.0 - lam_init)
    f = _dot(on.astype(BF16), wout_ref[...])
    y_ref[0] = _residual_update(x, f, p_ref[0], lng_ref[...], lnb_ref[...], wgate_ref[...], wproj_ref[...])


def _diff_weights(w_in, lam_q1, lam_k1, lam_q2, lam_k2, norm_g, w_out, ln_g, ln_b, w_gate, w_proj):
    w = DIFF_WIDTH
    return dict(
        wq=w_in[:, :w].astype(BF16), wk=w_in[:, w:2 * w].astype(BF16),
        wv=w_in[:, 2 * w:3 * w].astype(BF16), wg=w_in[:, 3 * w:].astype(BF16),
        lam=jnp.stack([lam_q1, lam_k1, lam_q2, lam_k2]).astype(F32),
        ng=norm_g.reshape(1, w), wout=w_out.astype(BF16),
        lng=ln_g.reshape(1, D_MODEL), lnb=ln_b.reshape(1, D_MODEL),
        wgate=w_gate.astype(BF16), wproj=w_proj.astype(BF16))


def diff_layer_prompt(x, p_all, layer, dw, rel_bias, lam_init, *, block=256, heads_per_iter=2):
    bsz, t, _ = x.shape
    assert t % block == 0 and block >= MAX_DISTANCE and DIFF_HEADS % heads_per_iter == 0
    n_groups = DIFF_HEADS // heads_per_iter
    kk = jnp.arange(block, dtype=jnp.int32)[:, None]
    qq = jnp.arange(block, dtype=jnp.int32)[None, :]
    dist = jnp.stack([qq - kk, block + qq - kk])
    far = rel_bias.astype(F32)[N_BUCKETS - 1].reshape(DIFF_HEADS, 1, 1, 1)
    bias = jnp.where(dist >= 0, (_t5_bias(rel_bias, dist) - far) * LOG2_E, MASK_VALUE)
    tok = lambda width: pl.BlockSpec((1, block, width), lambda b, i: (b, i, 0))
    kv_spec = pl.BlockSpec((1, block * DIFF_HEADS, DIFF_VD), lambda b, i: (b, i, 0))
    consts = [dw["wq"], dw["wk"], dw["wv"], dw["wg"], dw["lam"], bias, dw["ng"], dw["wout"],
              dw["lng"], dw["lnb"], dw["wgate"], dw["wproj"]]
    return pl.pallas_call(
        functools.partial(_diff_prompt_kernel, lam_init=lam_init, heads_per_iter=heads_per_iter),
        grid=(bsz, t // block),
        in_specs=[tok(D_MODEL), pl.BlockSpec((None, 1, block, PLE_DIM), lambda b, i: (layer, b, i, 0))]
                 + [_resident_spec(c.shape) for c in consts],
        out_specs=[tok(D_MODEL), kv_spec, kv_spec],
        out_shape=[jax.ShapeDtypeStruct((bsz, t, D_MODEL), F32),
                   jax.ShapeDtypeStruct((bsz, t * DIFF_HEADS, DIFF_VD), F32),
                   jax.ShapeDtypeStruct((bsz, t * DIFF_HEADS, DIFF_VD), F32)],
        scratch_shapes=[pltpu.VMEM((DIFF_HEADS, t, DIFF_VD), BF16),
                        pltpu.VMEM((DIFF_HEADS, t // block, DIFF_VD + SUM_ROWS, block), BF16),
                        pltpu.VMEM((DIFF_HEADS, DIFF_VD, 2 * block), BF16),
                        pltpu.VMEM((DIFF_HEADS, block, DIFF_VD), F32),
                        pltpu.VMEM((heads_per_iter, 1, 2 * block), F32),
                        *[pltpu.VMEM((n_groups, DIFF_VD + SUM_ROWS, 2 * block), F32)] * heads_per_iter,
                        *[pltpu.VMEM((n_groups, 1, 2 * block), F32)] * heads_per_iter,
                        *[pltpu.VMEM((block, 2 * block + LANES), F32)] * (2 * heads_per_iter),
                        *[pltpu.VMEM((block, 2 * block + LANES), BF16)] * (2 * heads_per_iter)],
        compiler_params=pltpu.CompilerParams(dimension_semantics=("arbitrary", "arbitrary"),
                                             vmem_limit_bytes=VMEM_LIMIT),
        name="diff_prompt",
    )(x, p_all, *consts)


def _diff_sample_project_kernel(x_ref, wq_ref, wk_ref, wv_ref, qt_ref, k_ref, v_ref):
    xb = x_ref[...].astype(BF16)
    qt_ref[...] = (_dot(xb, wq_ref[...]) * (DIFF_DH ** -0.5)).T.astype(BF16)
    _store_token_head_rows(k_ref, (), _dot(xb, wk_ref[...]))
    _store_token_head_rows(v_ref, (), _dot(xb, wv_ref[...]))


def _row_to_col(row, n):
    eye = lax.broadcasted_iota(jnp.int32, (n, n), 0) == lax.broadcasted_iota(jnp.int32, (n, n), 1)
    return jnp.sum(jnp.where(eye, row, 0.0), axis=1, keepdims=True)


def _paged_attn_kernel(pt_ref, qt_ref, kn_ref, vn_ref, lam_ref, bias_ref, bnew_ref, own_ref, ck_hbm, cv_hbm, o_ref,
                       k_ring, v_ring, sems, wq_scr, acc_scr, m_scr, l_scr, *, pages, seq, lam_init):
    b = pl.program_id(0)
    g = pl.program_id(1)
    n_steps = pl.num_programs(0) * pl.num_programs(1)
    step = b * pl.num_programs(1) + g
    ncol = 2 * DIFF_HEADS * seq

    def page_copies(s, slot):
        copies = []
        for i in range(pages):
            page = pt_ref[s * pages + i]
            copies.append(pltpu.make_async_copy(ck_hbm.at[page], k_ring.at[slot, i], sems.at[0, slot]))
            copies.append(pltpu.make_async_copy(cv_hbm.at[page], v_ring.at[slot, i], sems.at[1, slot]))
        return copies

    @pl.when(step == 0)
    def _():
        for ahead in range(PAGE_RING - 1):
            @pl.when(ahead < n_steps)
            def _():
                for c in page_copies(ahead, ahead):
                    c.start()

    nxt = step + PAGE_RING - 1

    @pl.when(nxt < n_steps)
    def _():
        for c in page_copies(nxt, lax.rem(nxt, PAGE_RING)):
            c.start()

    slot = lax.rem(step, PAGE_RING)
    for c in page_copies(step, slot):
        c.wait()
    k_pages = [k_ring.at[slot, i] for i in range(pages)]
    v_pages = [v_ring.at[slot, i] for i in range(pages)]

    @pl.when(g == 0)
    def _():
        local = (b % (LANES // seq)) * seq
        src = lax.broadcasted_iota(jnp.int32, (LANES, ncol), 0)
        col = lax.broadcasted_iota(jnp.int32, (LANES, ncol), 1)
        pick = jnp.where(src == local + col % seq, 1.0, 0.0).astype(BF16)
        rep = _dot(qt_ref[...], pick)
        wq_scr[...] = (rep * own_ref[...]).astype(BF16)
        m_scr[...] = jnp.full(m_scr.shape, MASK_VALUE, F32)
        l_scr[...] = jnp.zeros(l_scr.shape, F32)
        acc_scr[...] = jnp.zeros(acc_scr.shape, F32)

    def flash_step(scores, values):
        m_old = m_scr[...]
        m_new = m_old
        for s in scores:
            m_new = jnp.maximum(m_new, jnp.max(s, axis=0, keepdims=True))
        alpha = jnp.exp(m_old - m_new)
        l_new = alpha * l_scr[...]
        probs = []
        for s in scores:
            pr = jnp.exp(s - m_new)
            l_new = l_new + jnp.sum(pr, axis=0, keepdims=True)
            probs.append(pr.astype(BF16))
        pv = None
        for a in range(0, len(probs), 2):
            t = _dot_ta(jnp.concatenate(probs[a:a + 2], axis=0),
                        jnp.concatenate(values[a:a + 2], axis=0))
            pv = t if pv is None else pv + t
        acc_scr[...] = acc_scr[...] * _row_to_col(alpha, ncol) + pv
        l_scr[...] = l_new
        m_scr[...] = m_new

    def load_page(ref):
        return _load_token_head_rows(ref, (), PAGE_SIZE).astype(BF16)

    wq = wq_scr[...]
    scores, values = [], []
    for i in range(pages):
        s = _dot(load_page(k_pages[i]), wq)
        scores.append(s + bias_ref[0] if i == pages - 1 else s)
        values.append(load_page(v_pages[i]))
    flash_step(scores, values)

    @pl.when(g == pl.num_programs(1) - 1)
    def _():
        pad = jnp.zeros((16 - seq, DIFF_WIDTH), F32)
        kn = jnp.concatenate([_load_token_head_rows(kn_ref, (0,), seq), pad], axis=0).astype(BF16)
        vn = jnp.concatenate([_load_token_head_rows(vn_ref, (0,), seq), pad], axis=0).astype(BF16)
        flash_step([_dot(kn, wq) + bnew_ref[...]], [vn])
        lam = _diff_lambda(lam_ref, lam_init)
        inv_l = _row_to_col(1.0 / l_scr[...], ncol)
        half_rows = DIFF_HEADS * seq
        for h in range(DIFF_HEADS):
            cs = slice(h * DIFF_VD, (h + 1) * DIFF_VD)
            r1 = slice(h * seq, (h + 1) * seq)
            r2 = slice(half_rows + h * seq, half_rows + (h + 1) * seq)
            o_ref[0, :, cs] = acc_scr[r1, cs] * inv_l[r1] - lam * (acc_scr[r2, cs] * inv_l[r2])


def _diff_sample_out_kernel(x_ref, o_ref, p_ref, wg_ref, ng_ref, wout_ref, lng_ref, lnb_ref, wgate_ref, wproj_ref,
                            y_ref, *, lam_init):
    x = x_ref[...]
    gate = _dot(x.astype(BF16), wg_ref[...])
    on = _head_rmsnorm_gate(o_ref[...], gate, ng_ref[...], DIFF_HEADS, DIFF_VD, 1.0 - lam_init)
    f = _dot(on.astype(BF16), wout_ref[...])
    y_ref[...] = _residual_update(x, f, p_ref[...], lng_ref[...], lnb_ref[...], wgate_ref[...], wproj_ref[...])


def diff_layer_sample(x, p_all, layer, cache_k, cache_v, page_table, dw, rel_bias, lam_init, *, pages=8, block=256):
    n, seq, _ = x.shape
    n_pages = page_table.shape[1]
    rows = n * seq
    assert n_pages % pages == 0 and rows % block == 0 and LANES % seq == 0 and seq <= 16
    assert PAGE_SIZE >= MAX_DISTANCE
    x2 = x.reshape(rows, D_MODEL)
    row_spec = lambda width: pl.BlockSpec((block, width), lambda i: (i, 0))
    qt, kn, vn = pl.pallas_call(
        _diff_sample_project_kernel,
        grid=(rows // block,),
        in_specs=[row_spec(D_MODEL)] + [_const_spec((D_MODEL, DIFF_WIDTH))] * 3,
        out_specs=[pl.BlockSpec((DIFF_WIDTH, block), lambda i: (0, i)),
                   pl.BlockSpec((block * DIFF_HEADS, DIFF_VD), lambda i: (i, 0)),
                   pl.BlockSpec((block * DIFF_HEADS, DIFF_VD), lambda i: (i, 0))],
        out_shape=[jax.ShapeDtypeStruct((DIFF_WIDTH, rows), BF16),
                   jax.ShapeDtypeStruct((rows * DIFF_HEADS, DIFF_VD), F32),
                   jax.ShapeDtypeStruct((rows * DIFF_HEADS, DIFF_VD), F32)],
        compiler_params=pltpu.CompilerParams(dimension_semantics=("arbitrary",), vmem_limit_bytes=VMEM_LIMIT),
        name="diff_sample_project",
    )(x2, dw["wq"], dw["wk"], dw["wv"])

    ncol = 2 * DIFF_HEADS * seq
    col = jnp.arange(ncol, dtype=jnp.int32)[None, :]
    col_h = (col % (DIFF_HEADS * seq)) // seq
    col_t = col % seq
    kk = jnp.arange(PAGE_SIZE, dtype=jnp.int32)[:, None]
    tk = jnp.arange(16, dtype=jnp.int32)[:, None]
    dist_last = PAGE_SIZE + col_t - kk
    dist_new = col_t - tk
    far = rel_bias.astype(F32)[N_BUCKETS - 1].reshape(DIFF_HEADS, 1, 1)

    def own_head(per_head):
        return sum(jnp.where(col_h == h, per_head[h], 0.0) for h in range(DIFF_HEADS))

    bias_last = own_head(_t5_bias(rel_bias, dist_last) - far)
    bias_pages = jnp.stack([jnp.zeros_like(bias_last), bias_last])
    bias_new = jnp.where((tk < seq) & (dist_new >= 0), own_head(_t5_bias(rel_bias, dist_new) - far), MASK_VALUE)
    feat = jnp.arange(DIFF_WIDTH, dtype=jnp.int32)[:, None]
    own = ((feat // DIFF_VD == col_h) & ((feat % DIFF_VD) // DIFF_DH == col // (DIFF_HEADS * seq))).astype(F32)

    n_groups = n_pages // pages
    seq_per_blk = LANES // seq
    page_shape = (PAGE_RING, pages, PAGE_SIZE * DIFF_HEADS, DIFF_VD)

    tok_spec = pl.BlockSpec((1, seq, DIFF_WIDTH), lambda b, g, pt: (b, 0, 0))
    new_spec = pl.BlockSpec((1, seq * DIFF_HEADS, DIFF_VD), lambda b, g, pt: (b, 0, 0))
    grid_spec = pltpu.PrefetchScalarGridSpec(
        num_scalar_prefetch=1,
        grid=(n, n_groups),
        in_specs=[pl.BlockSpec((DIFF_WIDTH, LANES), lambda b, g, pt: (0, b // seq_per_blk)),
                  new_spec, new_spec,
                  pl.BlockSpec((4, DIFF_DH), lambda b, g, pt: (0, 0)),
                  pl.BlockSpec((1, PAGE_SIZE, ncol), lambda b, g, pt: ((g + 1) // n_groups, 0, 0)),
                  pl.BlockSpec((16, ncol), lambda b, g, pt: (0, 0)),
                  pl.BlockSpec((DIFF_WIDTH, ncol), lambda b, g, pt: (0, 0)),
                  pl.BlockSpec(memory_space=pl.ANY), pl.BlockSpec(memory_space=pl.ANY)],
        out_specs=tok_spec,
        scratch_shapes=[pltpu.VMEM(page_shape, F32), pltpu.VMEM(page_shape, F32),
                        pltpu.SemaphoreType.DMA((2, PAGE_RING)),
                        pltpu.VMEM((DIFF_WIDTH, ncol), BF16),
                        pltpu.VMEM((ncol, DIFF_WIDTH), F32),
                        pltpu.VMEM((1, ncol), F32),
                        pltpu.VMEM((1, ncol), F32)])
    o = pl.pallas_call(
        functools.partial(_paged_attn_kernel, pages=pages, seq=seq, lam_init=lam_init),
        grid_spec=grid_spec,
        out_shape=jax.ShapeDtypeStruct((n, seq, DIFF_WIDTH), F32),
        compiler_params=pltpu.CompilerParams(dimension_semantics=("arbitrary", "arbitrary"),
                                             vmem_limit_bytes=VMEM_LIMIT),
        name="diff_sample_attn",
    )(page_table.reshape(-1), qt, kn.reshape(n, seq * DIFF_HEADS, DIFF_VD), vn.reshape(n, seq * DIFF_HEADS, DIFF_VD),
      dw["lam"], bias_pages, bias_new, own, cache_k, cache_v)

    consts = [dw["wg"], dw["ng"], dw["wout"], dw["lng"], dw["lnb"], dw["wgate"], dw["wproj"]]
    y = pl.pallas_call(
        functools.partial(_diff_sample_out_kernel, lam_init=lam_init),
        grid=(rows // block,),
        in_specs=[row_spec(D_MODEL), row_spec(DIFF_WIDTH),
                  pl.BlockSpec((None, block, PLE_DIM), lambda i: (layer, i, 0))] + [_const_spec(c.shape) for c in consts],
        out_specs=row_spec(D_MODEL),
        out_shape=jax.ShapeDtypeStruct((rows, D_MODEL), F32),
        compiler_params=pltpu.CompilerParams(dimension_semantics=("arbitrary",), vmem_limit_bytes=VMEM_LIMIT),
        name="diff_sample_out",
    )(x2, o.reshape(rows, DIFF_WIDTH), p_all.reshape(-1, rows, PLE_DIM), *consts)
    return y.reshape(n, seq, D_MODEL), kn, vn


def kernel(x_prompt, x_sample, state_gla, cache_k, cache_v, page_table, p_prompt, p_sample, rel_bias,
           gla_w_in, gla_w_a2, gla_b_a, gla_norm_g, gla_w_out,
           diff_w_in, diff_lam_q1, diff_lam_k1, diff_lam_q2, diff_lam_k2, diff_norm_g, diff_w_out,
           ln_g, ln_b, ple_w_proj, ple_w_gate):
    w0 = _gla_weights(gla_w_in[0], gla_w_a2[0], gla_b_a[0], gla_norm_g[0], gla_w_out[0],
                      ln_g[0], ln_b[0], ple_w_gate[0], ple_w_proj[0])
    xp1, sp = gla_layer_prompt(x_prompt, p_prompt, 0, w0)
    xs1, ss = gla_layer_sample(x_sample, p_sample, 0, state_gla[0], w0)
    lam_init = 0.8 - 0.6 * math.exp(-0.3 * 1)
    dw = _diff_weights(diff_w_in[0], diff_lam_q1[0], diff_lam_k1[0], diff_lam_q2[0], diff_lam_k2[0],
                       diff_norm_g[0], diff_w_out[0], ln_g[1], ln_b[1], ple_w_gate[1], ple_w_proj[1])
    yp, kp, vp = diff_layer_prompt(xp1, p_prompt, 1, dw, rel_bias, lam_init)
    pool = cache_k.shape[1]
    ys, ks, vs = diff_layer_sample(xs1, p_sample, 1,
                                   cache_k[0].reshape(pool, PAGE_SIZE * DIFF_HEADS, DIFF_VD),
                                   cache_v[0].reshape(pool, PAGE_SIZE * DIFF_HEADS, DIFF_VD),
                                   page_table, dw, rel_bias, lam_init)
    bsz, t, _ = x_prompt.shape
    n, seq, _ = x_sample.shape
    heads = (DIFF_HEADS, DIFF_VD)
    return (yp, ys, sp[None], ss[None],
            kp.reshape(1, bsz, t, *heads), vp.reshape(1, bsz, t, *heads),
            ks.reshape(1, n, seq, *heads), vs.reshape(1, n, seq, *heads))
```

```python
import functools
import math

import jax
import jax.numpy as jnp
from jax import lax
from jax.experimental import pallas as pl
from jax.experimental.pallas import tpu as pltpu

F32 = jnp.float32
BF16 = jnp.bfloat16

D_MODEL = 1024
DEPTH = 2
GLA_HEADS = 4
GLA_DK = 128
GLA_DV = 256
GLA_HK = GLA_HEADS * GLA_DK
GLA_HV = GLA_HEADS * GLA_DV
GLA_LOWRANK = 16
GLA_TAU = 16.0
DIFF_HEADS = 8
DIFF_DH = 64
DIFF_VD = 128
DIFF_WIDTH = DIFF_HEADS * DIFF_VD
N_BUCKETS = 32
MAX_DISTANCE = 128
PLE_DIM = 256
PAGE_SIZE = 128
ALPHA = (2 * DEPTH) ** 0.25
EPS = 1e-5

LANES = 128
GLA_CHUNK = 64
VMEM_LIMIT = 56 * 1024 * 1024
MASK_VALUE = -1e30
LOG2_E = math.log2(math.e)
SUM_ROWS = 16
PAGE_RING = 4


def _dot(a, b):
    return jnp.dot(a, b, preferred_element_type=F32)


def _dot_tb(a, b):
    return lax.dot_general(a, b, (((1,), (1,)), ((), ())), preferred_element_type=F32)


def _dot_ta(a, b):
    return lax.dot_general(a, b, (((0,), (0,)), ((), ())), preferred_element_type=F32)


def _split_bf16(x):
    hi = x.astype(BF16)
    mid = (x - hi.astype(F32)).astype(BF16)
    return hi, mid


def _log_sigmoid(z):
    return jnp.minimum(z, 0.0) - jnp.log(1.0 + jnp.exp(-jnp.abs(z)))


def _sigmoid(z):
    return 1.0 / (1.0 + jnp.exp(-z))


def _head_rmsnorm_gate(o, gate, norm_g, n_heads, head_dim, scale):
    parts = []
    for h in range(n_heads):
        oh = o[:, h * head_dim:(h + 1) * head_dim]
        ms = jnp.mean(oh * oh, axis=-1, keepdims=True)
        parts.append(oh * lax.rsqrt(ms + EPS))
    on = jnp.concatenate(parts, axis=-1) * norm_g
    if scale != 1.0:
        on = on * scale
    return on * (gate * _sigmoid(gate))


def _residual_update(x, f, p, ln_g, ln_b, w_gate, w_proj):
    hp = ALPHA * x + f
    mu = jnp.mean(hp, axis=-1, keepdims=True)
    hc = hp - mu
    var = jnp.mean(hc * hc, axis=-1, keepdims=True)
    h = hc * lax.rsqrt(var + EPS) * ln_g + ln_b
    gate = _sigmoid(_dot(h.astype(BF16), w_gate))
    return h + gate * _dot(p.astype(BF16), w_proj)


def _gla_project(xb, wq, wk, wv, wg, wa, wa2, ba):
    q = _dot(xb, wq) * (GLA_DK ** -0.5)
    k = _dot(xb, wk)
    v = _dot(xb, wv)
    g = _dot(xb, wg)
    a_lr = _dot(xb, wa)
    z = _dot(a_lr.astype(BF16), wa2) + ba
    log_a = _log_sigmoid(z) * (1.0 / GLA_TAU)
    return q, k, v, g, log_a


def _chunk_decay_matrices(rows, chunk):
    half = chunk // 2
    i = lax.broadcasted_iota(jnp.int32, (rows, rows), 0)
    j = lax.broadcasted_iota(jnp.int32, (rows, rows), 1)
    same = (i // chunk) == (j // chunk)
    jl = j % chunk
    il = i % chunk
    pos = same & (jl >= half) & (jl <= il)
    neg = same & (jl < half) & (jl > il)
    rel = jnp.where(pos, 1.0, jnp.where(neg, -1.0, 0.0)).astype(BF16)
    n_sel = max(8, 2 * rows // chunk)
    s = lax.broadcasted_iota(jnp.int32, (n_sel, rows), 0)
    t = lax.broadcasted_iota(jnp.int32, (n_sel, rows), 1)
    halves = jnp.where((t // half) == s, 1.0, 0.0).astype(BF16)
    return rel, halves


def _gla_prompt_kernel(x_ref, p_ref, wq_ref, wk_ref, wv_ref, wg_ref, wa_ref, wa2_ref, ba_ref, ng_ref,
                       wout_ref, lng_ref, lnb_ref, wgate_ref, wproj_ref,
                       y_ref, st_ref, s_scr, o_scr, *, chunk):
    blk = pl.program_id(1)
    rows = o_scr.shape[1]

    @pl.when(blk == 0)
    def _():
        s_scr[...] = jnp.zeros_like(s_scr)

    ci = lax.broadcasted_iota(jnp.int32, (rows, rows), 0)
    cj = lax.broadcasted_iota(jnp.int32, (rows, rows), 1)
    causal = (ci >= cj) & ((ci // chunk) == (cj // chunk))
    rel, halves = _chunk_decay_matrices(rows, chunk)
    for sb in range(x_ref.shape[1] // rows):
        tok = slice(sb * rows, (sb + 1) * rows)
        x = x_ref[0, tok, :]
        xb = x.astype(BF16)
        q, k, v, g, log_a = _gla_project(xb, wq_ref[...], wk_ref[...], wv_ref[...], wg_ref[...],
                                         wa_ref[...], wa2_ref[...], ba_ref[...])
        la_hi, la_mid = _split_bf16(log_a)
        d = _dot(rel, la_hi) + _dot(rel, la_mid)
        hs = _dot(halves, la_hi) + _dot(halves, la_mid)
        ehs = jnp.exp(hs)
        q_dec = (q * jnp.exp(d)).astype(BF16)
        k_inv = (k * jnp.exp(-d)).astype(BF16)
        vb = v.astype(BF16)

        for h in range(GLA_HEADS):
            ks = slice(h * GLA_DK, (h + 1) * GLA_DK)
            vs = slice(h * GLA_DV, (h + 1) * GLA_DV)
            att = jnp.where(causal, _dot_tb(q_dec[:, ks], k_inv[:, ks]), 0.0).astype(BF16)
            o_scr[sb, :, vs] = _dot(att, vb[:, vs])

        for c in range(rows // chunk):
            r0 = c * chunk
            for h in range(GLA_HEADS):
                ks = slice(h * GLA_DK, (h + 1) * GLA_DK)
                vs = slice(h * GLA_DV, (h + 1) * GLA_DV)
                qd = q_dec[r0:r0 + chunk, ks]
                ki = k_inv[r0:r0 + chunk, ks]
                vh = vb[r0:r0 + chunk, vs]
                e_first = ehs[2 * c:2 * c + 1, ks]
                e_second = ehs[2 * c + 1:2 * c + 2, ks]
                s_mid = s_scr[h] * e_first
                o_scr[sb, r0:r0 + chunk, vs] += _dot_tb(qd, s_mid.astype(BF16))
                s_scr[h] = (s_mid + _dot_ta(vh, ki)) * e_second

        on = _head_rmsnorm_gate(o_scr[sb], g, ng_ref[...], GLA_HEADS, GLA_DV, 1.0)
        f = _dot(on.astype(BF16), wout_ref[...])
        y_ref[0, tok, :] = _residual_update(x, f, p_ref[0, tok, :], lng_ref[...], lnb_ref[...],
                                            wgate_ref[...], wproj_ref[...])

    @pl.when(blk == pl.num_programs(1) - 1)
    def _():
        for h in range(GLA_HEADS):
            st_ref[0, h] = s_scr[h].T


def _const_spec(shape):
    nd = len(shape)
    return pl.BlockSpec(shape, lambda *_: (0,) * nd)


def _gla_weights(w_in, w_a2, b_a, norm_g, w_out, ln_g, ln_b, w_gate, w_proj):
    hk, hv = GLA_HK, GLA_HV
    wq = w_in[:, :hk].astype(BF16)
    wk = w_in[:, hk:2 * hk].astype(BF16)
    wv = w_in[:, 2 * hk:2 * hk + hv].astype(BF16)
    wg = w_in[:, 2 * hk + hv:2 * hk + 2 * hv].astype(BF16)
    wa = jnp.pad(w_in[:, 2 * hk + 2 * hv:], ((0, 0), (0, LANES - GLA_LOWRANK))).astype(BF16)
    wa2 = jnp.pad(w_a2, ((0, LANES - GLA_LOWRANK), (0, 0))).astype(BF16)
    return (wq, wk, wv, wg, wa, wa2, b_a.reshape(1, hk), norm_g.reshape(1, hv),
            w_out.astype(BF16), ln_g.reshape(1, D_MODEL), ln_b.reshape(1, D_MODEL),
            w_gate.astype(BF16), w_proj.astype(BF16))


def gla_layer_prompt(x, p_all, layer, weights, *, block=512, sub_block=256, chunk=GLA_CHUNK):
    bsz, t, _ = x.shape
    assert t % block == 0 and block % sub_block == 0 and sub_block % chunk == 0
    in_specs = [pl.BlockSpec((1, block, D_MODEL), lambda b, i: (b, i, 0)),
                pl.BlockSpec((None, 1, block, PLE_DIM), lambda b, i: (layer, b, i, 0))]
    in_specs += [_const_spec(w.shape) for w in weights]
    return pl.pallas_call(
        functools.partial(_gla_prompt_kernel, chunk=chunk),
        grid=(bsz, t // block),
        in_specs=in_specs,
        out_specs=[pl.BlockSpec((1, block, D_MODEL), lambda b, i: (b, i, 0)),
                   pl.BlockSpec((1, GLA_HEADS, GLA_DK, GLA_DV), lambda b, i: (b, 0, 0, 0))],
        out_shape=[jax.ShapeDtypeStruct((bsz, t, D_MODEL), F32),
                   jax.ShapeDtypeStruct((bsz, GLA_HEADS, GLA_DK, GLA_DV), F32)],
        scratch_shapes=[pltpu.VMEM((GLA_HEADS, GLA_DV, GLA_DK), F32),
                        pltpu.VMEM((block // sub_block, sub_block, GLA_HV), F32)],
        compiler_params=pltpu.CompilerParams(dimension_semantics=("arbitrary", "arbitrary"),
                                             vmem_limit_bytes=VMEM_LIMIT),
        name="gla_prompt",
    )(x, p_all, *weights)


def _gla_sample_kernel(x_ref, p_ref, s0_ref, wq_ref, wk_ref, wv_ref, wg_ref, wa_ref, wa2_ref, ba_ref, ng_ref,
                       wout_ref, lng_ref, lnb_ref, wgate_ref, wproj_ref,
                       y_ref, st_ref, o_scr, *, seq):
    rows = x_ref.shape[0]
    n_seq = rows // seq
    half = seq // 2
    x = x_ref[...]
    xb = x.astype(BF16)
    q, k, v, g, log_a = _gla_project(xb, wq_ref[...], wk_ref[...], wv_ref[...], wg_ref[...],
                                     wa_ref[...], wa2_ref[...], ba_ref[...])
    rel, _ = _chunk_decay_matrices(rows, seq)
    la_hi, la_mid = _split_bf16(log_a)
    d = _dot(rel, la_hi) + _dot(rel, la_mid)
    s_i = lax.broadcasted_iota(jnp.int32, (n_seq, rows), 0)
    t_i = lax.broadcasted_iota(jnp.int32, (n_seq, rows), 1)
    in_seq = (t_i // seq) == s_i
    sel_first = jnp.where(in_seq & ((t_i % seq) < half), 1.0, 0.0).astype(BF16)
    sel_second = jnp.where(in_seq & ((t_i % seq) >= half), 1.0, 0.0).astype(BF16)
    t_c = lax.broadcasted_iota(jnp.int32, (rows, n_seq), 0)
    s_c = lax.broadcasted_iota(jnp.int32, (rows, n_seq), 1)
    sel_tot_t = jnp.where((t_c // seq) == s_c, 1.0, 0.0).astype(BF16)
    e_first = jnp.exp(_dot(sel_first, la_hi) + _dot(sel_first, la_mid))
    e_second = jnp.exp(_dot(sel_second, la_hi) + _dot(sel_second, la_mid))
    e_tot_col = jnp.exp(_dot_ta(la_hi, sel_tot_t) + _dot_ta(la_mid, sel_tot_t))
    q_dec = q * jnp.exp(d)
    k_inv = k * jnp.exp(-d)

    ci = lax.broadcasted_iota(jnp.int32, (rows, rows), 0)
    cj = lax.broadcasted_iota(jnp.int32, (rows, rows), 1)
    causal = (ci >= cj) & ((ci // seq) == (cj // seq))
    q_dec_b = q_dec.astype(BF16)
    k_inv_b = k_inv.astype(BF16)
    for h in range(GLA_HEADS):
        ks = slice(h * GLA_DK, (h + 1) * GLA_DK)
        vs = slice(h * GLA_DV, (h + 1) * GLA_DV)
        att = jnp.where(causal, _dot_tb(q_dec_b[:, ks], k_inv_b[:, ks]), 0.0).astype(BF16)
        o_scr[:, vs] = _dot(att, v[:, vs].astype(BF16))

    for c in range(n_seq):
        r0 = c * seq
        for h in range(GLA_HEADS):
            ks = slice(h * GLA_DK, (h + 1) * GLA_DK)
            vs = slice(h * GLA_DV, (h + 1) * GLA_DV)
            qd = q_dec[r0:r0 + seq, ks]
            ki = k_inv[r0:r0 + seq, ks]
            vh = v[r0:r0 + seq, vs].astype(BF16)
            s_old = s0_ref[c, h]
            q_mid = (qd * e_first[c:c + 1, ks]).astype(BF16)
            o_scr[r0:r0 + seq, vs] += _dot(q_mid, s_old.astype(BF16))
            k_end = (ki * e_second[c:c + 1, ks]).astype(BF16)
            st_ref[c, h] = s_old * e_tot_col[h * GLA_DK:(h + 1) * GLA_DK, c:c + 1] + _dot_ta(k_end, vh)

    on = _head_rmsnorm_gate(o_scr[...], g, ng_ref[...], GLA_HEADS, GLA_DV, 1.0)
    f = _dot(on.astype(BF16), wout_ref[...])
    y_ref[...] = _residual_update(x, f, p_ref[...], lng_ref[...], lnb_ref[...], wgate_ref[...], wproj_ref[...])


def gla_layer_sample(x, p_all, layer, s0, weights, *, group=16):
    n, seq, _ = x.shape
    assert n % group == 0 and seq % 2 == 0
    rows = group * seq
    in_specs = [pl.BlockSpec((rows, D_MODEL), lambda i: (i, 0)),
                pl.BlockSpec((None, rows, PLE_DIM), lambda i: (layer, i, 0)),
                pl.BlockSpec((group, GLA_HEADS, GLA_DK, GLA_DV), lambda i: (i, 0, 0, 0))]
    in_specs += [_resident_spec(w.shape) for w in weights]
    y, st = pl.pallas_call(
        functools.partial(_gla_sample_kernel, seq=seq),
        grid=(n // group,),
        in_specs=in_specs,
        out_specs=[pl.BlockSpec((rows, D_MODEL), lambda i: (i, 0)),
                   pl.BlockSpec((group, GLA_HEADS, GLA_DK, GLA_DV), lambda i: (i, 0, 0, 0))],
        out_shape=[jax.ShapeDtypeStruct((n * seq, D_MODEL), F32),
                   jax.ShapeDtypeStruct((n, GLA_HEADS, GLA_DK, GLA_DV), F32)],
        scratch_shapes=[pltpu.VMEM((rows, GLA_HV), F32)],
        compiler_params=pltpu.CompilerParams(dimension_semantics=("arbitrary",),
                                             vmem_limit_bytes=VMEM_LIMIT),
        name="gla_sample",
    )(x.reshape(n * seq, D_MODEL), p_all.reshape(-1, n * seq, PLE_DIM), s0, *weights)
    return y.reshape(n, seq, D_MODEL), st


def _resident_spec(shape):
    nd = len(shape)
    return pl.BlockSpec(shape, lambda *_: (0,) * nd, pipeline_mode=pl.Buffered(1))


def _t5_bias(rel_bias, dist):
    max_exact = N_BUCKETS // 2
    n = jnp.maximum(dist, 0)[None]
    nf = jnp.maximum(n, 1).astype(F32)
    steps = jnp.log(nf / max_exact) / math.log(MAX_DISTANCE / max_exact) * (N_BUCKETS - max_exact)
    table = rel_bias.astype(F32)
    per_head = (DIFF_HEADS,) + (1,) * dist.ndim
    out = jnp.broadcast_to(table[N_BUCKETS - 1].reshape(per_head), (DIFF_HEADS,) + dist.shape)
    for b in range(N_BUCKETS - 2, max_exact - 1, -1):
        out = jnp.where(steps < (b + 1 - max_exact), table[b].reshape(per_head), out)
    for b in range(max_exact):
        out = jnp.where(n == b, table[b].reshape(per_head), out)
    return out


def _store_token_head_rows(ref, lead, x):
    tokens = x.shape[0]
    for h in range(DIFF_HEADS):
        ref[(*lead, pl.ds(h, tokens, stride=DIFF_HEADS), slice(None))] = x[:, h * DIFF_VD:(h + 1) * DIFF_VD]


def _load_token_head_rows(ref, lead, tokens):
    return jnp.concatenate(
        [ref[(*lead, pl.ds(h, tokens, stride=DIFF_HEADS), slice(None))] for h in range(DIFF_HEADS)], axis=1)


def _diff_lambda(lam_ref, lam_init):
    lv = lam_ref[...]
    a = jnp.sum(lv[0:1] * lv[1:2], axis=-1, keepdims=True)
    b = jnp.sum(lv[2:3] * lv[3:4], axis=-1, keepdims=True)
    return jnp.exp(a) - jnp.exp(b) + lam_init


def _diff_prompt_kernel(x_ref, p_ref, wq_ref, wk_ref, wv_ref, wg_ref, lam_ref, bias_ref, ng_ref,
                        wout_ref, lng_ref, lnb_ref, wgate_ref, wproj_ref,
                        y_ref, kout_ref, vout_ref,
                        k_scr, vt_scr, q_scr, o_scr, alpha_scr, *bufs,
                        lam_init, heads_per_iter):
    i = pl.program_id(1)
    tq = x_ref.shape[1]
    x = x_ref[0]
    xb = x.astype(BF16)
    q = _dot(xb, wq_ref[...]) * (DIFF_DH ** -0.5 * LOG2_E)
    k = _dot(xb, wk_ref[...])
    v = _dot(xb, wv_ref[...])
    _store_token_head_rows(kout_ref, (0,), k)
    _store_token_head_rows(vout_ref, (0,), v)
    first_half = lax.broadcasted_iota(jnp.int32, (DIFF_VD, tq), 0) < DIFF_DH
    row0 = pl.multiple_of(i * tq, tq)
    for h in range(DIFF_HEADS):
        hs = slice(h * DIFF_VD, (h + 1) * DIFF_VD)
        qh_t = q[:, hs].T
        q_scr[h, :, 0:tq] = jnp.where(first_half, qh_t, 0.0).astype(BF16)
        q_scr[h, :, tq:2 * tq] = jnp.where(first_half, 0.0, qh_t).astype(BF16)
        k_scr[h, pl.ds(row0, tq), :] = k[:, hs].astype(BF16)
        vt_scr[h, i, 0:DIFF_VD, :] = v[:, hs].T.astype(BF16)
        vt_scr[h, i, DIFF_VD:, :] = jnp.ones((SUM_ROWS, tq), BF16)
    lam = _diff_lambda(lam_ref, lam_init)

    hpi = heads_per_iter
    n_groups = DIFF_HEADS // hpi
    acc_scr = bufs[0:hpi]
    m_scr = bufs[hpi:2 * hpi]
    s_buf = [bufs[2 * hpi + 2 * u:2 * hpi + 2 * u + 2] for u in range(hpi)]
    p_buf = [bufs[4 * hpi + 2 * u:4 * hpi + 2 * u + 2] for u in range(hpi)]

    n_pairs = (i + 2) // 2
    steps_per_group = 2 * n_pairs

    def block_of(t):
        near = jnp.maximum(i - t, 0)
        if isinstance(t, int):
            return near if t < 2 else jnp.minimum(t - 2, i)
        return jnp.where(t < 2, near, jnp.minimum(t - 2, i))

    def scores(g, t, slot):
        rows = pl.ds(pl.multiple_of(block_of(t) * tq, tq), tq)
        for u in range(hpi):
            h = g * hpi + u
            s_buf[u][slot][:, 0:2 * tq] = _dot(k_scr[h, rows, :], q_scr[h])

    def softmax(g, t, slot, bias_slot):
        valid = t <= i
        for u in range(hpi):
            s = s_buf[u][slot][:, 0:2 * tq]
            if bias_slot is not None:
                tile = bias_ref[g * hpi + u, bias_slot]
                s = s + jnp.concatenate([tile, tile], axis=1)
            m_old = m_scr[u][g]
            m_new = jnp.where(valid, jnp.maximum(m_old, jnp.max(s, axis=0, keepdims=True)), m_old)
            alpha = jnp.exp2(m_old - m_new)
            pr = jnp.exp2(s - jnp.where(valid, m_new, -MASK_VALUE))
            m_scr[u][g] = m_new
            alpha_scr[u] = alpha
            p_buf[u][slot][:, 0:2 * tq] = pr.astype(BF16)

    def weighted_values(g, t, slot):
        j = block_of(t)
        for u in range(hpi):
            acc_scr[u][g] = (acc_scr[u][g] * alpha_scr[u]
                             + _dot(vt_scr[g * hpi + u, j], p_buf[u][slot][:, 0:2 * tq]))

    def pair(g, r, biased):
        first, last = r == 0, r == n_pairs - 1
        weighted_values(jnp.where(first, jnp.maximum(g - 1, 0), g),
                        jnp.where(first, steps_per_group - 1, 2 * r - 1), 1)
        scores(g, 2 * r + 1, 1)
        softmax(g, 2 * r, 0, 0 if biased else None)
        weighted_values(g, 2 * r, 0)
        scores(jnp.where(last, jnp.minimum(g + 1, n_groups - 1), g), jnp.where(last, 0, 2 * r + 2), 0)
        softmax(g, 2 * r + 1, 1, 1 if biased else None)

    for u in range(hpi):
        m_scr[u][...] = jnp.full(m_scr[u].shape, MASK_VALUE, F32)
        acc_scr[u][...] = jnp.zeros(acc_scr[u].shape, F32)
        p_buf[u][1][...] = jnp.zeros(p_buf[u][1].shape, BF16)
    alpha_scr[...] = jnp.ones(alpha_scr.shape, F32)
    scores(0, 0, 0)

    def two_pairs(d, carry):
        located = []
        for n in (2 * d, 2 * d + 1):
            g = n // n_pairs
            located.append((g, n - g * n_pairs))
        (g0, r0), (g1, r1) = located
        for near0 in (True, False):
            for near1 in (True, False):
                @pl.when(((r0 == 0) == near0) & ((r1 == 0) == near1))
                def _(near0=near0, near1=near1):
                    pair(g0, r0, near0)
                    pair(g1, r1, near1)
        return carry

    assert n_groups % 2 == 0
    lax.fori_loop(0, n_groups * n_pairs // 2, two_pairs, 0)
    weighted_values(n_groups - 1, steps_per_group - 1, 1)

    for h in range(DIFF_HEADS):
        g, u = divmod(h, hpi)
        inv_l = 1.0 / acc_scr[u][g, DIFF_VD:DIFF_VD + 1, :]
        acc = acc_scr[u][g, 0:DIFF_VD, :]
        o_t = acc[:, :tq] * inv_l[:, :tq] - lam * (acc[:, tq:] * inv_l[:, tq:])
        o_scr[h] = o_t.T
    o = jnp.concatenate([o_scr[h] for h in range(DIFF_HEADS)], axis=-1)
    gate = _dot(xb, wg_ref[...])
    on = _head_rmsnorm_gate(o, gate, ng_ref[...], DIFF_HEADS, DIFF_VD, 1.0 - lam_init)
    f = _dot(on.astype(BF16), wout_ref[...])
    y_ref[0] = _residual_update(x, f, p_ref[0], lng_ref[...], lnb_ref[...], wgate_ref[...], wproj_ref[...])


def _diff_weights(w_in, lam_q1, lam_k1, lam_q2, lam_k2, norm_g, w_out, ln_g, ln_b, w_gate, w_proj):
    w = DIFF_WIDTH
    return dict(
        wq=w_in[:, :w].astype(BF16), wk=w_in[:, w:2 * w].astype(BF16),
        wv=w_in[:, 2 * w:3 * w].astype(BF16), wg=w_in[:, 3 * w:].astype(BF16),
        lam=jnp.stack([lam_q1, lam_k1, lam_q2, lam_k2]).astype(F32),
        ng=norm_g.reshape(1, w), wout=w_out.astype(BF16),
        lng=ln_g.reshape(1, D_MODEL), lnb=ln_b.reshape(1, D_MODEL),
        wgate=w_gate.astype(BF16), wproj=w_proj.astype(BF16))


def diff_layer_prompt(x, p_all, layer, dw, rel_bias, lam_init, *, block=256, heads_per_iter=2):
    bsz, t, _ = x.shape
    assert t % block == 0 and block >= MAX_DISTANCE and DIFF_HEADS % heads_per_iter == 0
    n_groups = DIFF_HEADS // heads_per_iter
    kk = jnp.arange(block, dtype=jnp.int32)[:, None]
    qq = jnp.arange(block, dtype=jnp.int32)[None, :]
    dist = jnp.stack([qq - kk, block + qq - kk])
    far = rel_bias.astype(F32)[N_BUCKETS - 1].reshape(DIFF_HEADS, 1, 1, 1)
    bias = jnp.where(dist >= 0, (_t5_bias(rel_bias, dist) - far) * LOG2_E, MASK_VALUE)
    tok = lambda width: pl.BlockSpec((1, block, width), lambda b, i: (b, i, 0))
    kv_spec = pl.BlockSpec((1, block * DIFF_HEADS, DIFF_VD), lambda b, i: (b, i, 0))
    consts = [dw["wq"], dw["wk"], dw["wv"], dw["wg"], dw["lam"], bias, dw["ng"], dw["wout"],
              dw["lng"], dw["lnb"], dw["wgate"], dw["wproj"]]
    return pl.pallas_call(
        functools.partial(_diff_prompt_kernel, lam_init=lam_init, heads_per_iter=heads_per_iter),
        grid=(bsz, t // block),
        in_specs=[tok(D_MODEL), pl.BlockSpec((None, 1, block, PLE_DIM), lambda b, i: (layer, b, i, 0))]
                 + [_resident_spec(c.shape) for c in consts],
        out_specs=[tok(D_MODEL), kv_spec, kv_spec],
        out_shape=[jax.ShapeDtypeStruct((bsz, t, D_MODEL), F32),
                   jax.ShapeDtypeStruct((bsz, t * DIFF_HEADS, DIFF_VD), F32),
                   jax.ShapeDtypeStruct((bsz, t * DIFF_HEADS, DIFF_VD), F32)],
        scratch_shapes=[pltpu.VMEM((DIFF_HEADS, t, DIFF_VD), BF16),
                        pltpu.VMEM((DIFF_HEADS, t // block, DIFF_VD + SUM_ROWS, block), BF16),
                        pltpu.VMEM((DIFF_HEADS, DIFF_VD, 2 * block), BF16),
                        pltpu.VMEM((DIFF_HEADS, block, DIFF_VD), F32),
                        pltpu.VMEM((heads_per_iter, 1, 2 * block), F32),
                        *[pltpu.VMEM((n_groups, DIFF_VD + SUM_ROWS, 2 * block), F32)] * heads_per_iter,
                        *[pltpu.VMEM((n_groups, 1, 2 * block), F32)] * heads_per_iter,
                        *[pltpu.VMEM((block, 2 * block + LANES), F32)] * (2 * heads_per_iter),
                        *[pltpu.VMEM((block, 2 * block + LANES), BF16)] * (2 * heads_per_iter)],
        compiler_params=pltpu.CompilerParams(dimension_semantics=("arbitrary", "arbitrary"),
                                             vmem_limit_bytes=VMEM_LIMIT),
        name="diff_prompt",
    )(x, p_all, *consts)


def _diff_sample_project_kernel(x_ref, wq_ref, wk_ref, wv_ref, qt_ref, k_ref, v_ref):
    xb = x_ref[...].astype(BF16)
    qt_ref[...] = (_dot(xb, wq_ref[...]) * (DIFF_DH ** -0.5)).T.astype(BF16)
    _store_token_head_rows(k_ref, (), _dot(xb, wk_ref[...]))
    _store_token_head_rows(v_ref, (), _dot(xb, wv_ref[...]))


def _row_to_col(row, n):
    eye = lax.broadcasted_iota(jnp.int32, (n, n), 0) == lax.broadcasted_iota(jnp.int32, (n, n), 1)
    return jnp.sum(jnp.where(eye, row, 0.0), axis=1, keepdims=True)


def _paged_attn_kernel(pt_ref, qt_ref, kn_ref, vn_ref, lam_ref, bias_ref, bnew_ref, own_ref, ck_hbm, cv_hbm, o_ref,
                       k_ring, v_ring, sems, wq_scr, acc_scr, m_scr, l_scr, *, pages, seq, lam_init):
    b = pl.program_id(0)
    g = pl.program_id(1)
    n_steps = pl.num_programs(0) * pl.num_programs(1)
    step = b * pl.num_programs(1) + g
    ncol = 2 * DIFF_HEADS * seq

    def page_copies(s, slot):
        copies = []
        for i in range(pages):
            page = pt_ref[s * pages + i]
            copies.append(pltpu.make_async_copy(ck_hbm.at[page], k_ring.at[slot, i], sems.at[0, slot]))
            copies.append(pltpu.make_async_copy(cv_hbm.at[page], v_ring.at[slot, i], sems.at[1, slot]))
        return copies

    @pl.when(step == 0)
    def _():
        for ahead in range(PAGE_RING - 1):
            @pl.when(ahead < n_steps)
            def _():
                for c in page_copies(ahead, ahead):
                    c.start()

    nxt = step + PAGE_RING - 1

    @pl.when(nxt < n_steps)
    def _():
        for c in page_copies(nxt, lax.rem(nxt, PAGE_RING)):
            c.start()

    slot = lax.rem(step, PAGE_RING)
    for c in page_copies(step, slot):
        c.wait()
    k_pages = [k_ring.at[slot, i] for i in range(pages)]
    v_pages = [v_ring.at[slot, i] for i in range(pages)]

    @pl.when(g == 0)
    def _():
        local = (b % (LANES // seq)) * seq
        src = lax.broadcasted_iota(jnp.int32, (LANES, ncol), 0)
        col = lax.broadcasted_iota(jnp.int32, (LANES, ncol), 1)
        pick = jnp.where(src == local + col % seq, 1.0, 0.0).astype(BF16)
        rep = _dot(qt_ref[...], pick)
        wq_scr[...] = (rep * own_ref[...]).astype(BF16)
        m_scr[...] = jnp.full(m_scr.shape, MASK_VALUE, F32)
        l_scr[...] = jnp.zeros(l_scr.shape, F32)
        acc_scr[...] = jnp.zeros(acc_scr.shape, F32)

    def flash_step(scores, values):
        m_old = m_scr[...]
        m_new = m_old
        for s in scores:
            m_new = jnp.maximum(m_new, jnp.max(s, axis=0, keepdims=True))
        alpha = jnp.exp(m_old - m_new)
        l_new = alpha * l_scr[...]
        probs = []
        for s in scores:
            pr = jnp.exp(s - m_new)
            l_new = l_new + jnp.sum(pr, axis=0, keepdims=True)
            probs.append(pr.astype(BF16))
        pv = None
        for a in range(0, len(probs), 2):
            t = _dot_ta(jnp.concatenate(probs[a:a + 2], axis=0),
                        jnp.concatenate(values[a:a + 2], axis=0))
            pv = t if pv is None else pv + t
        acc_scr[...] = acc_scr[...] * _row_to_col(alpha, ncol) + pv
        l_scr[...] = l_new
        m_scr[...] = m_new

    def load_page(ref):
        return _load_token_head_rows(ref, (), PAGE_SIZE).astype(BF16)

    wq = wq_scr[...]
    scores, values = [], []
    for i in range(pages):
        s = _dot(load_page(k_pages[i]), wq)
        scores.append(s + bias_ref[0] if i == pages - 1 else s)
        values.append(load_page(v_pages[i]))
    flash_step(scores, values)

    @pl.when(g == pl.num_programs(1) - 1)
    def _():
        pad = jnp.zeros((16 - seq, DIFF_WIDTH), F32)
        kn = jnp.concatenate([_load_token_head_rows(kn_ref, (0,), seq), pad], axis=0).astype(BF16)
        vn = jnp.concatenate([_load_token_head_rows(vn_ref, (0,), seq), pad], axis=0).astype(BF16)
        flash_step([_dot(kn, wq) + bnew_ref[...]], [vn])
        lam = _diff_lambda(lam_ref, lam_init)
        inv_l = _row_to_col(1.0 / l_scr[...], ncol)
        half_rows = DIFF_HEADS * seq
        for h in range(DIFF_HEADS):
            cs = slice(h * DIFF_VD, (h + 1) * DIFF_VD)
            r1 = slice(h * seq, (h + 1) * seq)
            r2 = slice(half_rows + h * seq, half_rows + (h + 1) * seq)
            o_ref[0, :, cs] = acc_scr[r1, cs] * inv_l[r1] - lam * (acc_scr[r2, cs] * inv_l[r2])


def _diff_sample_out_kernel(x_ref, o_ref, p_ref, wg_ref, ng_ref, wout_ref, lng_ref, lnb_ref, wgate_ref, wproj_ref,
                            y_ref, *, lam_init):
    x = x_ref[...]
    gate = _dot(x.astype(BF16), wg_ref[...])
    on = _head_rmsnorm_gate(o_ref[...], gate, ng_ref[...], DIFF_HEADS, DIFF_VD, 1.0 - lam_init)
    f = _dot(on.astype(BF16), wout_ref[...])
    y_ref[...] = _residual_update(x, f, p_ref[...], lng_ref[...], lnb_ref[...], wgate_ref[...], wproj_ref[...])


def diff_layer_sample(x, p_all, layer, cache_k, cache_v, page_table, dw, rel_bias, lam_init, *, pages=8, block=256):
    n, seq, _ = x.shape
    n_pages = page_table.shape[1]
    rows = n * seq
    assert n_pages % pages == 0 and rows % block == 0 and LANES % seq == 0 and seq <= 16
    assert PAGE_SIZE >= MAX_DISTANCE
    x2 = x.reshape(rows, D_MODEL)
    row_spec = lambda width: pl.BlockSpec((block, width), lambda i: (i, 0))
    qt, kn, vn = pl.pallas_call(
        _diff_sample_project_kernel,
        grid=(rows // block,),
        in_specs=[row_spec(D_MODEL)] + [_const_spec((D_MODEL, DIFF_WIDTH))] * 3,
        out_specs=[pl.BlockSpec((DIFF_WIDTH, block), lambda i: (0, i)),
                   pl.BlockSpec((block * DIFF_HEADS, DIFF_VD), lambda i: (i, 0)),
                   pl.BlockSpec((block * DIFF_HEADS, DIFF_VD), lambda i: (i, 0))],
        out_shape=[jax.ShapeDtypeStruct((DIFF_WIDTH, rows), BF16),
                   jax.ShapeDtypeStruct((rows * DIFF_HEADS, DIFF_VD), F32),
                   jax.ShapeDtypeStruct((rows * DIFF_HEADS, DIFF_VD), F32)],
        compiler_params=pltpu.CompilerParams(dimension_semantics=("arbitrary",), vmem_limit_bytes=VMEM_LIMIT),
        name="diff_sample_project",
    )(x2, dw["wq"], dw["wk"], dw["wv"])

    ncol = 2 * DIFF_HEADS * seq
    col = jnp.arange(ncol, dtype=jnp.int32)[None, :]
    col_h = (col % (DIFF_HEADS * seq)) // seq
    col_t = col % seq
    kk = jnp.arange(PAGE_SIZE, dtype=jnp.int32)[:, None]
    tk = jnp.arange(16, dtype=jnp.int32)[:, None]
    dist_last = PAGE_SIZE + col_t - kk
    dist_new = col_t - tk
    far = rel_bias.astype(F32)[N_BUCKETS - 1].reshape(DIFF_HEADS, 1, 1)

    def own_head(per_head):
        return sum(jnp.where(col_h == h, per_head[h], 0.0) for h in range(DIFF_HEADS))

    bias_last = own_head(_t5_bias(rel_bias, dist_last) - far)
    bias_pages = jnp.stack([jnp.zeros_like(bias_last), bias_last])
    bias_new = jnp.where((tk < seq) & (dist_new >= 0), own_head(_t5_bias(rel_bias, dist_new) - far), MASK_VALUE)
    feat = jnp.arange(DIFF_WIDTH, dtype=jnp.int32)[:, None]
    own = ((feat // DIFF_VD == col_h) & ((feat % DIFF_VD) // DIFF_DH == col // (DIFF_HEADS * seq))).astype(F32)

    n_groups = n_pages // pages
    seq_per_blk = LANES // seq
    page_shape = (PAGE_RING, pages, PAGE_SIZE * DIFF_HEADS, DIFF_VD)

    tok_spec = pl.BlockSpec((1, seq, DIFF_WIDTH), lambda b, g, pt: (b, 0, 0))
    new_spec = pl.BlockSpec((1, seq * DIFF_HEADS, DIFF_VD), lambda b, g, pt: (b, 0, 0))
    grid_spec = pltpu.PrefetchScalarGridSpec(
        num_scalar_prefetch=1,
        grid=(n, n_groups),
        in_specs=[pl.BlockSpec((DIFF_WIDTH, LANES), lambda b, g, pt: (0, b // seq_per_blk)),
                  new_spec, new_spec,
                  pl.BlockSpec((4, DIFF_DH), lambda b, g, pt: (0, 0)),
                  pl.BlockSpec((1, PAGE_SIZE, ncol), lambda b, g, pt: ((g + 1) // n_groups, 0, 0)),
                  pl.BlockSpec((16, ncol), lambda b, g, pt: (0, 0)),
                  pl.BlockSpec((DIFF_WIDTH, ncol), lambda b, g, pt: (0, 0)),
                  pl.BlockSpec(memory_space=pl.ANY), pl.BlockSpec(memory_space=pl.ANY)],
        out_specs=tok_spec,
        scratch_shapes=[pltpu.VMEM(page_shape, F32), pltpu.VMEM(page_shape, F32),
                        pltpu.SemaphoreType.DMA((2, PAGE_RING)),
                        pltpu.VMEM((DIFF_WIDTH, ncol), BF16),
                        pltpu.VMEM((ncol, DIFF_WIDTH), F32),
                        pltpu.VMEM((1, ncol), F32),
                        pltpu.VMEM((1, ncol), F32)])
    o = pl.pallas_call(
        functools.partial(_paged_attn_kernel, pages=pages, seq=seq, lam_init=lam_init),
        grid_spec=grid_spec,
        out_shape=jax.ShapeDtypeStruct((n, seq, DIFF_WIDTH), F32),
        compiler_params=pltpu.CompilerParams(dimension_semantics=("arbitrary", "arbitrary"),
                                             vmem_limit_bytes=VMEM_LIMIT),
        name="diff_sample_attn",
    )(page_table.reshape(-1), qt, kn.reshape(n, seq * DIFF_HEADS, DIFF_VD), vn.reshape(n, seq * DIFF_HEADS, DIFF_VD),
      dw["lam"], bias_pages, bias_new, own, cache_k, cache_v)

    consts = [dw["wg"], dw["ng"], dw["wout"], dw["lng"], dw["lnb"], dw["wgate"], dw["wproj"]]
    y = pl.pallas_call(
        functools.partial(_diff_sample_out_kernel, lam_init=lam_init),
        grid=(rows // block,),
        in_specs=[row_spec(D_MODEL), row_spec(DIFF_WIDTH),
                  pl.BlockSpec((None, block, PLE_DIM), lambda i: (layer, i, 0))] + [_const_spec(c.shape) for c in consts],
        out_specs=row_spec(D_MODEL),
        out_shape=jax.ShapeDtypeStruct((rows, D_MODEL), F32),
        compiler_params=pltpu.CompilerParams(dimension_semantics=("arbitrary",), vmem_limit_bytes=VMEM_LIMIT),
        name="diff_sample_out",
    )(x2, o.reshape(rows, DIFF_WIDTH), p_all.reshape(-1, rows, PLE_DIM), *consts)
    return y.reshape(n, seq, D_MODEL), kn, vn


def kernel(x_prompt, x_sample, state_gla, cache_k, cache_v, page_table, p_prompt, p_sample, rel_bias,
           gla_w_in, gla_w_a2, gla_b_a, gla_norm_g, gla_w_out,
           diff_w_in, diff_lam_q1, diff_lam_k1, diff_lam_q2, diff_lam_k2, diff_norm_g, diff_w_out,
           ln_g, ln_b, ple_w_proj, ple_w_gate):
    w0 = _gla_weights(gla_w_in[0], gla_w_a2[0], gla_b_a[0], gla_norm_g[0], gla_w_out[0],
                      ln_g[0], ln_b[0], ple_w_gate[0], ple_w_proj[0])
    xp1, sp = gla_layer_prompt(x_prompt, p_prompt, 0, w0)
    xs1, ss = gla_layer_sample(x_sample, p_sample, 0, state_gla[0], w0)
    lam_init = 0.8 - 0.6 * math.exp(-0.3 * 1)
    dw = _diff_weights(diff_w_in[0], diff_lam_q1[0], diff_lam_k1[0], diff_lam_q2[0], diff_lam_k2[0],
                       diff_norm_g[0], diff_w_out[0], ln_g[1], ln_b[1], ple_w_gate[1], ple_w_proj[1])
    yp, kp, vp = diff_layer_prompt(xp1, p_prompt, 1, dw, rel_bias, lam_init)
    pool = cache_k.shape[1]
    ys, ks, vs = diff_layer_sample(xs1, p_sample, 1,
                                   cache_k[0].reshape(pool, PAGE_SIZE * DIFF_HEADS, DIFF_VD),
                                   cache_v[0].reshape(pool, PAGE_SIZE * DIFF_HEADS, DIFF_VD),
                                   page_table, dw, rel_bias, lam_init)
    bsz, t, _ = x_prompt.shape
    n, seq, _ = x_sample.shape
    heads = (DIFF_HEADS, DIFF_VD)
    return (yp, ys, sp[None], ss[None],
            kp.reshape(1, bsz, t, *heads), vp.reshape(1, bsz, t, *heads),
            ks.reshape(1, n, seq, *heads), vs.reshape(1, n, seq, *heads))
```

```python
import functools
import math

import jax
import jax.numpy as jnp
from jax import lax
from jax.experimental import pallas as pl
from jax.experimental.pallas import tpu as pltpu

F32 = jnp.float32
BF16 = jnp.bfloat16

D_MODEL = 1024
DEPTH = 2
GLA_HEADS = 4
GLA_DK = 128
GLA_DV = 256
GLA_HK = GLA_HEADS * GLA_DK
GLA_HV = GLA_HEADS * GLA_DV
GLA_LOWRANK = 16
GLA_TAU = 16.0
DIFF_HEADS = 8
DIFF_DH = 64
DIFF_VD = 128
DIFF_WIDTH = DIFF_HEADS * DIFF_VD
N_BUCKETS = 32
MAX_DISTANCE = 128
PLE_DIM = 256
PAGE_SIZE = 128
ALPHA = (2 * DEPTH) ** 0.25
EPS = 1e-5

LANES = 128
GLA_CHUNK = 64
VMEM_LIMIT = 56 * 1024 * 1024
MASK_VALUE = -1e30
LOG2_E = math.log2(math.e)
SUM_ROWS = 16
PAGE_RING = 4


def _dot(a, b):
    return jnp.dot(a, b, preferred_element_type=F32)


def _dot_tb(a, b):
    return lax.dot_general(a, b, (((1,), (1,)), ((), ())), preferred_element_type=F32)


def _dot_ta(a, b):
    return lax.dot_general(a, b, (((0,), (0,)), ((), ())), preferred_element_type=F32)


def _split_bf16(x):
    hi = x.astype(BF16)
    mid = (x - hi.astype(F32)).astype(BF16)
    return hi, mid


def _log_sigmoid(z):
    return jnp.minimum(z, 0.0) - jnp.log(1.0 + jnp.exp(-jnp.abs(z)))


def _sigmoid(z):
    return 1.0 / (1.0 + jnp.exp(-z))


def _head_rmsnorm_gate(o, gate, norm_g, n_heads, head_dim, scale):
    parts = []
    for h in range(n_heads):
        oh = o[:, h * head_dim:(h + 1) * head_dim]
        ms = jnp.mean(oh * oh, axis=-1, keepdims=True)
        parts.append(oh * lax.rsqrt(ms + EPS))
    on = jnp.concatenate(parts, axis=-1) * norm_g
    if scale != 1.0:
        on = on * scale
    return on * (gate * _sigmoid(gate))


def _residual_update(x, f, p, ln_g, ln_b, w_gate, w_proj):
    hp = ALPHA * x + f
    mu = jnp.mean(hp, axis=-1, keepdims=True)
    hc = hp - mu
    var = jnp.mean(hc * hc, axis=-1, keepdims=True)
    h = hc * lax.rsqrt(var + EPS) * ln_g + ln_b
    gate = _sigmoid(_dot(h.astype(BF16), w_gate))
    return h + gate * _dot(p.astype(BF16), w_proj)


def _gla_project(xb, wq, wk, wv, wg, wa, wa2, ba):
    q = _dot(xb, wq) * (GLA_DK ** -0.5)
    k = _dot(xb, wk)
    v = _dot(xb, wv)
    g = _dot(xb, wg)
    a_lr = _dot(xb, wa)
    z = _dot(a_lr.astype(BF16), wa2) + ba
    log_a = _log_sigmoid(z) * (1.0 / GLA_TAU)
    return q, k, v, g, log_a


def _chunk_decay_matrices(rows, chunk):
    half = chunk // 2
    i = lax.broadcasted_iota(jnp.int32, (rows, rows), 0)
    j = lax.broadcasted_iota(jnp.int32, (rows, rows), 1)
    same = (i // chunk) == (j // chunk)
    jl = j % chunk
    il = i % chunk
    pos = same & (jl >= half) & (jl <= il)
    neg = same & (jl < half) & (jl > il)
    rel = jnp.where(pos, 1.0, jnp.where(neg, -1.0, 0.0)).astype(BF16)
    n_sel = max(8, 2 * rows // chunk)
    s = lax.broadcasted_iota(jnp.int32, (n_sel, rows), 0)
    t = lax.broadcasted_iota(jnp.int32, (n_sel, rows), 1)
    halves = jnp.where((t // half) == s, 1.0, 0.0).astype(BF16)
    return rel, halves


def _gla_prompt_kernel(x_ref, p_ref, wq_ref, wk_ref, wv_ref, wg_ref, wa_ref, wa2_ref, ba_ref, ng_ref,
                       wout_ref, lng_ref, lnb_ref, wgate_ref, wproj_ref,
                       y_ref, st_ref, s_scr, o_scr, *, chunk):
    blk = pl.program_id(1)
    rows = o_scr.shape[1]

    @pl.when(blk == 0)
    def _():
        s_scr[...] = jnp.zeros_like(s_scr)

    ci = lax.broadcasted_iota(jnp.int32, (rows, rows), 0)
    cj = lax.broadcasted_iota(jnp.int32, (rows, rows), 1)
    causal = (ci >= cj) & ((ci // chunk) == (cj // chunk))
    rel, halves = _chunk_decay_matrices(rows, chunk)
    for sb in range(x_ref.shape[1] // rows):
        tok = slice(sb * rows, (sb + 1) * rows)
        x = x_ref[0, tok, :]
        xb = x.astype(BF16)
        q, k, v, g, log_a = _gla_project(xb, wq_ref[...], wk_ref[...], wv_ref[...], wg_ref[...],
                                         wa_ref[...], wa2_ref[...], ba_ref[...])
        la_hi, la_mid = _split_bf16(log_a)
        d = _dot(rel, la_hi) + _dot(rel, la_mid)
        hs = _dot(halves, la_hi) + _dot(halves, la_mid)
        ehs = jnp.exp(hs)
        q_dec = (q * jnp.exp(d)).astype(BF16)
        k_inv = (k * jnp.exp(-d)).astype(BF16)
        vb = v.astype(BF16)

        for h in range(GLA_HEADS):
            ks = slice(h * GLA_DK, (h + 1) * GLA_DK)
            vs = slice(h * GLA_DV, (h + 1) * GLA_DV)
            att = jnp.where(causal, _dot_tb(q_dec[:, ks], k_inv[:, ks]), 0.0).astype(BF16)
            o_scr[sb, :, vs] = _dot(att, vb[:, vs])

        for c in range(rows // chunk):
            r0 = c * chunk
            for h in range(GLA_HEADS):
                ks = slice(h * GLA_DK, (h + 1) * GLA_DK)
                vs = slice(h * GLA_DV, (h + 1) * GLA_DV)
                qd = q_dec[r0:r0 + chunk, ks]
                ki = k_inv[r0:r0 + chunk, ks]
                vh = vb[r0:r0 + chunk, vs]
                e_first = ehs[2 * c:2 * c + 1, ks]
                e_second = ehs[2 * c + 1:2 * c + 2, ks]
                s_mid = s_scr[h] * e_first
                o_scr[sb, r0:r0 + chunk, vs] += _dot_tb(qd, s_mid.astype(BF16))
                s_scr[h] = (s_mid + _dot_ta(vh, ki)) * e_second

        on = _head_rmsnorm_gate(o_scr[sb], g, ng_ref[...], GLA_HEADS, GLA_DV, 1.0)
        f = _dot(on.astype(BF16), wout_ref[...])
        y_ref[0, tok, :] = _residual_update(x, f, p_ref[0, tok, :], lng_ref[...], lnb_ref[...],
                                            wgate_ref[...], wproj_ref[...])

    @pl.when(blk == pl.num_programs(1) - 1)
    def _():
        for h in range(GLA_HEADS):
            st_ref[0, h] = s_scr[h].T


def _const_spec(shape):
    nd = len(shape)
    return pl.BlockSpec(shape, lambda *_: (0,) * nd)


def _gla_weights(w_in, w_a2, b_a, norm_g, w_out, ln_g, ln_b, w_gate, w_proj):
    hk, hv = GLA_HK, GLA_HV
    wq = w_in[:, :hk].astype(BF16)
    wk = w_in[:, hk:2 * hk].astype(BF16)
    wv = w_in[:, 2 * hk:2 * hk + hv].astype(BF16)
    wg = w_in[:, 2 * hk + hv:2 * hk + 2 * hv].astype(BF16)
    wa = jnp.pad(w_in[:, 2 * hk + 2 * hv:], ((0, 0), (0, LANES - GLA_LOWRANK))).astype(BF16)
    wa2 = jnp.pad(w_a2, ((0, LANES - GLA_LOWRANK), (0, 0))).astype(BF16)
    return (wq, wk, wv, wg, wa, wa2, b_a.reshape(1, hk), norm_g.reshape(1, hv),
            w_out.astype(BF16), ln_g.reshape(1, D_MODEL), ln_b.reshape(1, D_MODEL),
            w_gate.astype(BF16), w_proj.astype(BF16))


def gla_layer_prompt(x, p_all, layer, weights, *, block=1024, sub_block=256, chunk=GLA_CHUNK):
    bsz, t, _ = x.shape
    assert t % block == 0 and block % sub_block == 0 and sub_block % chunk == 0
    in_specs = [pl.BlockSpec((1, block, D_MODEL), lambda b, i: (b, i, 0)),
                pl.BlockSpec((None, 1, block, PLE_DIM), lambda b, i: (layer, b, i, 0))]
    in_specs += [_const_spec(w.shape) for w in weights]
    return pl.pallas_call(
        functools.partial(_gla_prompt_kernel, chunk=chunk),
        grid=(bsz, t // block),
        in_specs=in_specs,
        out_specs=[pl.BlockSpec((1, block, D_MODEL), lambda b, i: (b, i, 0)),
                   pl.BlockSpec((1, GLA_HEADS, GLA_DK, GLA_DV), lambda b, i: (b, 0, 0, 0))],
        out_shape=[jax.ShapeDtypeStruct((bsz, t, D_MODEL), F32),
                   jax.ShapeDtypeStruct((bsz, GLA_HEADS, GLA_DK, GLA_DV), F32)],
        scratch_shapes=[pltpu.VMEM((GLA_HEADS, GLA_DV, GLA_DK), F32),
                        pltpu.VMEM((block // sub_block, sub_block, GLA_HV), F32)],
        compiler_params=pltpu.CompilerParams(dimension_semantics=("arbitrary", "arbitrary"),
                                             vmem_limit_bytes=VMEM_LIMIT),
        name="gla_prompt",
    )(x, p_all, *weights)


def _gla_sample_kernel(x_ref, p_ref, s0_ref, wq_ref, wk_ref, wv_ref, wg_ref, wa_ref, wa2_ref, ba_ref, ng_ref,
                       wout_ref, lng_ref, lnb_ref, wgate_ref, wproj_ref,
                       y_ref, st_ref, o_scr, *, seq):
    rows = x_ref.shape[0]
    n_seq = rows // seq
    half = seq // 2
    x = x_ref[...]
    xb = x.astype(BF16)
    q, k, v, g, log_a = _gla_project(xb, wq_ref[...], wk_ref[...], wv_ref[...], wg_ref[...],
                                     wa_ref[...], wa2_ref[...], ba_ref[...])
    rel, _ = _chunk_decay_matrices(rows, seq)
    la_hi, la_mid = _split_bf16(log_a)
    d = _dot(rel, la_hi) + _dot(rel, la_mid)
    s_i = lax.broadcasted_iota(jnp.int32, (n_seq, rows), 0)
    t_i = lax.broadcasted_iota(jnp.int32, (n_seq, rows), 1)
    in_seq = (t_i // seq) == s_i
    sel_first = jnp.where(in_seq & ((t_i % seq) < half), 1.0, 0.0).astype(BF16)
    sel_second = jnp.where(in_seq & ((t_i % seq) >= half), 1.0, 0.0).astype(BF16)
    t_c = lax.broadcasted_iota(jnp.int32, (rows, n_seq), 0)
    s_c = lax.broadcasted_iota(jnp.int32, (rows, n_seq), 1)
    sel_tot_t = jnp.where((t_c // seq) == s_c, 1.0, 0.0).astype(BF16)
    e_first = jnp.exp(_dot(sel_first, la_hi) + _dot(sel_first, la_mid))
    e_second = jnp.exp(_dot(sel_second, la_hi) + _dot(sel_second, la_mid))
    e_tot_col = jnp.exp(_dot_ta(la_hi, sel_tot_t) + _dot_ta(la_mid, sel_tot_t))
    q_dec = q * jnp.exp(d)
    k_inv = k * jnp.exp(-d)

    ci = lax.broadcasted_iota(jnp.int32, (rows, rows), 0)
    cj = lax.broadcasted_iota(jnp.int32, (rows, rows), 1)
    causal = (ci >= cj) & ((ci // seq) == (cj // seq))
    q_dec_b = q_dec.astype(BF16)
    k_inv_b = k_inv.astype(BF16)
    for h in range(GLA_HEADS):
        ks = slice(h * GLA_DK, (h + 1) * GLA_DK)
        vs = slice(h * GLA_DV, (h + 1) * GLA_DV)
        att = jnp.where(causal, _dot_tb(q_dec_b[:, ks], k_inv_b[:, ks]), 0.0).astype(BF16)
        o_scr[:, vs] = _dot(att, v[:, vs].astype(BF16))

    for c in range(n_seq):
        r0 = c * seq
        for h in range(GLA_HEADS):
            ks = slice(h * GLA_DK, (h + 1) * GLA_DK)
            vs = slice(h * GLA_DV, (h + 1) * GLA_DV)
            qd = q_dec[r0:r0 + seq, ks]
            ki = k_inv[r0:r0 + seq, ks]
            vh = v[r0:r0 + seq, vs].astype(BF16)
            s_old = s0_ref[c, h]
            q_mid = (qd * e_first[c:c + 1, ks]).astype(BF16)
            o_scr[r0:r0 + seq, vs] += _dot(q_mid, s_old.astype(BF16))
            k_end = (ki * e_second[c:c + 1, ks]).astype(BF16)
            st_ref[c, h] = s_old * e_tot_col[h * GLA_DK:(h + 1) * GLA_DK, c:c + 1] + _dot_ta(k_end, vh)

    on = _head_rmsnorm_gate(o_scr[...], g, ng_ref[...], GLA_HEADS, GLA_DV, 1.0)
    f = _dot(on.astype(BF16), wout_ref[...])
    y_ref[...] = _residual_update(x, f, p_ref[...], lng_ref[...], lnb_ref[...], wgate_ref[...], wproj_ref[...])


def gla_layer_sample(x, p_all, layer, s0, weights, *, group=16):
    n, seq, _ = x.shape
    assert n % group == 0 and seq % 2 == 0
    rows = group * seq
    in_specs = [pl.BlockSpec((rows, D_MODEL), lambda i: (i, 0)),
                pl.BlockSpec((None, rows, PLE_DIM), lambda i: (layer, i, 0)),
                pl.BlockSpec((group, GLA_HEADS, GLA_DK, GLA_DV), lambda i: (i, 0, 0, 0))]
    in_specs += [_resident_spec(w.shape) for w in weights]
    y, st = pl.pallas_call(
        functools.partial(_gla_sample_kernel, seq=seq),
        grid=(n // group,),
        in_specs=in_specs,
        out_specs=[pl.BlockSpec((rows, D_MODEL), lambda i: (i, 0)),
                   pl.BlockSpec((group, GLA_HEADS, GLA_DK, GLA_DV), lambda i: (i, 0, 0, 0))],
        out_shape=[jax.ShapeDtypeStruct((n * seq, D_MODEL), F32),
                   jax.ShapeDtypeStruct((n, GLA_HEADS, GLA_DK, GLA_DV), F32)],
        scratch_shapes=[pltpu.VMEM((rows, GLA_HV), F32)],
        compiler_params=pltpu.CompilerParams(dimension_semantics=("arbitrary",),
                                             vmem_limit_bytes=VMEM_LIMIT),
        name="gla_sample",
    )(x.reshape(n * seq, D_MODEL), p_all.reshape(-1, n * seq, PLE_DIM), s0, *weights)
    return y.reshape(n, seq, D_MODEL), st


def _resident_spec(shape):
    nd = len(shape)
    return pl.BlockSpec(shape, lambda *_: (0,) * nd, pipeline_mode=pl.Buffered(1))


def _t5_bias(rel_bias, dist):
    max_exact = N_BUCKETS // 2
    n = jnp.maximum(dist, 0)[None]
    nf = jnp.maximum(n, 1).astype(F32)
    steps = jnp.log(nf / max_exact) / math.log(MAX_DISTANCE / max_exact) * (N_BUCKETS - max_exact)
    table = rel_bias.astype(F32)
    per_head = (DIFF_HEADS,) + (1,) * dist.ndim
    out = jnp.broadcast_to(table[N_BUCKETS - 1].reshape(per_head), (DIFF_HEADS,) + dist.shape)
    for b in range(N_BUCKETS - 2, max_exact - 1, -1):
        out = jnp.where(steps < (b + 1 - max_exact), table[b].reshape(per_head), out)
    for b in range(max_exact):
        out = jnp.where(n == b, table[b].reshape(per_head), out)
    return out


def _store_token_head_rows(ref, lead, x):
    tokens = x.shape[0]
    for h in range(DIFF_HEADS):
        ref[(*lead, pl.ds(h, tokens, stride=DIFF_HEADS), slice(None))] = x[:, h * DIFF_VD:(h + 1) * DIFF_VD]


def _load_token_head_rows(ref, lead, tokens):
    return jnp.concatenate(
        [ref[(*lead, pl.ds(h, tokens, stride=DIFF_HEADS), slice(None))] for h in range(DIFF_HEADS)], axis=1)


def _diff_lambda(lam_ref, lam_init):
    lv = lam_ref[...]
    a = jnp.sum(lv[0:1] * lv[1:2], axis=-1, keepdims=True)
    b = jnp.sum(lv[2:3] * lv[3:4], axis=-1, keepdims=True)
    return jnp.exp(a) - jnp.exp(b) + lam_init


def _diff_prompt_kernel(x_ref, p_ref, wq_ref, wk_ref, wv_ref, wg_ref, lam_ref, bias_ref, ng_ref,
                        wout_ref, lng_ref, lnb_ref, wgate_ref, wproj_ref,
                        y_ref, kout_ref, vout_ref,
                        k_scr, vt_scr, q_scr, o_scr, alpha_scr, *bufs,
                        lam_init, heads_per_iter):
    i = pl.program_id(1)
    tq = x_ref.shape[1]
    x = x_ref[0]
    xb = x.astype(BF16)
    q = _dot(xb, wq_ref[...]) * (DIFF_DH ** -0.5 * LOG2_E)
    k = _dot(xb, wk_ref[...])
    v = _dot(xb, wv_ref[...])
    _store_token_head_rows(kout_ref, (0,), k)
    _store_token_head_rows(vout_ref, (0,), v)
    first_half = lax.broadcasted_iota(jnp.int32, (DIFF_VD, tq), 0) < DIFF_DH
    row0 = pl.multiple_of(i * tq, tq)
    for h in range(DIFF_HEADS):
        hs = slice(h * DIFF_VD, (h + 1) * DIFF_VD)
        qh_t = q[:, hs].T
        q_scr[h, :, 0:tq] = jnp.where(first_half, qh_t, 0.0).astype(BF16)
        q_scr[h, :, tq:2 * tq] = jnp.where(first_half, 0.0, qh_t).astype(BF16)
        k_scr[h, pl.ds(row0, tq), :] = k[:, hs].astype(BF16)
        vt_scr[h, i, 0:DIFF_VD, :] = v[:, hs].T.astype(BF16)
        vt_scr[h, i, DIFF_VD:, :] = jnp.ones((SUM_ROWS, tq), BF16)
    lam = _diff_lambda(lam_ref, lam_init)

    hpi = heads_per_iter
    n_groups = DIFF_HEADS // hpi
    acc_scr = bufs[0:hpi]
    m_scr = bufs[hpi:2 * hpi]
    s_buf = [bufs[2 * hpi + 2 * u:2 * hpi + 2 * u + 2] for u in range(hpi)]
    p_buf = [bufs[4 * hpi + 2 * u:4 * hpi + 2 * u + 2] for u in range(hpi)]

    n_pairs = (i + 2) // 2
    steps_per_group = 2 * n_pairs

    def block_of(t):
        near = jnp.maximum(i - t, 0)
        if isinstance(t, int):
            return near if t < 2 else jnp.minimum(t - 2, i)
        return jnp.where(t < 2, near, jnp.minimum(t - 2, i))

    def scores(g, t, slot):
        rows = pl.ds(pl.multiple_of(block_of(t) * tq, tq), tq)
        for u in range(hpi):
            h = g * hpi + u
            s_buf[u][slot][:, 0:2 * tq] = _dot(k_scr[h, rows, :], q_scr[h])

    def softmax(g, t, slot, bias_slot):
        valid = t <= i
        for u in range(hpi):
            s = s_buf[u][slot][:, 0:2 * tq]
            if bias_slot is not None:
                tile = bias_ref[g * hpi + u, bias_slot]
                s = s + jnp.concatenate([tile, tile], axis=1)
            m_old = m_scr[u][g]
            m_new = jnp.where(valid, jnp.maximum(m_old, jnp.max(s, axis=0, keepdims=True)), m_old)
            alpha = jnp.exp2(m_old - m_new)
            pr = jnp.exp2(s - jnp.where(valid, m_new, -MASK_VALUE))
            m_scr[u][g] = m_new
            alpha_scr[u] = alpha
            p_buf[u][slot][:, 0:2 * tq] = pr.astype(BF16)

    def weighted_values(g, t, slot):
        j = block_of(t)
        for u in range(hpi):
            acc_scr[u][g] = (acc_scr[u][g] * alpha_scr[u]
                             + _dot(vt_scr[g * hpi + u, j], p_buf[u][slot][:, 0:2 * tq]))

    def pair(g, r, biased):
        first, last = r == 0, r == n_pairs - 1
        weighted_values(jnp.where(first, jnp.maximum(g - 1, 0), g),
                        jnp.where(first, steps_per_group - 1, 2 * r - 1), 1)
        scores(g, 2 * r + 1, 1)
        softmax(g, 2 * r, 0, 0 if biased else None)
        weighted_values(g, 2 * r, 0)
        scores(jnp.where(last, jnp.minimum(g + 1, n_groups - 1), g), jnp.where(last, 0, 2 * r + 2), 0)
        softmax(g, 2 * r + 1, 1, 1 if biased else None)

    for u in range(hpi):
        m_scr[u][...] = jnp.full(m_scr[u].shape, MASK_VALUE, F32)
        acc_scr[u][...] = jnp.zeros(acc_scr[u].shape, F32)
        p_buf[u][1][...] = jnp.zeros(p_buf[u][1].shape, BF16)
    alpha_scr[...] = jnp.ones(alpha_scr.shape, F32)
    scores(0, 0, 0)

    def two_pairs(d, carry):
        located = []
        for n in (2 * d, 2 * d + 1):
            g = n // n_pairs
            located.append((g, n - g * n_pairs))
        (g0, r0), (g1, r1) = located
        for near0 in (True, False):
            for near1 in (True, False):
                @pl.when(((r0 == 0) == near0) & ((r1 == 0) == near1))
                def _(near0=near0, near1=near1):
                    pair(g0, r0, near0)
                    pair(g1, r1, near1)
        return carry

    assert n_groups % 2 == 0
    lax.fori_loop(0, n_groups * n_pairs // 2, two_pairs, 0)
    weighted_values(n_groups - 1, steps_per_group - 1, 1)

    for h in range(DIFF_HEADS):
        g, u = divmod(h, hpi)
        inv_l = 1.0 / acc_scr[u][g, DIFF_VD:DIFF_VD + 1, :]
        acc = acc_scr[u][g, 0:DIFF_VD, :]
        o_t = acc[:, :tq] * inv_l[:, :tq] - lam * (acc[:, tq:] * inv_l[:, tq:])
        o_scr[h] = o_t.T
    o = jnp.concatenate([o_scr[h] for h in range(DIFF_HEADS)], axis=-1)
    gate = _dot(xb, wg_ref[...])
    on = _head_rmsnorm_gate(o, gate, ng_ref[...], DIFF_HEADS, DIFF_VD, 1.0 - lam_init)
    f = _dot(on.astype(BF16), wout_ref[...])
    y_ref[0] = _residual_update(x, f, p_ref[0], lng_ref[...], lnb_ref[...], wgate_ref[...], wproj_ref[...])


def _diff_weights(w_in, lam_q1, lam_k1, lam_q2, lam_k2, norm_g, w_out, ln_g, ln_b, w_gate, w_proj):
    w = DIFF_WIDTH
    return dict(
        wq=w_in[:, :w].astype(BF16), wk=w_in[:, w:2 * w].astype(BF16),
        wv=w_in[:, 2 * w:3 * w].astype(BF16), wg=w_in[:, 3 * w:].astype(BF16),
        lam=jnp.stack([lam_q1, lam_k1, lam_q2, lam_k2]).astype(F32),
        ng=norm_g.reshape(1, w), wout=w_out.astype(BF16),
        lng=ln_g.reshape(1, D_MODEL), lnb=ln_b.reshape(1, D_MODEL),
        wgate=w_gate.astype(BF16), wproj=w_proj.astype(BF16))


def diff_layer_prompt(x, p_all, layer, dw, rel_bias, lam_init, *, block=256, heads_per_iter=2):
    bsz, t, _ = x.shape
    assert t % block == 0 and block >= MAX_DISTANCE and DIFF_HEADS % heads_per_iter == 0
    n_groups = DIFF_HEADS // heads_per_iter
    kk = jnp.arange(block, dtype=jnp.int32)[:, None]
    qq = jnp.arange(block, dtype=jnp.int32)[None, :]
    dist = jnp.stack([qq - kk, block + qq - kk])
    far = rel_bias.astype(F32)[N_BUCKETS - 1].reshape(DIFF_HEADS, 1, 1, 1)
    bias = jnp.where(dist >= 0, (_t5_bias(rel_bias, dist) - far) * LOG2_E, MASK_VALUE)
    tok = lambda width: pl.BlockSpec((1, block, width), lambda b, i: (b, i, 0))
    kv_spec = pl.BlockSpec((1, block * DIFF_HEADS, DIFF_VD), lambda b, i: (b, i, 0))
    consts = [dw["wq"], dw["wk"], dw["wv"], dw["wg"], dw["lam"], bias, dw["ng"], dw["wout"],
              dw["lng"], dw["lnb"], dw["wgate"], dw["wproj"]]
    return pl.pallas_call(
        functools.partial(_diff_prompt_kernel, lam_init=lam_init, heads_per_iter=heads_per_iter),
        grid=(bsz, t // block),
        in_specs=[tok(D_MODEL), pl.BlockSpec((None, 1, block, PLE_DIM), lambda b, i: (layer, b, i, 0))]
                 + [_resident_spec(c.shape) for c in consts],
        out_specs=[tok(D_MODEL), kv_spec, kv_spec],
        out_shape=[jax.ShapeDtypeStruct((bsz, t, D_MODEL), F32),
                   jax.ShapeDtypeStruct((bsz, t * DIFF_HEADS, DIFF_VD), F32),
                   jax.ShapeDtypeStruct((bsz, t * DIFF_HEADS, DIFF_VD), F32)],
        scratch_shapes=[pltpu.VMEM((DIFF_HEADS, t, DIFF_VD), BF16),
                        pltpu.VMEM((DIFF_HEADS, t // block, DIFF_VD + SUM_ROWS, block), BF16),
                        pltpu.VMEM((DIFF_HEADS, DIFF_VD, 2 * block), BF16),
                        pltpu.VMEM((DIFF_HEADS, block, DIFF_VD), F32),
                        pltpu.VMEM((heads_per_iter, 1, 2 * block), F32),
                        *[pltpu.VMEM((n_groups, DIFF_VD + SUM_ROWS, 2 * block), F32)] * heads_per_iter,
                        *[pltpu.VMEM((n_groups, 1, 2 * block), F32)] * heads_per_iter,
                        *[pltpu.VMEM((block, 2 * block + LANES), F32)] * (2 * heads_per_iter),
                        *[pltpu.VMEM((block, 2 * block + LANES), BF16)] * (2 * heads_per_iter)],
        compiler_params=pltpu.CompilerParams(dimension_semantics=("arbitrary", "arbitrary"),
                                             vmem_limit_bytes=VMEM_LIMIT),
        name="diff_prompt",
    )(x, p_all, *consts)


def _diff_sample_project_kernel(x_ref, wq_ref, wk_ref, wv_ref, qt_ref, k_ref, v_ref):
    xb = x_ref[...].astype(BF16)
    qt_ref[...] = (_dot(xb, wq_ref[...]) * (DIFF_DH ** -0.5)).T.astype(BF16)
    _store_token_head_rows(k_ref, (), _dot(xb, wk_ref[...]))
    _store_token_head_rows(v_ref, (), _dot(xb, wv_ref[...]))


def _row_to_col(row, n):
    eye = lax.broadcasted_iota(jnp.int32, (n, n), 0) == lax.broadcasted_iota(jnp.int32, (n, n), 1)
    return jnp.sum(jnp.where(eye, row, 0.0), axis=1, keepdims=True)


def _paged_attn_kernel(pt_ref, qt_ref, kn_ref, vn_ref, lam_ref, bias_ref, bnew_ref, own_ref, ck_hbm, cv_hbm, o_ref,
                       k_ring, v_ring, sems, wq_scr, acc_scr, m_scr, l_scr, *, pages, seq, lam_init):
    b = pl.program_id(0)
    g = pl.program_id(1)
    n_steps = pl.num_programs(0) * pl.num_programs(1)
    step = b * pl.num_programs(1) + g
    ncol = 2 * DIFF_HEADS * seq

    def page_copies(s, slot):
        copies = []
        for i in range(pages):
            page = pt_ref[s * pages + i]
            copies.append(pltpu.make_async_copy(ck_hbm.at[page], k_ring.at[slot, i], sems.at[0, slot]))
            copies.append(pltpu.make_async_copy(cv_hbm.at[page], v_ring.at[slot, i], sems.at[1, slot]))
        return copies

    @pl.when(step == 0)
    def _():
        for ahead in range(PAGE_RING - 1):
            @pl.when(ahead < n_steps)
            def _():
                for c in page_copies(ahead, ahead):
                    c.start()

    nxt = step + PAGE_RING - 1

    @pl.when(nxt < n_steps)
    def _():
        for c in page_copies(nxt, lax.rem(nxt, PAGE_RING)):
            c.start()

    slot = lax.rem(step, PAGE_RING)
    for c in page_copies(step, slot):
        c.wait()
    k_pages = [k_ring.at[slot, i] for i in range(pages)]
    v_pages = [v_ring.at[slot, i] for i in range(pages)]

    @pl.when(g == 0)
    def _():
        local = (b % (LANES // seq)) * seq
        src = lax.broadcasted_iota(jnp.int32, (LANES, ncol), 0)
        col = lax.broadcasted_iota(jnp.int32, (LANES, ncol), 1)
        pick = jnp.where(src == local + col % seq, 1.0, 0.0).astype(BF16)
        rep = _dot(qt_ref[...], pick)
        wq_scr[...] = (rep * own_ref[...]).astype(BF16)
        m_scr[...] = jnp.full(m_scr.shape, MASK_VALUE, F32)
        l_scr[...] = jnp.zeros(l_scr.shape, F32)
        acc_scr[...] = jnp.zeros(acc_scr.shape, F32)

    def flash_step(scores, values):
        m_old = m_scr[...]
        m_new = m_old
        for s in scores:
            m_new = jnp.maximum(m_new, jnp.max(s, axis=0, keepdims=True))
        alpha = jnp.exp(m_old - m_new)
        l_new = alpha * l_scr[...]
        probs = []
        for s in scores:
            pr = jnp.exp(s - m_new)
            l_new = l_new + jnp.sum(pr, axis=0, keepdims=True)
            probs.append(pr.astype(BF16))
        pv = None
        for a in range(0, len(probs), 2):
            t = _dot_ta(jnp.concatenate(probs[a:a + 2], axis=0),
                        jnp.concatenate(values[a:a + 2], axis=0))
            pv = t if pv is None else pv + t
        acc_scr[...] = acc_scr[...] * _row_to_col(alpha, ncol) + pv
        l_scr[...] = l_new
        m_scr[...] = m_new

    def load_page(ref):
        return _load_token_head_rows(ref, (), PAGE_SIZE).astype(BF16)

    wq = wq_scr[...]
    scores, values = [], []
    for i in range(pages):
        s = _dot(load_page(k_pages[i]), wq)
        scores.append(s + bias_ref[0] if i == pages - 1 else s)
        values.append(load_page(v_pages[i]))
    flash_step(scores, values)

    @pl.when(g == pl.num_programs(1) - 1)
    def _():
        pad = jnp.zeros((16 - seq, DIFF_WIDTH), F32)
        kn = jnp.concatenate([_load_token_head_rows(kn_ref, (0,), seq), pad], axis=0).astype(BF16)
        vn = jnp.concatenate([_load_token_head_rows(vn_ref, (0,), seq), pad], axis=0).astype(BF16)
        flash_step([_dot(kn, wq) + bnew_ref[...]], [vn])
        lam = _diff_lambda(lam_ref, lam_init)
        inv_l = _row_to_col(1.0 / l_scr[...], ncol)
        half_rows = DIFF_HEADS * seq
        for h in range(DIFF_HEADS):
            cs = slice(h * DIFF_VD, (h + 1) * DIFF_VD)
            r1 = slice(h * seq, (h + 1) * seq)
            r2 = slice(half_rows + h * seq, half_rows + (h + 1) * seq)
            o_ref[0, :, cs] = acc_scr[r1, cs] * inv_l[r1] - lam * (acc_scr[r2, cs] * inv_l[r2])


def _diff_sample_out_kernel(x_ref, o_ref, p_ref, wg_ref, ng_ref, wout_ref, lng_ref, lnb_ref, wgate_ref, wproj_ref,
                            y_ref, *, lam_init):
    x = x_ref[...]
    gate = _dot(x.astype(BF16), wg_ref[...])
    on = _head_rmsnorm_gate(o_ref[...], gate, ng_ref[...], DIFF_HEADS, DIFF_VD, 1.0 - lam_init)
    f = _dot(on.astype(BF16), wout_ref[...])
    y_ref[...] = _residual_update(x, f, p_ref[...], lng_ref[...], lnb_ref[...], wgate_ref[...], wproj_ref[...])


def diff_layer_sample(x, p_all, layer, cache_k, cache_v, page_table, dw, rel_bias, lam_init, *, pages=8, block=256):
    n, seq, _ = x.shape
    n_pages = page_table.shape[1]
    rows = n * seq
    assert n_pages % pages == 0 and rows % block == 0 and LANES % seq == 0 and seq <= 16
    assert PAGE_SIZE >= MAX_DISTANCE
    x2 = x.reshape(rows, D_MODEL)
    row_spec = lambda width: pl.BlockSpec((block, width), lambda i: (i, 0))
    qt, kn, vn = pl.pallas_call(
        _diff_sample_project_kernel,
        grid=(rows // block,),
        in_specs=[row_spec(D_MODEL)] + [_const_spec((D_MODEL, DIFF_WIDTH))] * 3,
        out_specs=[pl.BlockSpec((DIFF_WIDTH, block), lambda i: (0, i)),
                   pl.BlockSpec((block * DIFF_HEADS, DIFF_VD), lambda i: (i, 0)),
                   pl.BlockSpec((block * DIFF_HEADS, DIFF_VD), lambda i: (i, 0))],
        out_shape=[jax.ShapeDtypeStruct((DIFF_WIDTH, rows), BF16),
                   jax.ShapeDtypeStruct((rows * DIFF_HEADS, DIFF_VD), F32),
                   jax.ShapeDtypeStruct((rows * DIFF_HEADS, DIFF_VD), F32)],
        compiler_params=pltpu.CompilerParams(dimension_semantics=("arbitrary",), vmem_limit_bytes=VMEM_LIMIT),
        name="diff_sample_project",
    )(x2, dw["wq"], dw["wk"], dw["wv"])

    ncol = 2 * DIFF_HEADS * seq
    col = jnp.arange(ncol, dtype=jnp.int32)[None, :]
    col_h = (col % (DIFF_HEADS * seq)) // seq
    col_t = col % seq
    kk = jnp.arange(PAGE_SIZE, dtype=jnp.int32)[:, None]
    tk = jnp.arange(16, dtype=jnp.int32)[:, None]
    dist_last = PAGE_SIZE + col_t - kk
    dist_new = col_t - tk
    far = rel_bias.astype(F32)[N_BUCKETS - 1].reshape(DIFF_HEADS, 1, 1)

    def own_head(per_head):
        return sum(jnp.where(col_h == h, per_head[h], 0.0) for h in range(DIFF_HEADS))

    bias_last = own_head(_t5_bias(rel_bias, dist_last) - far)
    bias_pages = jnp.stack([jnp.zeros_like(bias_last), bias_last])
    bias_new = jnp.where((tk < seq) & (dist_new >= 0), own_head(_t5_bias(rel_bias, dist_new) - far), MASK_VALUE)
    feat = jnp.arange(DIFF_WIDTH, dtype=jnp.int32)[:, None]
    own = ((feat // DIFF_VD == col_h) & ((feat % DIFF_VD) // DIFF_DH == col // (DIFF_HEADS * seq))).astype(F32)

    n_groups = n_pages // pages
    seq_per_blk = LANES // seq
    page_shape = (PAGE_RING, pages, PAGE_SIZE * DIFF_HEADS, DIFF_VD)

    tok_spec = pl.BlockSpec((1, seq, DIFF_WIDTH), lambda b, g, pt: (b, 0, 0))
    new_spec = pl.BlockSpec((1, seq * DIFF_HEADS, DIFF_VD), lambda b, g, pt: (b, 0, 0))
    grid_spec = pltpu.PrefetchScalarGridSpec(
        num_scalar_prefetch=1,
        grid=(n, n_groups),
        in_specs=[pl.BlockSpec((DIFF_WIDTH, LANES), lambda b, g, pt: (0, b // seq_per_blk)),
                  new_spec, new_spec,
                  pl.BlockSpec((4, DIFF_DH), lambda b, g, pt: (0, 0)),
                  pl.BlockSpec((1, PAGE_SIZE, ncol), lambda b, g, pt: ((g + 1) // n_groups, 0, 0)),
                  pl.BlockSpec((16, ncol), lambda b, g, pt: (0, 0)),
                  pl.BlockSpec((DIFF_WIDTH, ncol), lambda b, g, pt: (0, 0)),
                  pl.BlockSpec(memory_space=pl.ANY), pl.BlockSpec(memory_space=pl.ANY)],
        out_specs=tok_spec,
        scratch_shapes=[pltpu.VMEM(page_shape, F32), pltpu.VMEM(page_shape, F32),
                        pltpu.SemaphoreType.DMA((2, PAGE_RING)),
                        pltpu.VMEM((DIFF_WIDTH, ncol), BF16),
                        pltpu.VMEM((ncol, DIFF_WIDTH), F32),
                        pltpu.VMEM((1, ncol), F32),
                        pltpu.VMEM((1, ncol), F32)])
    o = pl.pallas_call(
        functools.partial(_paged_attn_kernel, pages=pages, seq=seq, lam_init=lam_init),
        grid_spec=grid_spec,
        out_shape=jax.ShapeDtypeStruct((n, seq, DIFF_WIDTH), F32),
        compiler_params=pltpu.CompilerParams(dimension_semantics=("arbitrary", "arbitrary"),
                                             vmem_limit_bytes=VMEM_LIMIT),
        name="diff_sample_attn",
    )(page_table.reshape(-1), qt, kn.reshape(n, seq * DIFF_HEADS, DIFF_VD), vn.reshape(n, seq * DIFF_HEADS, DIFF_VD),
      dw["lam"], bias_pages, bias_new, own, cache_k, cache_v)

    consts = [dw["wg"], dw["ng"], dw["wout"], dw["lng"], dw["lnb"], dw["wgate"], dw["wproj"]]
    y = pl.pallas_call(
        functools.partial(_diff_sample_out_kernel, lam_init=lam_init),
        grid=(rows // block,),
        in_specs=[row_spec(D_MODEL), row_spec(DIFF_WIDTH),
                  pl.BlockSpec((None, block, PLE_DIM), lambda i: (layer, i, 0))] + [_const_spec(c.shape) for c in consts],
        out_specs=row_spec(D_MODEL),
        out_shape=jax.ShapeDtypeStruct((rows, D_MODEL), F32),
        compiler_params=pltpu.CompilerParams(dimension_semantics=("arbitrary",), vmem_limit_bytes=VMEM_LIMIT),
        name="diff_sample_out",
    )(x2, o.reshape(rows, DIFF_WIDTH), p_all.reshape(-1, rows, PLE_DIM), *consts)
    return y.reshape(n, seq, D_MODEL), kn, vn


def kernel(x_prompt, x_sample, state_gla, cache_k, cache_v, page_table, p_prompt, p_sample, rel_bias,
           gla_w_in, gla_w_a2, gla_b_a, gla_norm_g, gla_w_out,
           diff_w_in, diff_lam_q1, diff_lam_k1, diff_lam_q2, diff_lam_k2, diff_norm_g, diff_w_out,
           ln_g, ln_b, ple_w_proj, ple_w_gate):
    w0 = _gla_weights(gla_w_in[0], gla_w_a2[0], gla_b_a[0], gla_norm_g[0], gla_w_out[0],
                      ln_g[0], ln_b[0], ple_w_gate[0], ple_w_proj[0])
    xp1, sp = gla_layer_prompt(x_prompt, p_prompt, 0, w0)
    xs1, ss = gla_layer_sample(x_sample, p_sample, 0, state_gla[0], w0)
    lam_init = 0.8 - 0.6 * math.exp(-0.3 * 1)
    dw = _diff_weights(diff_w_in[0], diff_lam_q1[0], diff_lam_k1[0], diff_lam_q2[0], diff_lam_k2[0],
                       diff_norm_g[0], diff_w_out[0], ln_g[1], ln_b[1], ple_w_gate[1], ple_w_proj[1])
    yp, kp, vp = diff_layer_prompt(xp1, p_prompt, 1, dw, rel_bias, lam_init)
    pool = cache_k.shape[1]
    ys, ks, vs = diff_layer_sample(xs1, p_sample, 1,
                                   cache_k[0].reshape(pool, PAGE_SIZE * DIFF_HEADS, DIFF_VD),
                                   cache_v[0].reshape(pool, PAGE_SIZE * DIFF_HEADS, DIFF_VD),
                                   page_table, dw, rel_bias, lam_init)
    bsz, t, _ = x_prompt.shape
    n, seq, _ = x_sample.shape
    heads = (DIFF_HEADS, DIFF_VD)
    return (yp, ys, sp[None], ss[None],
            kp.reshape(1, bsz, t, *heads), vp.reshape(1, bsz, t, *heads),
            ks.reshape(1, n, seq, *heads), vs.reshape(1, n, seq, *heads))
```
